```python
import jax, jax.numpy as jnp
from jax import lax
import numpy as np

D_MODEL = 1024
BATCH = 2
SEQ = 8192
DEPTH = 4
DEC_BATCH = 8
DEC_SEQ = 4096
PAST_LEN = 128

HEAD_DIM = 128
N_HEADS = D_MODEL // HEAD_DIM
N_KV_HEADS = 2
GROUP = N_HEADS // N_KV_HEADS
Q_DIM = N_HEADS * HEAD_DIM
KV_DIM = N_KV_HEADS * HEAD_DIM
QKV_DIM = Q_DIM + 2 * KV_DIM
Q_BLOCK = 128
GRID_W = 64
AXIS_ROT_DIM = HEAD_DIM // 2
ROPE_THETA = 10000.0
N_FGROUPS = 8
FGROUP_DIM = D_MODEL // N_FGROUPS
D_FF = ((8 * D_MODEL + 3 * 256 - 1) // (3 * 256)) * 256
N_FOURIER_LAYERS = (DEPTH + 1) // 2
N_ATTN_LAYERS = DEPTH // 2
EPS = 1e-6

kernel_name = "fnet_gqa_axial_interleaved_encoder"


def rms_norm(x, g):
    xf = x.astype(jnp.float32)
    y = xf * lax.rsqrt(jnp.mean(xf * xf, axis=-1, keepdims=True) + EPS)
    return (y * g.astype(jnp.float32)).astype(x.dtype)


def axial_rope_tables(seq_len):
    rows = seq_len // GRID_W
    row = jnp.repeat(jnp.arange(rows, dtype=jnp.float32), GRID_W)
    col = jnp.tile(jnp.arange(GRID_W, dtype=jnp.float32), rows)
    inv = ROPE_THETA ** (-jnp.arange(0, AXIS_ROT_DIM, 2, dtype=jnp.float32) / AXIS_ROT_DIM)
    ang = jnp.concatenate([row[:, None] * inv, col[:, None] * inv], axis=-1)
    return jnp.cos(ang), jnp.sin(ang)


def apply_rope(x, cos, sin):
    xf = x.astype(jnp.float32).reshape(*x.shape[:-1], HEAD_DIM // 2, 2)
    x0, x1 = xf[..., 0], xf[..., 1]
    c = cos[None, :, None, :]
    s = sin[None, :, None, :]
    out = jnp.stack([x0 * c - x1 * s, x0 * s + x1 * c], axis=-1).reshape(x.shape)
    return out.astype(x.dtype)


def fourier_mixer(x, g, w):
    b, s, d = x.shape
    h = rms_norm(x, g).astype(jnp.float32).reshape(b, s, N_FGROUPS, FGROUP_DIM)
    mixed = jnp.real(jnp.fft.fft2(h, axes=(1, 3), norm="ortho"))
    return mixed.reshape(b, s, d).astype(x.dtype) @ w


def attention_mixer(x, g, w_qkv, q_gain, k_gain, w_o, cos, sin):
    b, s, _ = x.shape
    h = rms_norm(x, g)
    qkv = h @ w_qkv
    q, k, v = jnp.split(qkv, [Q_DIM, Q_DIM + KV_DIM], axis=-1)
    q = apply_rope(rms_norm(q.reshape(b, s, N_HEADS, HEAD_DIM), q_gain), cos, sin)
    k = apply_rope(rms_norm(k.reshape(b, s, N_KV_HEADS, HEAD_DIM), k_gain), cos, sin)
    v = v.reshape(b, s, N_KV_HEADS, HEAD_DIM)
    n_blk = s // Q_BLOCK
    qb = q.reshape(b, n_blk, Q_BLOCK, N_KV_HEADS, GROUP, HEAD_DIM).transpose(1, 0, 2, 3, 4, 5)
    scale = HEAD_DIM ** -0.5

    def block(qi):
        sc = jnp.einsum('bqkgd,bskd->bkgqs', qi, k).astype(jnp.float32) * scale
        p = jax.nn.softmax(sc, axis=-1).astype(v.dtype)
        return jnp.einsum('bkgqs,bskd->bqkgd', p, v)

    o = lax.map(block, qb)
    o = o.transpose(1, 0, 2, 3, 4, 5).reshape(b, s, Q_DIM)
    return o @ w_o


def swiglu_ffn(x, g, w_gate, w_up, w_down):
    h = rms_norm(x, g)
    return (jax.nn.silu(h @ w_gate) * (h @ w_up)) @ w_down


def setup_inputs(seed: int = 0) -> dict:
    key = jax.random.key(seed)
    ks = jax.random.split(key, 12)
    f32 = jnp.float32
    d = D_MODEL
    return {
        "x_prompt": jax.random.normal(ks[0], (BATCH, SEQ, d), f32),
        "x_sample": jax.random.normal(ks[1], (DEC_BATCH, DEC_SEQ, d), f32),
        "norm_mix": 1.0 + 0.02 * jax.random.normal(ks[2], (DEPTH, d), f32),
        "norm_ffn": 1.0 + 0.02 * jax.random.normal(ks[3], (DEPTH, d), f32),
        "fourier_w": jax.random.normal(ks[4], (N_FOURIER_LAYERS, d, d), f32) * d ** -0.5,
        "attn_w_qkv": jax.random.normal(ks[5], (N_ATTN_LAYERS, d, QKV_DIM), f32) * d ** -0.5,
        "attn_q_norm": 1.0 + 0.02 * jax.random.normal(ks[6], (N_ATTN_LAYERS, HEAD_DIM), f32),
        "attn_k_norm": 1.0 + 0.02 * jax.random.normal(ks[7], (N_ATTN_LAYERS, HEAD_DIM), f32),
        "attn_w_o": jax.random.normal(ks[8], (N_ATTN_LAYERS, Q_DIM, d), f32) * Q_DIM ** -0.5,
        "ffn_w_gate": jax.random.normal(ks[9], (DEPTH, d, D_FF), f32) * d ** -0.5,
        "ffn_w_up": jax.random.normal(ks[10], (DEPTH, d, D_FF), f32) * d ** -0.5,
        "ffn_w_down": jax.random.normal(ks[11], (DEPTH, D_FF, d), f32) * D_FF ** -0.5,
    }


def reference(x_prompt, x_sample, norm_mix, norm_ffn, fourier_w, attn_w_qkv, attn_q_norm,
              attn_k_norm, attn_w_o, ffn_w_gate, ffn_w_up, ffn_w_down):
    def run_trunk(x):
        cos, sin = axial_rope_tables(x.shape[1])
        for i in range(DEPTH):
            j = i // 2
            if i % 2 == 0:
                x = x + fourier_mixer(x, norm_mix[i], fourier_w[j])
            else:
                x = x + attention_mixer(x, norm_mix[i], attn_w_qkv[j], attn_q_norm[j],
                                        attn_k_norm[j], attn_w_o[j], cos, sin)
            x = x + swiglu_ffn(x, norm_ffn[i], ffn_w_gate[i], ffn_w_up[i], ffn_w_down[i])
        return x

    y_prompt = run_trunk(x_prompt)
    y_sample = run_trunk(x_sample)
    return (y_prompt, y_sample)
```

```python
import functools

import numpy as np
import jax
import jax.numpy as jnp
from jax import lax
from jax.experimental import pallas as pl
from jax.experimental.pallas import tpu as pltpu

D_MODEL = 1024
HEAD_DIM = 128
N_HEADS = D_MODEL // HEAD_DIM
N_KV_HEADS = 2
GROUP = N_HEADS // N_KV_HEADS
Q_DIM = N_HEADS * HEAD_DIM
KV_DIM = N_KV_HEADS * HEAD_DIM
QKV_DIM = Q_DIM + 2 * KV_DIM
GRID_W = 64
AXIS_ROT_DIM = HEAD_DIM // 2
ROPE_THETA = 10000.0
N_FGROUPS = 8
FGROUP_DIM = D_MODEL // N_FGROUPS
EPS = 1e-6

LANES = 128
VMEM_LIMIT = 56 * 1024 * 1024

F32 = jnp.float32
BF16 = jnp.bfloat16


def _rms(x, g):
    ms = jnp.mean(x * x, axis=-1, keepdims=True)
    return x * lax.rsqrt(ms + EPS) * g


def _dft_cos_sin(n):
    k = np.arange(n)
    ang = 2.0 * np.pi * ((k[:, None] * k[None, :]) % n) / n
    return np.cos(ang), np.sin(ang)


def _fourier_a_kernel(x_ref, g_ref, f1_ref, twr_ref, twi_ref, zr_ref, zi_ref, *, n1, ts2):
    d = D_MODEL
    g = g_ref[...]
    f1 = f1_ref[...]
    for j in range(ts2):
        sl = slice(j * d, (j + 1) * d)
        h = _rms(x_ref[:, sl], g).astype(BF16)
        z = jnp.dot(f1, h, preferred_element_type=F32)
        zr, zi = z[:n1], z[n1:]
        tr = jnp.tile(twr_ref[j], (1, d // LANES))
        ti = jnp.tile(twi_ref[j], (1, d // LANES))
        zr_ref[:, sl] = (zr * tr - zi * ti).astype(BF16)
        zi_ref[:, sl] = (zr * ti + zi * tr).astype(BF16)


def _fourier_c_kernel(zr_ref, zi_ref, x_ref, f2_ref, cs_ref, w_ref, o_ref, *, n2, tk1, scale):
    d = D_MODEL
    f2 = f2_ref[...]
    xs = []
    for j in range(tk1):
        zin = jnp.concatenate([zr_ref[j], zi_ref[j]], axis=0)
        xs.append(jnp.dot(f2, zin, preferred_element_type=F32))
    xr = jnp.concatenate([x[:n2] for x in xs], axis=0).astype(BF16)
    xi = jnp.concatenate([x[n2:] for x in xs], axis=0).astype(BF16)
    cs = cs_ref[...]
    mixed = []
    for gi in range(N_FGROUPS):
        sl = slice(gi * FGROUP_DIM, (gi + 1) * FGROUP_DIM)
        lhs = jnp.concatenate([xr[:, sl], xi[:, sl]], axis=1)
        mixed.append(jnp.dot(lhs, cs, preferred_element_type=F32) * scale)
    mixed = jnp.concatenate(mixed, axis=1).astype(BF16)
    delta = jnp.dot(mixed, w_ref[...], preferred_element_type=F32)
    for j in range(tk1):
        sl = slice(j * d, (j + 1) * d)
        o_ref[:, sl] = x_ref[:, sl] + delta[j * n2:(j + 1) * n2]


def _fourier_tables(seq_len):
    n2 = 64
    n1 = seq_len // n2
    c1, s1 = _dft_cos_sin(n1)
    f1 = np.concatenate([c1, -s1], axis=0)
    c2, s2 = _dft_cos_sin(n2)
    f2 = np.block([[c2, s2], [-s2, c2]])
    cc, sc = _dft_cos_sin(FGROUP_DIM)
    cs = np.concatenate([cc, sc], axis=0)
    ang = 2.0 * np.pi * (np.arange(n2)[:, None] * np.arange(n1)[None, :]) / seq_len
    twr = np.cos(ang)[:, :, None].astype(np.float32)
    twi = (-np.sin(ang))[:, :, None].astype(np.float32)
    return n1, n2, f1, f2, cs, twr, twi


def _fourier_layer(x, g, w_bf16):
    b, s, d = x.shape
    n1, n2, f1, f2, cs, twr, twi = _fourier_tables(s)
    ts2 = 4
    tk1 = 8
    f1 = jnp.asarray(f1, BF16)
    f2 = jnp.asarray(f2, BF16)
    cs = jnp.asarray(cs, BF16)
    twr = jnp.broadcast_to(jnp.asarray(twr), (n2, n1, LANES))
    twi = jnp.broadcast_to(jnp.asarray(twi), (n2, n1, LANES))
    g2 = g.reshape(1, d)

    z_shape = jax.ShapeDtypeStruct((b, n1, n2 * d), BF16)
    zr, zi = pl.pallas_call(
        functools.partial(_fourier_a_kernel, n1=n1, ts2=ts2),
        grid=(b, n2 // ts2),
        in_specs=[
            pl.BlockSpec((None, n1, ts2 * d), lambda bi, i: (bi, 0, i)),
            pl.BlockSpec((1, d), lambda bi, i: (0, 0)),
            pl.BlockSpec((2 * n1, n1), lambda bi, i: (0, 0)),
            pl.BlockSpec((ts2, n1, LANES), lambda bi, i: (i, 0, 0)),
            pl.BlockSpec((ts2, n1, LANES), lambda bi, i: (i, 0, 0)),
        ],
        out_specs=[
            pl.BlockSpec((None, n1, ts2 * d), lambda bi, i: (bi, 0, i)),
            pl.BlockSpec((None, n1, ts2 * d), lambda bi, i: (bi, 0, i)),
        ],
        out_shape=[z_shape, z_shape],
        compiler_params=pltpu.CompilerParams(
            dimension_semantics=("parallel", "parallel"), vmem_limit_bytes=VMEM_LIMIT),
        name="fourier_a",
    )(x.reshape(b, n1, n2 * d), g2, f1, twr, twi)

    scale = float(1.0 / np.sqrt(s * FGROUP_DIM))
    out = pl.pallas_call(
        functools.partial(_fourier_c_kernel, n2=n2, tk1=tk1, scale=scale),
        grid=(b, n1 // tk1),
        in_specs=[
            pl.BlockSpec((None, tk1, n2, d), lambda bi, i: (bi, i, 0, 0)),
            pl.BlockSpec((None, tk1, n2, d), lambda bi, i: (bi, i, 0, 0)),
            pl.BlockSpec((None, n2, tk1 * d), lambda bi, i: (bi, 0, i)),
            pl.BlockSpec((2 * n2, 2 * n2), lambda bi, i: (0, 0)),
            pl.BlockSpec((2 * FGROUP_DIM, FGROUP_DIM), lambda bi, i: (0, 0)),
            pl.BlockSpec((d, d), lambda bi, i: (0, 0)),
        ],
        out_specs=pl.BlockSpec((None, n2, tk1 * d), lambda bi, i: (bi, 0, i)),
        out_shape=jax.ShapeDtypeStruct((b, n2, n1 * d), F32),
        compiler_params=pltpu.CompilerParams(
            dimension_semantics=("parallel", "parallel"), vmem_limit_bytes=VMEM_LIMIT),
        name="fourier_c",
    )(zr.reshape(b, n1, n2, d), zi.reshape(b, n1, n2, d), x.reshape(b, n2, n1 * d), f2, cs, w_bf16)
    return out.reshape(b, s, d)


def _qkv_kernel(x_ref, g_ref, w_ref, qg_ref, kg_ref, cos_ref, sin_ref, q_ref, k_ref, v_ref):
    h = _rms(x_ref[...], g_ref[...]).astype(BF16)
    qkv = jnp.dot(h, w_ref[...], preferred_element_type=F32)
    cos = cos_ref[...]
    sin = sin_ref[...]

    def norm_rope(y, gain):
        yn = _rms(y, gain)
        return yn * cos + pltpu.roll(yn, HEAD_DIM // 2, axis=1) * sin

    qg = qg_ref[...]
    kg = kg_ref[...]
    q_scale = HEAD_DIM ** -0.5
    for hd in range(N_HEADS):
        sl = slice(hd * HEAD_DIM, (hd + 1) * HEAD_DIM)
        q_ref[:, sl] = (norm_rope(qkv[:, sl], qg) * q_scale).astype(BF16)
    for hd in range(N_KV_HEADS):
        sl = slice(hd * HEAD_DIM, (hd + 1) * HEAD_DIM)
        k_ref[:, sl] = norm_rope(qkv[:, Q_DIM + hd * HEAD_DIM:Q_DIM + (hd + 1) * HEAD_DIM], kg).astype(BF16)
    v_ref[...] = qkv[:, Q_DIM + KV_DIM:].astype(BF16)


def _attn_kernel(q_ref, k_ref, v_ref, o_ref, *, tq, tk):
    s = k_ref.shape[0]
    q = jnp.concatenate([q_ref[:, gi * HEAD_DIM:(gi + 1) * HEAD_DIM] for gi in range(GROUP)], axis=0)
    m_rows = GROUP * tq

    def body(j, carry):
        m, l, acc = carry
        start = pl.multiple_of(j * tk, tk)
        kc = k_ref[pl.ds(start, tk), :]
        vc = v_ref[pl.ds(start, tk), :]
        sc = lax.dot_general(q, kc, (((1,), (1,)), ((), ())), preferred_element_type=F32)
        m_new = jnp.maximum(m, jnp.max(sc, axis=-1, keepdims=True))
        alpha = jnp.exp(m - m_new)
        p = jnp.exp(sc - m_new)
        l = alpha * l + jnp.sum(p, axis=-1, keepdims=True)
        acc = alpha * acc + jnp.dot(p.astype(BF16), vc, preferred_element_type=F32)
        return m_new, l, acc

    m0 = jnp.full((m_rows, 1), -jnp.inf, F32)
    l0 = jnp.zeros((m_rows, 1), F32)
    a0 = jnp.zeros((m_rows, HEAD_DIM), F32)
    _, l, acc = lax.fori_loop(0, s // tk, body, (m0, l0, a0))
    o = (acc / l).astype(BF16)
    for gi in range(GROUP):
        o_ref[:, gi * HEAD_DIM:(gi + 1) * HEAD_DIM] = o[gi * tq:(gi + 1) * tq]


def _rope_tables(seq_len):
    rows = seq_len // GRID_W
    row = jnp.repeat(jnp.arange(rows, dtype=F32), GRID_W)
    col = jnp.tile(jnp.arange(GRID_W, dtype=F32), rows)
    inv = ROPE_THETA ** (-jnp.arange(0, AXIS_ROT_DIM, 2, dtype=F32) / AXIS_ROT_DIM)
    ang = jnp.concatenate([row[:, None] * inv, col[:, None] * inv], axis=-1)
    cos, sin = jnp.cos(ang), jnp.sin(ang)
    return jnp.concatenate([cos, cos], axis=-1), jnp.concatenate([-sin, sin], axis=-1)


def _attention_qkv(x, g, w_qkv_bf16, q_gain, k_gain):
    b, s, d = x.shape
    t = b * s
    tm = 512
    cos, sin = _rope_tables(s)
    nblk = s // tm
    q, k, v = pl.pallas_call(
        _qkv_kernel,
        grid=(t // tm,),
        in_specs=[
            pl.BlockSpec((tm, d), lambda i: (i, 0)),
            pl.BlockSpec((1, d), lambda i: (0, 0)),
            pl.BlockSpec((d, QKV_DIM), lambda i: (0, 0)),
            pl.BlockSpec((1, HEAD_DIM), lambda i: (0, 0)),
            pl.BlockSpec((1, HEAD_DIM), lambda i: (0, 0)),
            pl.BlockSpec((tm, HEAD_DIM), lambda i: (i % nblk, 0)),
            pl.BlockSpec((tm, HEAD_DIM), lambda i: (i % nblk, 0)),
        ],
        out_specs=[
            pl.BlockSpec((tm, Q_DIM), lambda i: (i, 0)),
            pl.BlockSpec((tm, KV_DIM), lambda i: (i, 0)),
            pl.BlockSpec((tm, KV_DIM), lambda i: (i, 0)),
        ],
        out_shape=[
            jax.ShapeDtypeStruct((t, Q_DIM), BF16),
            jax.ShapeDtypeStruct((t, KV_DIM), BF16),
            jax.ShapeDtypeStruct((t, KV_DIM), BF16),
        ],
        compiler_params=pltpu.CompilerParams(
            dimension_semantics=("parallel",), vmem_limit_bytes=VMEM_LIMIT),
        name="qkv_proj",
    )(x.reshape(t, d), g.reshape(1, d), w_qkv_bf16, q_gain.reshape(1, HEAD_DIM),
      k_gain.reshape(1, HEAD_DIM), cos, sin)

    tq = 128
    tk = 512
    o = pl.pallas_call(
        functools.partial(_attn_kernel, tq=tq, tk=tk),
        grid=(b, N_KV_HEADS, s // tq),
        in_specs=[
            pl.BlockSpec((None, tq, GROUP * HEAD_DIM), lambda bi, hi, i: (bi, i, hi)),
            pl.BlockSpec((None, s, HEAD_DIM), lambda bi, hi, i: (bi, 0, hi)),
            pl.BlockSpec((None, s, HEAD_DIM), lambda bi, hi, i: (bi, 0, hi)),
        ],
        out_specs=pl.BlockSpec((None, tq, GROUP * HEAD_DIM), lambda bi, hi, i: (bi, i, hi)),
        out_shape=jax.ShapeDtypeStruct((b, s, Q_DIM), BF16),
        compiler_params=pltpu.CompilerParams(
            dimension_semantics=("parallel", "parallel", "parallel"), vmem_limit_bytes=VMEM_LIMIT),
        name="attention",
    )(q.reshape(b, s, Q_DIM), k.reshape(b, s, KV_DIM), v.reshape(b, s, KV_DIM))
    return o.reshape(t, Q_DIM)


def _ffn_kernel(*refs, with_attn):
    if with_attn:
        x_ref, a_ref, wo_ref, g_ref, wg_ref, wu_ref, wd_ref, o_ref = refs
        x = x_ref[...] + jnp.dot(a_ref[...], wo_ref[...], preferred_element_type=F32)
    else:
        x_ref, g_ref, wg_ref, wu_ref, wd_ref, o_ref = refs
        x = x_ref[...]
    h = _rms(x, g_ref[...]).astype(BF16)
    gate = jnp.dot(h, wg_ref[...], preferred_element_type=F32)
    up = jnp.dot(h, wu_ref[...], preferred_element_type=F32)
    act = (gate * jax.nn.sigmoid(gate) * up).astype(BF16)
    o_ref[...] = x + jnp.dot(act, wd_ref[...], preferred_element_type=F32)


def _ffn_layer(x2d, g, wg, wu, wd, attn=None):
    t, d = x2d.shape
    dff = wg.shape[1]
    tm = 512
    resident = functools.partial(pl.BlockSpec, index_map=lambda i: (0, 0), pipeline_mode=pl.Buffered(1))
    row = lambda width: pl.BlockSpec((tm, width), lambda i: (i, 0))
    in_specs = [row(d)]
    args = [x2d]
    if attn is not None:
        a2d, wo = attn
        in_specs += [row(Q_DIM), resident((Q_DIM, d))]
        args += [a2d, wo]
    in_specs += [resident((1, d)), resident((d, dff)), resident((d, dff)), resident((dff, d))]
    args += [g.reshape(1, d), wg, wu, wd]
    return pl.pallas_call(
        functools.partial(_ffn_kernel, with_attn=attn is not None),
        grid=(t // tm,),
        in_specs=in_specs,
        out_specs=row(d),
        out_shape=jax.ShapeDtypeStruct((t, d), F32),
        compiler_params=pltpu.CompilerParams(
            dimension_semantics=("parallel",), vmem_limit_bytes=VMEM_LIMIT),
        name="swiglu",
    )(*args)


def _pair_split_perm():
    p = np.concatenate([np.arange(0, HEAD_DIM, 2), np.arange(1, HEAD_DIM, 2)])
    cols = [h * HEAD_DIM + p for h in range(N_HEADS + N_KV_HEADS)]
    cols.append(np.arange(Q_DIM + KV_DIM, QKV_DIM))
    return p, np.concatenate(cols)


def kernel(x_prompt, x_sample, norm_mix, norm_ffn, fourier_w, attn_w_qkv, attn_q_norm, attn_k_norm,
           attn_w_o, ffn_w_gate, ffn_w_up, ffn_w_down):
    depth = norm_mix.shape[0]
    head_perm, col_perm = _pair_split_perm()
    fw = fourier_w.astype(BF16)
    wqkv = attn_w_qkv[:, :, col_perm].astype(BF16)
    qn = attn_q_norm[:, head_perm]
    kn = attn_k_norm[:, head_perm]
    wo = attn_w_o.astype(BF16)
    wg = ffn_w_gate.astype(BF16)
    wu = ffn_w_up.astype(BF16)
    wd = ffn_w_down.astype(BF16)

    def run_trunk(x):
        b, s, d = x.shape
        for i in range(depth):
            j = i // 2
            if i % 2 == 0:
                x = _fourier_layer(x, norm_mix[i], fw[j])
                x2d = _ffn_layer(x.reshape(b * s, d), norm_ffn[i], wg[i], wu[i], wd[i])
            else:
                a2d = _attention_qkv(x, norm_mix[i], wqkv[j], qn[j], kn[j])
                x2d = _ffn_layer(x.reshape(b * s, d), norm_ffn[i], wg[i], wu[i], wd[i], attn=(a2d, wo[j]))
            x = x2d.reshape(b, s, d)
        return x

    return (run_trunk(x_prompt), run_trunk(x_sample))
```

```python
import functools

import numpy as np
import jax
import jax.numpy as jnp
from jax import lax
from jax.experimental import pallas as pl
from jax.experimental.pallas import tpu as pltpu

D_MODEL = 1024
HEAD_DIM = 128
N_HEADS = D_MODEL // HEAD_DIM
N_KV_HEADS = 2
GROUP = N_HEADS // N_KV_HEADS
Q_DIM = N_HEADS * HEAD_DIM
KV_DIM = N_KV_HEADS * HEAD_DIM
QKV_DIM = Q_DIM + 2 * KV_DIM
GRID_W = 64
AXIS_ROT_DIM = HEAD_DIM // 2
ROPE_THETA = 10000.0
N_FGROUPS = 8
FGROUP_DIM = D_MODEL // N_FGROUPS
EPS = 1e-6

LANES = 128
VMEM_LIMIT = 56 * 1024 * 1024

F32 = jnp.float32
BF16 = jnp.bfloat16


def _rms(x, g):
    ms = jnp.mean(x * x, axis=-1, keepdims=True)
    return x * lax.rsqrt(ms + EPS) * g


def _dft_cos_sin(n):
    k = np.arange(n)
    ang = 2.0 * np.pi * ((k[:, None] * k[None, :]) % n) / n
    return np.cos(ang), np.sin(ang)


def _fourier_a_kernel(x_ref, g_ref, f1_ref, twr_ref, twi_ref, zr_ref, zi_ref, *, n1, ts2):
    d = D_MODEL
    g = g_ref[...]
    f1 = f1_ref[...]
    for j in range(ts2):
        sl = slice(j * d, (j + 1) * d)
        h = _rms(x_ref[:, sl], g).astype(BF16)
        z = jnp.dot(f1, h, preferred_element_type=F32)
        zr, zi = z[:n1], z[n1:]
        tr = jnp.tile(twr_ref[j], (1, d // LANES))
        ti = jnp.tile(twi_ref[j], (1, d // LANES))
        zr_ref[:, sl] = (zr * tr - zi * ti).astype(BF16)
        zi_ref[:, sl] = (zr * ti + zi * tr).astype(BF16)


def _fourier_c_kernel(zr_ref, zi_ref, x_ref, f2_ref, cs_ref, w_ref, o_ref, *, n2, tk1, scale):
    d = D_MODEL
    f2 = f2_ref[...]
    xs = []
    for j in range(tk1):
        zin = jnp.concatenate([zr_ref[j], zi_ref[j]], axis=0)
        xs.append(jnp.dot(f2, zin, preferred_element_type=F32))
    xr = jnp.concatenate([x[:n2] for x in xs], axis=0).astype(BF16)
    xi = jnp.concatenate([x[n2:] for x in xs], axis=0).astype(BF16)
    cs = cs_ref[...]
    mixed = []
    for gi in range(N_FGROUPS):
        sl = slice(gi * FGROUP_DIM, (gi + 1) * FGROUP_DIM)
        lhs = jnp.concatenate([xr[:, sl], xi[:, sl]], axis=1)
        mixed.append(jnp.dot(lhs, cs, preferred_element_type=F32) * scale)
    mixed = jnp.concatenate(mixed, axis=1).astype(BF16)
    delta = jnp.dot(mixed, w_ref[...], preferred_element_type=F32)
    for j in range(tk1):
        sl = slice(j * d, (j + 1) * d)
        o_ref[:, sl] = x_ref[:, sl] + delta[j * n2:(j + 1) * n2]


def _fourier_tables(seq_len):
    n2 = 64
    n1 = seq_len // n2
    c1, s1 = _dft_cos_sin(n1)
    f1 = np.concatenate([c1, -s1], axis=0)
    c2, s2 = _dft_cos_sin(n2)
    f2 = np.block([[c2, s2], [-s2, c2]])
    cc, sc = _dft_cos_sin(FGROUP_DIM)
    cs = np.concatenate([cc, sc], axis=0)
    ang = 2.0 * np.pi * (np.arange(n2)[:, None] * np.arange(n1)[None, :]) / seq_len
    twr = np.cos(ang)[:, :, None].astype(np.float32)
    twi = (-np.sin(ang))[:, :, None].astype(np.float32)
    return n1, n2, f1, f2, cs, twr, twi


def _fourier_layer(x, g, w_bf16):
    b, s, d = x.shape
    n1, n2, f1, f2, cs, twr, twi = _fourier_tables(s)
    ts2 = 4
    tk1 = 8
    f1 = jnp.asarray(f1, BF16)
    f2 = jnp.asarray(f2, BF16)
    cs = jnp.asarray(cs, BF16)
    twr = jnp.broadcast_to(jnp.asarray(twr), (n2, n1, LANES))
    twi = jnp.broadcast_to(jnp.asarray(twi), (n2, n1, LANES))
    g2 = g.reshape(1, d)

    z_shape = jax.ShapeDtypeStruct((b, n1, n2 * d), BF16)
    zr, zi = pl.pallas_call(
        functools.partial(_fourier_a_kernel, n1=n1, ts2=ts2),
        grid=(b, n2 // ts2),
        in_specs=[
            pl.BlockSpec((None, n1, ts2 * d), lambda bi, i: (bi, 0, i)),
            pl.BlockSpec((1, d), lambda bi, i: (0, 0)),
            pl.BlockSpec((2 * n1, n1), lambda bi, i: (0, 0)),
            pl.BlockSpec((ts2, n1, LANES), lambda bi, i: (i, 0, 0)),
            pl.BlockSpec((ts2, n1, LANES), lambda bi, i: (i, 0, 0)),
        ],
        out_specs=[
            pl.BlockSpec((None, n1, ts2 * d), lambda bi, i: (bi, 0, i)),
            pl.BlockSpec((None, n1, ts2 * d), lambda bi, i: (bi, 0, i)),
        ],
        out_shape=[z_shape, z_shape],
        compiler_params=pltpu.CompilerParams(
            dimension_semantics=("parallel", "parallel"), vmem_limit_bytes=VMEM_LIMIT),
        name="fourier_a",
    )(x.reshape(b, n1, n2 * d), g2, f1, twr, twi)

    scale = float(1.0 / np.sqrt(s * FGROUP_DIM))
    out = pl.pallas_call(
        functools.partial(_fourier_c_kernel, n2=n2, tk1=tk1, scale=scale),
        grid=(b, n1 // tk1),
        in_specs=[
            pl.BlockSpec((None, tk1, n2, d), lambda bi, i: (bi, i, 0, 0)),
            pl.BlockSpec((None, tk1, n2, d), lambda bi, i: (bi, i, 0, 0)),
            pl.BlockSpec((None, n2, tk1 * d), lambda bi, i: (bi, 0, i)),
            pl.BlockSpec((2 * n2, 2 * n2), lambda bi, i: (0, 0)),
            pl.BlockSpec((2 * FGROUP_DIM, FGROUP_DIM), lambda bi, i: (0, 0)),
            pl.BlockSpec((d, d), lambda bi, i: (0, 0)),
        ],
        out_specs=pl.BlockSpec((None, n2, tk1 * d), lambda bi, i: (bi, 0, i)),
        out_shape=jax.ShapeDtypeStruct((b, n2, n1 * d), F32),
        compiler_params=pltpu.CompilerParams(
            dimension_semantics=("parallel", "parallel"), vmem_limit_bytes=VMEM_LIMIT),
        name="fourier_c",
    )(zr.reshape(b, n1, n2, d), zi.reshape(b, n1, n2, d), x.reshape(b, n2, n1 * d), f2, cs, w_bf16)
    return out.reshape(b, s, d)


def _qkv_kernel(x_ref, g_ref, w_ref, qg_ref, kg_ref, cos_ref, sin_ref, q_ref, k_ref, vt_ref):
    h = _rms(x_ref[...], g_ref[...]).astype(BF16)
    qkv = jnp.dot(h, w_ref[...], preferred_element_type=F32)
    cos = cos_ref[...]
    sin = sin_ref[...]

    def norm_rope(y, gain):
        yn = _rms(y, gain)
        return yn * cos + pltpu.roll(yn, HEAD_DIM // 2, axis=1) * sin

    qg = qg_ref[...]
    kg = kg_ref[...]
    q_scale = HEAD_DIM ** -0.5 * np.log2(np.e)
    for hd in range(N_HEADS):
        sl = slice(hd * HEAD_DIM, (hd + 1) * HEAD_DIM)
        q_ref[:, sl] = (norm_rope(qkv[:, sl], qg) * q_scale).astype(BF16)
    for hd in range(N_KV_HEADS):
        sl = slice(hd * HEAD_DIM, (hd + 1) * HEAD_DIM)
        k_ref[:, sl] = norm_rope(qkv[:, Q_DIM + hd * HEAD_DIM:Q_DIM + (hd + 1) * HEAD_DIM], kg).astype(BF16)
    vt_ref[...] = qkv[:, Q_DIM + KV_DIM:].T.astype(BF16)


def _attn_kernel(q_ref, k_ref, vt_ref, o_ref, s0_ref, s1_ref, acc_ref, *, tq, tk):
    n_chunks = vt_ref.shape[0]
    q = jnp.concatenate([q_ref[:, gi * HEAD_DIM:(gi + 1) * HEAD_DIM] for gi in range(GROUP)], axis=0)
    m_cols = GROUP * tq

    def scores(j, dst_ref):
        start = pl.multiple_of(j * tk, tk)
        kc = k_ref[pl.ds(start, tk), :]
        st = lax.dot_general(kc, q, (((1,), (1,)), ((), ())), preferred_element_type=F32)
        dst_ref[...] = st
        return jnp.max(st, axis=0, keepdims=True)

    def consume(j, src_ref, m, alpha, l):
        p = jnp.exp2(src_ref[...] - m)
        l = alpha * l + jnp.sum(p, axis=0, keepdims=True)
        pv = jnp.dot(vt_ref[j], p.astype(BF16), preferred_element_type=F32)
        acc_ref[...] = alpha * acc_ref[...] + pv
        return l

    def step(j, cur_ref, nxt_ref, carry):
        m, alpha, l = carry
        cmax = scores(j + 1, nxt_ref)
        l = consume(j, cur_ref, m, alpha, l)
        m_next = jnp.maximum(m, cmax)
        return m_next, jnp.exp2(m - m_next), l

    def body(jj, carry):
        carry = step(2 * jj, s0_ref, s1_ref, carry)
        return step(2 * jj + 1, s1_ref, s0_ref, carry)

    acc_ref[...] = jnp.zeros_like(acc_ref)
    m0 = scores(0, s0_ref)
    zero = jnp.zeros((1, m_cols), F32)
    carry = lax.fori_loop(0, n_chunks // 2 - 1, body, (m0, zero, zero))
    m, alpha, l = step(n_chunks - 2, s0_ref, s1_ref, carry)
    l = consume(n_chunks - 1, s1_ref, m, alpha, l)
    o_t = acc_ref[...] / l
    for gi in range(GROUP):
        o_ref[:, gi * HEAD_DIM:(gi + 1) * HEAD_DIM] = o_t[:, gi * tq:(gi + 1) * tq].T.astype(BF16)


def _rope_tables(seq_len):
    rows = seq_len // GRID_W
    row = jnp.repeat(jnp.arange(rows, dtype=F32), GRID_W)
    col = jnp.tile(jnp.arange(GRID_W, dtype=F32), rows)
    inv = ROPE_THETA ** (-jnp.arange(0, AXIS_ROT_DIM, 2, dtype=F32) / AXIS_ROT_DIM)
    ang = jnp.concatenate([row[:, None] * inv, col[:, None] * inv], axis=-1)
    cos, sin = jnp.cos(ang), jnp.sin(ang)
    return jnp.concatenate([cos, cos], axis=-1), jnp.concatenate([-sin, sin], axis=-1)


def _attention_qkv(x, g, w_qkv_bf16, q_gain, k_gain):
    b, s, d = x.shape
    t = b * s
    tq = 256
    tk = 512
    tm = tk
    cos, sin = _rope_tables(s)
    nblk = s // tm
    q, k, vt = pl.pallas_call(
        _qkv_kernel,
        grid=(t // tm,),
        in_specs=[
            pl.BlockSpec((tm, d), lambda i: (i, 0)),
            pl.BlockSpec((1, d), lambda i: (0, 0)),
            pl.BlockSpec((d, QKV_DIM), lambda i: (0, 0)),
            pl.BlockSpec((1, HEAD_DIM), lambda i: (0, 0)),
            pl.BlockSpec((1, HEAD_DIM), lambda i: (0, 0)),
            pl.BlockSpec((tm, HEAD_DIM), lambda i: (i % nblk, 0)),
            pl.BlockSpec((tm, HEAD_DIM), lambda i: (i % nblk, 0)),
        ],
        out_specs=[
            pl.BlockSpec((tm, Q_DIM), lambda i: (i, 0)),
            pl.BlockSpec((tm, KV_DIM), lambda i: (i, 0)),
            pl.BlockSpec((None, KV_DIM, tm), lambda i: (i, 0, 0)),
        ],
        out_shape=[
            jax.ShapeDtypeStruct((t, Q_DIM), BF16),
            jax.ShapeDtypeStruct((t, KV_DIM), BF16),
            jax.ShapeDtypeStruct((t // tm, KV_DIM, tm), BF16),
        ],
        compiler_params=pltpu.CompilerParams(
            dimension_semantics=("parallel",), vmem_limit_bytes=VMEM_LIMIT),
        name="qkv_proj",
    )(x.reshape(t, d), g.reshape(1, d), w_qkv_bf16, q_gain.reshape(1, HEAD_DIM),
      k_gain.reshape(1, HEAD_DIM), cos, sin)

    o = pl.pallas_call(
        functools.partial(_attn_kernel, tq=tq, tk=tk),
        grid=(b, N_KV_HEADS, s // tq),
        in_specs=[
            pl.BlockSpec((None, tq, GROUP * HEAD_DIM), lambda bi, hi, i: (bi, i, hi)),
            pl.BlockSpec((None, s, HEAD_DIM), lambda bi, hi, i: (bi, 0, hi)),
            pl.BlockSpec((None, s // tk, HEAD_DIM, tk), lambda bi, hi, i: (bi, 0, hi, 0)),
        ],
        out_specs=pl.BlockSpec((None, tq, GROUP * HEAD_DIM), lambda bi, hi, i: (bi, i, hi)),
        out_shape=jax.ShapeDtypeStruct((b, s, Q_DIM), BF16),
        scratch_shapes=[
            pltpu.VMEM((tk, GROUP * tq), F32),
            pltpu.VMEM((tk, GROUP * tq), F32),
            pltpu.VMEM((HEAD_DIM, GROUP * tq), F32),
        ],
        compiler_params=pltpu.CompilerParams(
            dimension_semantics=("parallel", "parallel", "parallel"), vmem_limit_bytes=VMEM_LIMIT),
        name="attention",
    )(q.reshape(b, s, Q_DIM), k.reshape(b, s, KV_DIM), vt.reshape(b, s // tk, KV_DIM, tk))
    return o.reshape(t, Q_DIM)


def _ffn_kernel(*refs, with_attn):
    if with_attn:
        x_ref, a_ref, wo_ref, g_ref, wg_ref, wu_ref, wd_ref, o_ref = refs
        x = x_ref[...] + jnp.dot(a_ref[...], wo_ref[...], preferred_element_type=F32)
    else:
        x_ref, g_ref, wg_ref, wu_ref, wd_ref, o_ref = refs
        x = x_ref[...]
    h = _rms(x, g_ref[...]).astype(BF16)
    gate = jnp.dot(h, wg_ref[...], preferred_element_type=F32)
    up = jnp.dot(h, wu_ref[...], preferred_element_type=F32)
    act = (gate * jax.nn.sigmoid(gate) * up).astype(BF16)
    o_ref[...] = x + jnp.dot(act, wd_ref[...], preferred_element_type=F32)


def _ffn_layer(x2d, g, wg, wu, wd, attn=None):
    t, d = x2d.shape
    dff = wg.shape[1]
    tm = 512
    resident = functools.partial(pl.BlockSpec, index_map=lambda i: (0, 0), pipeline_mode=pl.Buffered(1))
    row = lambda width: pl.BlockSpec((tm, width), lambda i: (i, 0))
    in_specs = [row(d)]
    args = [x2d]
    if attn is not None:
        a2d, wo = attn
        in_specs += [row(Q_DIM), resident((Q_DIM, d))]
        args += [a2d, wo]
    in_specs += [resident((1, d)), resident((d, dff)), resident((d, dff)), resident((dff, d))]
    args += [g.reshape(1, d), wg, wu, wd]
    return pl.pallas_call(
        functools.partial(_ffn_kernel, with_attn=attn is not None),
        grid=(t // tm,),
        in_specs=in_specs,
        out_specs=row(d),
        out_shape=jax.ShapeDtypeStruct((t, d), F32),
        compiler_params=pltpu.CompilerParams(
            dimension_semantics=("parallel",), vmem_limit_bytes=VMEM_LIMIT),
        name="swiglu",
    )(*args)


def _pair_split_perm():
    p = np.concatenate([np.arange(0, HEAD_DIM, 2), np.arange(1, HEAD_DIM, 2)])
    cols = [h * HEAD_DIM + p for h in range(N_HEADS + N_KV_HEADS)]
    cols.append(np.arange(Q_DIM + KV_DIM, QKV_DIM))
    return p, np.concatenate(cols)


def kernel(x_prompt, x_sample, norm_mix, norm_ffn, fourier_w, attn_w_qkv, attn_q_norm, attn_k_norm,
           attn_w_o, ffn_w_gate, ffn_w_up, ffn_w_down):
    depth = norm_mix.shape[0]
    head_perm, col_perm = _pair_split_perm()
    fw = fourier_w.astype(BF16)
    wqkv = attn_w_qkv[:, :, col_perm].astype(BF16)
    qn = attn_q_norm[:, head_perm]
    kn = attn_k_norm[:, head_perm]
    wo = attn_w_o.astype(BF16)
    wg = ffn_w_gate.astype(BF16)
    wu = ffn_w_up.astype(BF16)
    wd = ffn_w_down.astype(BF16)

    def run_trunk(x):
        b, s, d = x.shape
        for i in range(depth):
            j = i // 2
            if i % 2 == 0:
                x = _fourier_layer(x, norm_mix[i], fw[j])
                x2d = _ffn_layer(x.reshape(b * s, d), norm_ffn[i], wg[i], wu[i], wd[i])
            else:
                a2d = _attention_qkv(x, norm_mix[i], wqkv[j], qn[j], kn[j])
                x2d = _ffn_layer(x.reshape(b * s, d), norm_ffn[i], wg[i], wu[i], wd[i], attn=(a2d, wo[j]))
            x = x2d.reshape(b, s, d)
        return x

    return (run_trunk(x_prompt), run_trunk(x_sample))
```

```python
import functools

import numpy as np
import jax
import jax.numpy as jnp
from jax import lax
from jax.experimental import pallas as pl
from jax.experimental.pallas import tpu as pltpu

D_MODEL = 1024
HEAD_DIM = 128
N_HEADS = D_MODEL // HEAD_DIM
N_KV_HEADS = 2
GROUP = N_HEADS // N_KV_HEADS
Q_DIM = N_HEADS * HEAD_DIM
KV_DIM = N_KV_HEADS * HEAD_DIM
QKV_DIM = Q_DIM + 2 * KV_DIM
GRID_W = 64
AXIS_ROT_DIM = HEAD_DIM // 2
ROPE_THETA = 10000.0
N_FGROUPS = 8
FGROUP_DIM = D_MODEL // N_FGROUPS
EPS = 1e-6

LANES = 128
VMEM_LIMIT = 56 * 1024 * 1024

F32 = jnp.float32
BF16 = jnp.bfloat16


def _rms(x, g):
    ms = jnp.mean(x * x, axis=-1, keepdims=True)
    return x * lax.rsqrt(ms + EPS) * g


def _dft_cos_sin(n):
    k = np.arange(n)
    ang = 2.0 * np.pi * ((k[:, None] * k[None, :]) % n) / n
    return np.cos(ang), np.sin(ang)


def _fourier_a_kernel(x_ref, g_ref, f1_ref, twr_ref, twi_ref, zr_ref, zi_ref, *, n1, ts2):
    d = D_MODEL
    h = _rms(x_ref[...], g_ref[...])
    ht = jnp.swapaxes(h, 0, 1).astype(BF16)
    f1 = f1_ref[...]
    zr_parts, zi_parts = [], []
    for j in range(ts2):
        z = jnp.dot(f1, ht[j], preferred_element_type=F32)
        zr, zi = z[:n1], z[n1:]
        tr = jnp.tile(twr_ref[j], (1, d // LANES))
        ti = jnp.tile(twi_ref[j], (1, d // LANES))
        zr_parts.append(zr * tr - zi * ti)
        zi_parts.append(zr * ti + zi * tr)
    zr_ref[...] = jnp.swapaxes(jnp.stack(zr_parts, axis=0), 0, 1).astype(BF16)
    zi_ref[...] = jnp.swapaxes(jnp.stack(zi_parts, axis=0), 0, 1).astype(BF16)


def _fourier_c_kernel(zr_ref, zi_ref, x_ref, f2_ref, cs_ref, w_ref, o_ref, *, n2, tk1, scale):
    d = D_MODEL
    f2 = f2_ref[...]
    xs = []
    for j in range(tk1):
        zin = jnp.concatenate([zr_ref[j], zi_ref[j]], axis=0)
        xs.append(jnp.dot(f2, zin, preferred_element_type=F32))
    xr = jnp.concatenate([x[:n2] for x in xs], axis=0).astype(BF16)
    xi = jnp.concatenate([x[n2:] for x in xs], axis=0).astype(BF16)
    cs = cs_ref[...]
    mixed = []
    for gi in range(N_FGROUPS):
        sl = slice(gi * FGROUP_DIM, (gi + 1) * FGROUP_DIM)
        lhs = jnp.concatenate([xr[:, sl], xi[:, sl]], axis=1)
        mixed.append(jnp.dot(lhs, cs, preferred_element_type=F32) * scale)
    mixed = jnp.concatenate(mixed, axis=1).astype(BF16)
    delta = jnp.dot(mixed, w_ref[...], preferred_element_type=F32)
    o_ref[...] = x_ref[...] + jnp.swapaxes(delta.reshape(tk1, n2, d), 0, 1)


def _fourier_tables(seq_len):
    n1 = 64
    n2 = seq_len // n1
    c1, s1 = _dft_cos_sin(n1)
    f1 = np.concatenate([c1, -s1], axis=0)
    c2, s2 = _dft_cos_sin(n2)
    f2 = np.block([[c2, s2], [-s2, c2]])
    cc, sc = _dft_cos_sin(FGROUP_DIM)
    cs = np.concatenate([cc, sc], axis=0)
    ang = 2.0 * np.pi * (np.arange(n2)[:, None] * np.arange(n1)[None, :]) / seq_len
    twr = np.cos(ang)[:, :, None].astype(np.float32)
    twi = (-np.sin(ang))[:, :, None].astype(np.float32)
    return n1, n2, f1, f2, cs, twr, twi


def _fourier_layer(x, g, w_bf16):
    b, s, d = x.shape
    n1, n2, f1, f2, cs, twr, twi = _fourier_tables(s)
    ts2 = 16
    tk1 = 8
    f1 = jnp.asarray(f1, BF16)
    f2 = jnp.asarray(f2, BF16)
    cs = jnp.asarray(cs, BF16)
    twr = jnp.broadcast_to(jnp.asarray(twr), (n2, n1, LANES))
    twi = jnp.broadcast_to(jnp.asarray(twi), (n2, n1, LANES))
    g2 = g.reshape(1, d)

    z_shape = jax.ShapeDtypeStruct((b, n1, n2, d), BF16)
    zr, zi = pl.pallas_call(
        functools.partial(_fourier_a_kernel, n1=n1, ts2=ts2),
        grid=(b, n2 // ts2),
        in_specs=[
            pl.BlockSpec((None, n1, ts2, d), lambda bi, i: (bi, 0, i, 0)),
            pl.BlockSpec((1, d), lambda bi, i: (0, 0)),
            pl.BlockSpec((2 * n1, n1), lambda bi, i: (0, 0)),
            pl.BlockSpec((ts2, n1, LANES), lambda bi, i: (i, 0, 0)),
            pl.BlockSpec((ts2, n1, LANES), lambda bi, i: (i, 0, 0)),
        ],
        out_specs=[
            pl.BlockSpec((None, n1, ts2, d), lambda bi, i: (bi, 0, i, 0)),
            pl.BlockSpec((None, n1, ts2, d), lambda bi, i: (bi, 0, i, 0)),
        ],
        out_shape=[z_shape, z_shape],
        compiler_params=pltpu.CompilerParams(
            dimension_semantics=("parallel", "parallel"), vmem_limit_bytes=VMEM_LIMIT),
        name="fourier_a",
    )(x.reshape(b, n1, n2, d), g2, f1, twr, twi)

    scale = float(1.0 / np.sqrt(s * FGROUP_DIM))
    out = pl.pallas_call(
        functools.partial(_fourier_c_kernel, n2=n2, tk1=tk1, scale=scale),
        grid=(b, n1 // tk1),
        in_specs=[
            pl.BlockSpec((None, tk1, n2, d), lambda bi, i: (bi, i, 0, 0)),
            pl.BlockSpec((None, tk1, n2, d), lambda bi, i: (bi, i, 0, 0)),
            pl.BlockSpec((None, n2, tk1, d), lambda bi, i: (bi, 0, i, 0)),
            pl.BlockSpec((2 * n2, 2 * n2), lambda bi, i: (0, 0)),
            pl.BlockSpec((2 * FGROUP_DIM, FGROUP_DIM), lambda bi, i: (0, 0)),
            pl.BlockSpec((d, d), lambda bi, i: (0, 0)),
        ],
        out_specs=pl.BlockSpec((None, n2, tk1, d), lambda bi, i: (bi, 0, i, 0)),
        out_shape=jax.ShapeDtypeStruct((b, n2, n1, d), F32),
        compiler_params=pltpu.CompilerParams(
            dimension_semantics=("parallel", "parallel"), vmem_limit_bytes=VMEM_LIMIT),
        name="fourier_c",
    )(zr, zi, x.reshape(b, n2, n1, d), f2, cs, w_bf16)
    return out.reshape(b, s, d)


def _qkv_kernel(x_ref, g_ref, w_ref, qg_ref, kg_ref, cos_ref, sin_ref, q_ref, k_ref, vt_ref):
    h = _rms(x_ref[...], g_ref[...]).astype(BF16)
    qkv = jnp.dot(h, w_ref[...], preferred_element_type=F32)
    cos = cos_ref[...]
    sin = sin_ref[...]

    def norm_rope(y, gain):
        yn = _rms(y, gain)
        return yn * cos + pltpu.roll(yn, HEAD_DIM // 2, axis=1) * sin

    qg = qg_ref[...]
    kg = kg_ref[...]
    q_scale = HEAD_DIM ** -0.5 * np.log2(np.e)
    for hd in range(N_HEADS):
        sl = slice(hd * HEAD_DIM, (hd + 1) * HEAD_DIM)
        q_ref[:, sl] = (norm_rope(qkv[:, sl], qg) * q_scale).astype(BF16)
    for hd in range(N_KV_HEADS):
        sl = slice(hd * HEAD_DIM, (hd + 1) * HEAD_DIM)
        k_ref[:, sl] = norm_rope(qkv[:, Q_DIM + hd * HEAD_DIM:Q_DIM + (hd + 1) * HEAD_DIM], kg).astype(BF16)
    vt_ref[...] = qkv[:, Q_DIM + KV_DIM:].T.astype(BF16)


def _attn_kernel(q_ref, k_ref, vt_ref, o_ref, s0_ref, s1_ref, acc_ref, *, tq, tk):
    n_chunks = vt_ref.shape[0]
    q = jnp.concatenate([q_ref[:, gi * HEAD_DIM:(gi + 1) * HEAD_DIM] for gi in range(GROUP)], axis=0)
    m_cols = GROUP * tq

    def scores(j, dst_ref):
        start = pl.multiple_of(j * tk, tk)
        kc = k_ref[pl.ds(start, tk), :]
        st = lax.dot_general(kc, q, (((1,), (1,)), ((), ())), preferred_element_type=F32)
        dst_ref[...] = st
        return jnp.max(st, axis=0, keepdims=True)

    def consume(j, src_ref, m, alpha, l):
        p = jnp.exp2(src_ref[...] - m)
        l = alpha * l + jnp.sum(p, axis=0, keepdims=True)
        pv = jnp.dot(vt_ref[j], p.astype(BF16), preferred_element_type=F32)
        acc_ref[...] = alpha * acc_ref[...] + pv
        return l

    def step(j, cur_ref, nxt_ref, carry):
        m, alpha, l = carry
        cmax = scores(j + 1, nxt_ref)
        l = consume(j, cur_ref, m, alpha, l)
        m_next = jnp.maximum(m, cmax)
        return m_next, jnp.exp2(m - m_next), l

    def body(jj, carry):
        carry = step(2 * jj, s0_ref, s1_ref, carry)
        return step(2 * jj + 1, s1_ref, s0_ref, carry)

    acc_ref[...] = jnp.zeros_like(acc_ref)
    m0 = scores(0, s0_ref)
    zero = jnp.zeros((1, m_cols), F32)
    carry = lax.fori_loop(0, n_chunks // 2 - 1, body, (m0, zero, zero))
    m, alpha, l = step(n_chunks - 2, s0_ref, s1_ref, carry)
    l = consume(n_chunks - 1, s1_ref, m, alpha, l)
    o_t = acc_ref[...] / l
    for gi in range(GROUP):
        o_ref[:, gi * HEAD_DIM:(gi + 1) * HEAD_DIM] = o_t[:, gi * tq:(gi + 1) * tq].T.astype(BF16)


def _rope_tables(seq_len):
    rows = seq_len // GRID_W
    row = jnp.repeat(jnp.arange(rows, dtype=F32), GRID_W)
    col = jnp.tile(jnp.arange(GRID_W, dtype=F32), rows)
    inv = ROPE_THETA ** (-jnp.arange(0, AXIS_ROT_DIM, 2, dtype=F32) / AXIS_ROT_DIM)
    ang = jnp.concatenate([row[:, None] * inv, col[:, None] * inv], axis=-1)
    cos, sin = jnp.cos(ang), jnp.sin(ang)
    return jnp.concatenate([cos, cos], axis=-1), jnp.concatenate([-sin, sin], axis=-1)


def _attention_qkv(x, g, w_qkv_bf16, q_gain, k_gain):
    b, s, d = x.shape
    t = b * s
    tq = 256
    tk = 512
    tm = tk
    cos, sin = _rope_tables(s)
    nblk = s // tm
    q, k, vt = pl.pallas_call(
        _qkv_kernel,
        grid=(t // tm,),
        in_specs=[
            pl.BlockSpec((tm, d), lambda i: (i, 0)),
            pl.BlockSpec((1, d), lambda i: (0, 0)),
            pl.BlockSpec((d, QKV_DIM), lambda i: (0, 0)),
            pl.BlockSpec((1, HEAD_DIM), lambda i: (0, 0)),
            pl.BlockSpec((1, HEAD_DIM), lambda i: (0, 0)),
            pl.BlockSpec((tm, HEAD_DIM), lambda i: (i % nblk, 0)),
            pl.BlockSpec((tm, HEAD_DIM), lambda i: (i % nblk, 0)),
        ],
        out_specs=[
            pl.BlockSpec((tm, Q_DIM), lambda i: (i, 0)),
            pl.BlockSpec((tm, KV_DIM), lambda i: (i, 0)),
            pl.BlockSpec((None, KV_DIM, tm), lambda i: (i, 0, 0)),
        ],
        out_shape=[
            jax.ShapeDtypeStruct((t, Q_DIM), BF16),
            jax.ShapeDtypeStruct((t, KV_DIM), BF16),
            jax.ShapeDtypeStruct((t // tm, KV_DIM, tm), BF16),
        ],
        compiler_params=pltpu.CompilerParams(
            dimension_semantics=("parallel",), vmem_limit_bytes=VMEM_LIMIT),
        name="qkv_proj",
    )(x.reshape(t, d), g.reshape(1, d), w_qkv_bf16, q_gain.reshape(1, HEAD_DIM),
      k_gain.reshape(1, HEAD_DIM), cos, sin)

    o = pl.pallas_call(
        functools.partial(_attn_kernel, tq=tq, tk=tk),
        grid=(b, N_KV_HEADS, s // tq),
        in_specs=[
            pl.BlockSpec((None, tq, GROUP * HEAD_DIM), lambda bi, hi, i: (bi, i, hi)),
            pl.BlockSpec((None, s, HEAD_DIM), lambda bi, hi, i: (bi, 0, hi)),
            pl.BlockSpec((None, s // tk, HEAD_DIM, tk), lambda bi, hi, i: (bi, 0, hi, 0)),
        ],
        out_specs=pl.BlockSpec((None, tq, GROUP * HEAD_DIM), lambda bi, hi, i: (bi, i, hi)),
        out_shape=jax.ShapeDtypeStruct((b, s, Q_DIM), BF16),
        scratch_shapes=[
            pltpu.VMEM((tk, GROUP * tq), F32),
            pltpu.VMEM((tk, GROUP * tq), F32),
            pltpu.VMEM((HEAD_DIM, GROUP * tq), F32),
        ],
        compiler_params=pltpu.CompilerParams(
            dimension_semantics=("parallel", "parallel", "parallel"), vmem_limit_bytes=VMEM_LIMIT),
        name="attention",
    )(q.reshape(b, s, Q_DIM), k.reshape(b, s, KV_DIM), vt.reshape(b, s // tk, KV_DIM, tk))
    return o.reshape(t, Q_DIM)


def _ffn_kernel(*refs, with_attn):
    if with_attn:
        x_ref, a_ref, wo_ref, g_ref, wg_ref, wu_ref, wd_ref, o_ref = refs
        x = x_ref[...] + jnp.dot(a_ref[...], wo_ref[...], preferred_element_type=F32)
    else:
        x_ref, g_ref, wg_ref, wu_ref, wd_ref, o_ref = refs
        x = x_ref[...]
    h = _rms(x, g_ref[...]).astype(BF16)
    gate = jnp.dot(h, wg_ref[...], preferred_element_type=F32)
    up = jnp.dot(h, wu_ref[...], preferred_element_type=F32)
    act = (gate * jax.nn.sigmoid(gate) * up).astype(BF16)
    o_ref[...] = x + jnp.dot(act, wd_ref[...], preferred_element_type=F32)


def _ffn_layer(x2d, g, wg, wu, wd, attn=None):
    t, d = x2d.shape
    dff = wg.shape[1]
    tm = 512
    resident = functools.partial(pl.BlockSpec, index_map=lambda i: (0, 0), pipeline_mode=pl.Buffered(1))
    row = lambda width: pl.BlockSpec((tm, width), lambda i: (i, 0))
    in_specs = [row(d)]
    args = [x2d]
    if attn is not None:
        a2d, wo = attn
        in_specs += [row(Q_DIM), resident((Q_DIM, d))]
        args += [a2d, wo]
    in_specs += [resident((1, d)), resident((d, dff)), resident((d, dff)), resident((dff, d))]
    args += [g.reshape(1, d), wg, wu, wd]
    return pl.pallas_call(
        functools.partial(_ffn_kernel, with_attn=attn is not None),
        grid=(t // tm,),
        in_specs=in_specs,
        out_specs=row(d),
        out_shape=jax.ShapeDtypeStruct((t, d), F32),
        compiler_params=pltpu.CompilerParams(
            dimension_semantics=("parallel",), vmem_limit_bytes=VMEM_LIMIT),
        name="swiglu",
    )(*args)


def _pair_split_perm():
    p = np.concatenate([np.arange(0, HEAD_DIM, 2), np.arange(1, HEAD_DIM, 2)])
    cols = [h * HEAD_DIM + p for h in range(N_HEADS + N_KV_HEADS)]
    cols.append(np.arange(Q_DIM + KV_DIM, QKV_DIM))
    return p, np.concatenate(cols)


def kernel(x_prompt, x_sample, norm_mix, norm_ffn, fourier_w, attn_w_qkv, attn_q_norm, attn_k_norm,
           attn_w_o, ffn_w_gate, ffn_w_up, ffn_w_down):
    depth = norm_mix.shape[0]
    head_perm, col_perm = _pair_split_perm()
    fw = fourier_w.astype(BF16)
    wqkv = attn_w_qkv[:, :, col_perm].astype(BF16)
    qn = attn_q_norm[:, head_perm]
    kn = attn_k_norm[:, head_perm]
    wo = attn_w_o.astype(BF16)
    wg = ffn_w_gate.astype(BF16)
    wu = ffn_w_up.astype(BF16)
    wd = ffn_w_down.astype(BF16)

    def run_trunk(x):
        b, s, d = x.shape
        for i in range(depth):
            j = i // 2
            if i % 2 == 0:
                x = _fourier_layer(x, norm_mix[i], fw[j])
                x2d = _ffn_layer(x.reshape(b * s, d), norm_ffn[i], wg[i], wu[i], wd[i])
            else:
                a2d = _attention_qkv(x, norm_mix[i], wqkv[j], qn[j], kn[j])
                x2d = _ffn_layer(x.reshape(b * s, d), norm_ffn[i], wg[i], wu[i], wd[i], attn=(a2d, wo[j]))
            x = x2d.reshape(b, s, d)
        return x

    return (run_trunk(x_prompt), run_trunk(x_sample))
```

```python
import functools

import numpy as np
import jax
import jax.numpy as jnp
from jax import lax
from jax.experimental import pallas as pl
from jax.experimental.pallas import tpu as pltpu

D_MODEL = 1024
HEAD_DIM = 128
N_HEADS = D_MODEL // HEAD_DIM
N_KV_HEADS = 2
GROUP = N_HEADS // N_KV_HEADS
Q_DIM = N_HEADS * HEAD_DIM
KV_DIM = N_KV_HEADS * HEAD_DIM
QKV_DIM = Q_DIM + 2 * KV_DIM
GRID_W = 64
AXIS_ROT_DIM = HEAD_DIM // 2
ROPE_THETA = 10000.0
N_FGROUPS = 8
FGROUP_DIM = D_MODEL // N_FGROUPS
EPS = 1e-6

LANES = 128
VMEM_LIMIT = 56 * 1024 * 1024

F32 = jnp.float32
BF16 = jnp.bfloat16


def _rms(x, g):
    ms = jnp.mean(x * x, axis=-1, keepdims=True)
    return x * lax.rsqrt(ms + EPS) * g


def _dft_cos_sin(n):
    k = np.arange(n)
    ang = 2.0 * np.pi * ((k[:, None] * k[None, :]) % n) / n
    return np.cos(ang), np.sin(ang)


def _fourier_a_kernel(x_ref, g_ref, f1_ref, twr_ref, twi_ref, zr_ref, zi_ref, *, n1, ts2):
    d = D_MODEL
    h = _rms(x_ref[...], g_ref[...])
    ht = jnp.swapaxes(h, 0, 1).astype(BF16)
    f1 = f1_ref[...]
    zr_parts, zi_parts = [], []
    for j in range(ts2):
        z = jnp.dot(f1, ht[j], preferred_element_type=F32)
        zr, zi = z[:n1], z[n1:]
        tr = jnp.tile(twr_ref[j], (1, d // LANES))
        ti = jnp.tile(twi_ref[j], (1, d // LANES))
        zr_parts.append(zr * tr - zi * ti)
        zi_parts.append(zr * ti + zi * tr)
    zr_ref[...] = jnp.swapaxes(jnp.stack(zr_parts, axis=0), 0, 1).astype(BF16)
    zi_ref[...] = jnp.swapaxes(jnp.stack(zi_parts, axis=0), 0, 1).astype(BF16)


def _fourier_c_kernel(zr_ref, zi_ref, x_ref, f2_ref, cs_ref, w_ref, o_ref, *, n2, tk1, scale):
    d = D_MODEL
    f2 = f2_ref[...]
    xs = []
    for j in range(tk1):
        zin = jnp.concatenate([zr_ref[j], zi_ref[j]], axis=0)
        xs.append(jnp.dot(f2, zin, preferred_element_type=F32))
    xr = jnp.concatenate([x[:n2] for x in xs], axis=0).astype(BF16)
    xi = jnp.concatenate([x[n2:] for x in xs], axis=0).astype(BF16)
    cs = cs_ref[...]
    mixed = []
    for gi in range(N_FGROUPS):
        sl = slice(gi * FGROUP_DIM, (gi + 1) * FGROUP_DIM)
        lhs = jnp.concatenate([xr[:, sl], xi[:, sl]], axis=1)
        mixed.append(jnp.dot(lhs, cs, preferred_element_type=F32) * scale)
    mixed = jnp.concatenate(mixed, axis=1).astype(BF16)
    delta = jnp.dot(mixed, w_ref[...], preferred_element_type=F32)
    o_ref[...] = x_ref[...] + jnp.swapaxes(delta.reshape(tk1, n2, d), 0, 1)


def _fourier_tables(seq_len):
    n1 = 64
    n2 = seq_len // n1
    c1, s1 = _dft_cos_sin(n1)
    f1 = np.concatenate([c1, -s1], axis=0)
    c2, s2 = _dft_cos_sin(n2)
    f2 = np.block([[c2, s2], [-s2, c2]])
    cc, sc = _dft_cos_sin(FGROUP_DIM)
    cs = np.concatenate([cc, sc], axis=0)
    ang = 2.0 * np.pi * (np.arange(n2)[:, None] * np.arange(n1)[None, :]) / seq_len
    twr = np.cos(ang)[:, :, None].astype(np.float32)
    twi = (-np.sin(ang))[:, :, None].astype(np.float32)
    return n1, n2, f1, f2, cs, twr, twi


def _fourier_layer(x, g, w_bf16):
    b, s, d = x.shape
    n1, n2, f1, f2, cs, twr, twi = _fourier_tables(s)
    ts2 = 16
    tk1 = 8
    f1 = jnp.asarray(f1, BF16)
    f2 = jnp.asarray(f2, BF16)
    cs = jnp.asarray(cs, BF16)
    twr = jnp.broadcast_to(jnp.asarray(twr), (n2, n1, LANES))
    twi = jnp.broadcast_to(jnp.asarray(twi), (n2, n1, LANES))
    g2 = g.reshape(1, d)

    z_shape = jax.ShapeDtypeStruct((b, n1, n2, d), BF16)
    zr, zi = pl.pallas_call(
        functools.partial(_fourier_a_kernel, n1=n1, ts2=ts2),
        grid=(b, n2 // ts2),
        in_specs=[
            pl.BlockSpec((None, n1, ts2, d), lambda bi, i: (bi, 0, i, 0)),
            pl.BlockSpec((1, d), lambda bi, i: (0, 0)),
            pl.BlockSpec((2 * n1, n1), lambda bi, i: (0, 0)),
            pl.BlockSpec((ts2, n1, LANES), lambda bi, i: (i, 0, 0)),
            pl.BlockSpec((ts2, n1, LANES), lambda bi, i: (i, 0, 0)),
        ],
        out_specs=[
            pl.BlockSpec((None, n1, ts2, d), lambda bi, i: (bi, 0, i, 0)),
            pl.BlockSpec((None, n1, ts2, d), lambda bi, i: (bi, 0, i, 0)),
        ],
        out_shape=[z_shape, z_shape],
        compiler_params=pltpu.CompilerParams(
            dimension_semantics=("parallel", "parallel"), vmem_limit_bytes=VMEM_LIMIT),
        name="fourier_a",
    )(x.reshape(b, n1, n2, d), g2, f1, twr, twi)

    scale = float(1.0 / np.sqrt(s * FGROUP_DIM))
    out = pl.pallas_call(
        functools.partial(_fourier_c_kernel, n2=n2, tk1=tk1, scale=scale),
        grid=(b, n1 // tk1),
        in_specs=[
            pl.BlockSpec((None, tk1, n2, d), lambda bi, i: (bi, i, 0, 0)),
            pl.BlockSpec((None, tk1, n2, d), lambda bi, i: (bi, i, 0, 0)),
            pl.BlockSpec((None, n2, tk1, d), lambda bi, i: (bi, 0, i, 0)),
            pl.BlockSpec((2 * n2, 2 * n2), lambda bi, i: (0, 0)),
            pl.BlockSpec((2 * FGROUP_DIM, FGROUP_DIM), lambda bi, i: (0, 0)),
            pl.BlockSpec((d, d), lambda bi, i: (0, 0)),
        ],
        out_specs=pl.BlockSpec((None, n2, tk1, d), lambda bi, i: (bi, 0, i, 0)),
        out_shape=jax.ShapeDtypeStruct((b, n2, n1, d), F32),
        compiler_params=pltpu.CompilerParams(
            dimension_semantics=("parallel", "parallel"), vmem_limit_bytes=VMEM_LIMIT),
        name="fourier_c",
    )(zr, zi, x.reshape(b, n2, n1, d), f2, cs, w_bf16)
    return out.reshape(b, s, d)


def _qkv_kernel(x_ref, g_ref, w_ref, qg_ref, kg_ref, cos_ref, sin_ref, q_ref, k_ref, vt_ref):
    h = _rms(x_ref[...], g_ref[...]).astype(BF16)
    qkv = jnp.dot(h, w_ref[...], preferred_element_type=F32)
    cos = cos_ref[...]
    sin = sin_ref[...]

    def norm_rope(y, gain):
        yn = _rms(y, gain)
        return yn * cos + pltpu.roll(yn, HEAD_DIM // 2, axis=1) * sin

    qg = qg_ref[...]
    kg = kg_ref[...]
    q_scale = HEAD_DIM ** -0.5 * np.log2(np.e)
    for hd in range(N_HEADS):
        sl = slice(hd * HEAD_DIM, (hd + 1) * HEAD_DIM)
        q_ref[:, sl] = (norm_rope(qkv[:, sl], qg) * q_scale).astype(BF16)
    for hd in range(N_KV_HEADS):
        sl = slice(hd * HEAD_DIM, (hd + 1) * HEAD_DIM)
        k_ref[:, sl] = norm_rope(qkv[:, Q_DIM + hd * HEAD_DIM:Q_DIM + (hd + 1) * HEAD_DIM], kg).astype(BF16)
    vt_ref[...] = qkv[:, Q_DIM + KV_DIM:].T.astype(BF16)


def _attn_kernel(q_ref, k_ref, vt_ref, o_ref, s0_ref, s1_ref, acc_ref, *, tq, tk):
    n_chunks = vt_ref.shape[0]
    n_items = (q_ref.shape[0] // tq) * n_chunks
    m_cols = GROUP * tq

    def scores(t, dst_ref):
        q_start = pl.multiple_of((t // n_chunks) * tq, tq)
        k_start = pl.multiple_of((t % n_chunks) * tk, tk)
        qb = q_ref[pl.ds(q_start, tq), :]
        q = jnp.concatenate([qb[:, gi * HEAD_DIM:(gi + 1) * HEAD_DIM] for gi in range(GROUP)], axis=0)
        kc = k_ref[pl.ds(k_start, tk), :]
        st = lax.dot_general(kc, q, (((1,), (1,)), ((), ())), preferred_element_type=F32)
        dst_ref[...] = st
        return jnp.max(st, axis=0, keepdims=True)

    def consume(t, src_ref, m, alpha, l):
        p = jnp.exp2(src_ref[...] - m)
        l = alpha * l + jnp.sum(p, axis=0, keepdims=True)
        pv = jnp.dot(vt_ref[t % n_chunks], p.astype(BF16), preferred_element_type=F32)
        acc_ref[...] = alpha * acc_ref[...] + pv
        return l

    def finalize(t, l):
        o_t = acc_ref[...] / l
        q_start = pl.multiple_of((t // n_chunks) * tq, tq)
        for gi in range(GROUP):
            o_ref[pl.ds(q_start, tq), gi * HEAD_DIM:(gi + 1) * HEAD_DIM] = (
                o_t[:, gi * tq:(gi + 1) * tq].T.astype(BF16))

    def step(t, cur_ref, nxt_ref, carry, maybe_last):
        m, alpha, l = carry
        cmax = scores(t + 1, nxt_ref)
        l = consume(t, cur_ref, m, alpha, l)
        if not maybe_last:
            m_next = jnp.maximum(m, cmax)
            return m_next, jnp.exp2(m - m_next), l
        last = (t % n_chunks) == n_chunks - 1
        pl.when(last)(lambda: finalize(t, l))
        m_next = jnp.where(last, cmax, jnp.maximum(m, cmax))
        alpha_next = jnp.where(last, 0.0, jnp.exp2(m - m_next))
        return m_next, alpha_next, l

    def body(tt, carry):
        carry = step(2 * tt, s0_ref, s1_ref, carry, False)
        return step(2 * tt + 1, s1_ref, s0_ref, carry, True)

    acc_ref[...] = jnp.zeros_like(acc_ref)
    m0 = scores(0, s0_ref)
    zero = jnp.zeros((1, m_cols), F32)
    carry = lax.fori_loop(0, n_items // 2 - 1, body, (m0, zero, zero))
    m, alpha, l = step(n_items - 2, s0_ref, s1_ref, carry, False)
    l = consume(n_items - 1, s1_ref, m, alpha, l)
    finalize(n_items - 1, l)


def _rope_tables(seq_len):
    rows = seq_len // GRID_W
    row = jnp.repeat(jnp.arange(rows, dtype=F32), GRID_W)
    col = jnp.tile(jnp.arange(GRID_W, dtype=F32), rows)
    inv = ROPE_THETA ** (-jnp.arange(0, AXIS_ROT_DIM, 2, dtype=F32) / AXIS_ROT_DIM)
    ang = jnp.concatenate([row[:, None] * inv, col[:, None] * inv], axis=-1)
    cos, sin = jnp.cos(ang), jnp.sin(ang)
    return jnp.concatenate([cos, cos], axis=-1), jnp.concatenate([-sin, sin], axis=-1)


def _attention_qkv(x, g, w_qkv_bf16, q_gain, k_gain):
    b, s, d = x.shape
    t = b * s
    tq = 256
    tk = 1024
    tm = tk
    cos, sin = _rope_tables(s)
    nblk = s // tm
    q, k, vt = pl.pallas_call(
        _qkv_kernel,
        grid=(t // tm,),
        in_specs=[
            pl.BlockSpec((tm, d), lambda i: (i, 0)),
            pl.BlockSpec((1, d), lambda i: (0, 0)),
            pl.BlockSpec((d, QKV_DIM), lambda i: (0, 0)),
            pl.BlockSpec((1, HEAD_DIM), lambda i: (0, 0)),
            pl.BlockSpec((1, HEAD_DIM), lambda i: (0, 0)),
            pl.BlockSpec((tm, HEAD_DIM), lambda i: (i % nblk, 0)),
            pl.BlockSpec((tm, HEAD_DIM), lambda i: (i % nblk, 0)),
        ],
        out_specs=[
            pl.BlockSpec((tm, Q_DIM), lambda i: (i, 0)),
            pl.BlockSpec((tm, KV_DIM), lambda i: (i, 0)),
            pl.BlockSpec((None, KV_DIM, tm), lambda i: (i, 0, 0)),
        ],
        out_shape=[
            jax.ShapeDtypeStruct((t, Q_DIM), BF16),
            jax.ShapeDtypeStruct((t, KV_DIM), BF16),
            jax.ShapeDtypeStruct((t // tm, KV_DIM, tm), BF16),
        ],
        compiler_params=pltpu.CompilerParams(
            dimension_semantics=("parallel",), vmem_limit_bytes=VMEM_LIMIT),
        name="qkv_proj",
    )(x.reshape(t, d), g.reshape(1, d), w_qkv_bf16, q_gain.reshape(1, HEAD_DIM),
      k_gain.reshape(1, HEAD_DIM), cos, sin)

    o = pl.pallas_call(
        functools.partial(_attn_kernel, tq=tq, tk=tk),
        grid=(b, N_KV_HEADS),
        in_specs=[
            pl.BlockSpec((None, s, GROUP * HEAD_DIM), lambda bi, hi: (bi, 0, hi),
                         pipeline_mode=pl.Buffered(1)),
            pl.BlockSpec((None, s, HEAD_DIM), lambda bi, hi: (bi, 0, hi)),
            pl.BlockSpec((None, s // tk, HEAD_DIM, tk), lambda bi, hi: (bi, 0, hi, 0)),
        ],
        out_specs=pl.BlockSpec((None, s, GROUP * HEAD_DIM), lambda bi, hi: (bi, 0, hi)),
        out_shape=jax.ShapeDtypeStruct((b, s, Q_DIM), BF16),
        scratch_shapes=[
            pltpu.VMEM((tk, GROUP * tq), F32),
            pltpu.VMEM((tk, GROUP * tq), F32),
            pltpu.VMEM((HEAD_DIM, GROUP * tq), F32),
        ],
        compiler_params=pltpu.CompilerParams(
            dimension_semantics=("parallel", "parallel"), vmem_limit_bytes=VMEM_LIMIT),
        name="attention",
    )(q.reshape(b, s, Q_DIM), k.reshape(b, s, KV_DIM), vt.reshape(b, s // tk, KV_DIM, tk))
    return o.reshape(t, Q_DIM)


def _ffn_kernel(*refs, with_attn):
    if with_attn:
        x_ref, a_ref, wo_ref, g_ref, wg_ref, wu_ref, wd_ref, o_ref = refs
        x = x_ref[...] + jnp.dot(a_ref[...], wo_ref[...], preferred_element_type=F32)
    else:
        x_ref, g_ref, wg_ref, wu_ref, wd_ref, o_ref = refs
        x = x_ref[...]
    h = _rms(x, g_ref[...]).astype(BF16)
    gate = jnp.dot(h, wg_ref[...], preferred_element_type=F32)
    up = jnp.dot(h, wu_ref[...], preferred_element_type=F32)
    act = (gate * jax.nn.sigmoid(gate) * up).astype(BF16)
    o_ref[...] = x + jnp.dot(act, wd_ref[...], preferred_element_type=F32)


def _ffn_layer(x2d, g, wg, wu, wd, attn=None):
    t, d = x2d.shape
    dff = wg.shape[1]
    tm = 512
    resident = functools.partial(pl.BlockSpec, index_map=lambda i: (0, 0), pipeline_mode=pl.Buffered(1))
    row = lambda width: pl.BlockSpec((tm, width), lambda i: (i, 0))
    in_specs = [row(d)]
    args = [x2d]
    if attn is not None:
        a2d, wo = attn
        in_specs += [row(Q_DIM), resident((Q_DIM, d))]
        args += [a2d, wo]
    in_specs += [resident((1, d)), resident((d, dff)), resident((d, dff)), resident((dff, d))]
    args += [g.reshape(1, d), wg, wu, wd]
    return pl.pallas_call(
        functools.partial(_ffn_kernel, with_attn=attn is not None),
        grid=(t // tm,),
        in_specs=in_specs,
        out_specs=row(d),
        out_shape=jax.ShapeDtypeStruct((t, d), F32),
        compiler_params=pltpu.CompilerParams(
            dimension_semantics=("parallel",), vmem_limit_bytes=VMEM_LIMIT),
        name="swiglu",
    )(*args)


def _pair_split_perm():
    p = np.concatenate([np.arange(0, HEAD_DIM, 2), np.arange(1, HEAD_DIM, 2)])
    cols = [h * HEAD_DIM + p for h in range(N_HEADS + N_KV_HEADS)]
    cols.append(np.arange(Q_DIM + KV_DIM, QKV_DIM))
    return p, np.concatenate(cols)


def kernel(x_prompt, x_sample, norm_mix, norm_ffn, fourier_w, attn_w_qkv, attn_q_norm, attn_k_norm,
           attn_w_o, ffn_w_gate, ffn_w_up, ffn_w_down):
    depth = norm_mix.shape[0]
    head_perm, col_perm = _pair_split_perm()
    fw = fourier_w.astype(BF16)
    wqkv = attn_w_qkv[:, :, col_perm].astype(BF16)
    qn = attn_q_norm[:, head_perm]
    kn = attn_k_norm[:, head_perm]
    wo = attn_w_o.astype(BF16)
    wg = ffn_w_gate.astype(BF16)
    wu = ffn_w_up.astype(BF16)
    wd = ffn_w_down.astype(BF16)

    def run_trunk(x):
        b, s, d = x.shape
        for i in range(depth):
            j = i // 2
            if i % 2 == 0:
                x = _fourier_layer(x, norm_mix[i], fw[j])
                x2d = _ffn_layer(x.reshape(b * s, d), norm_ffn[i], wg[i], wu[i], wd[i])
            else:
                a2d = _attention_qkv(x, norm_mix[i], wqkv[j], qn[j], kn[j])
                x2d = _ffn_layer(x.reshape(b * s, d), norm_ffn[i], wg[i], wu[i], wd[i], attn=(a2d, wo[j]))
            x = x2d.reshape(b, s, d)
        return x

    return (run_trunk(x_prompt), run_trunk(x_sample))
```

```python
import functools

import numpy as np
import jax
import jax.numpy as jnp
from jax import lax
from jax.experimental import pallas as pl
from jax.experimental.pallas import tpu as pltpu

D_MODEL = 1024
HEAD_DIM = 128
N_HEADS = D_MODEL // HEAD_DIM
N_KV_HEADS = 2
GROUP = N_HEADS // N_KV_HEADS
Q_DIM = N_HEADS * HEAD_DIM
KV_DIM = N_KV_HEADS * HEAD_DIM
QKV_DIM = Q_DIM + 2 * KV_DIM
GRID_W = 64
AXIS_ROT_DIM = HEAD_DIM // 2
ROPE_THETA = 10000.0
N_FGROUPS = 8
FGROUP_DIM = D_MODEL // N_FGROUPS
EPS = 1e-6

ATTN_Q_SCALE = float(HEAD_DIM ** -0.5 * np.log2(np.e))
ATTN_BOUNDED_MAX_LOG2_SCORE = 60.0

LANES = 128
VMEM_LIMIT = 56 * 1024 * 1024

F32 = jnp.float32
BF16 = jnp.bfloat16


def _rms(x, g):
    ms = jnp.mean(x * x, axis=-1, keepdims=True)
    return x * lax.rsqrt(ms + EPS) * g


def _dft_cos_sin(n):
    k = np.arange(n)
    ang = 2.0 * np.pi * ((k[:, None] * k[None, :]) % n) / n
    return np.cos(ang), np.sin(ang)


def _fourier_a_kernel(x_ref, g_ref, f1_ref, twr_ref, twi_ref, zr_ref, zi_ref, *, n1, ts2):
    d = D_MODEL
    h = _rms(x_ref[...], g_ref[...])
    ht = jnp.swapaxes(h, 0, 1).astype(BF16)
    f1 = f1_ref[...]
    zr_parts, zi_parts = [], []
    for j in range(ts2):
        z = jnp.dot(f1, ht[j], preferred_element_type=F32)
        zr, zi = z[:n1], z[n1:]
        tr = jnp.tile(twr_ref[j], (1, d // LANES))
        ti = jnp.tile(twi_ref[j], (1, d // LANES))
        zr_parts.append(zr * tr - zi * ti)
        zi_parts.append(zr * ti + zi * tr)
    zr_ref[...] = jnp.swapaxes(jnp.stack(zr_parts, axis=0), 0, 1).astype(BF16)
    zi_ref[...] = jnp.swapaxes(jnp.stack(zi_parts, axis=0), 0, 1).astype(BF16)


def _fourier_c_kernel(zr_ref, zi_ref, x_ref, f2_ref, cs_ref, w_ref, o_ref, *, n2, tk1, scale):
    d = D_MODEL
    f2 = f2_ref[...]
    xs = []
    for j in range(tk1):
        zin = jnp.concatenate([zr_ref[j], zi_ref[j]], axis=0)
        xs.append(jnp.dot(f2, zin, preferred_element_type=F32))
    xr = jnp.concatenate([x[:n2] for x in xs], axis=0).astype(BF16)
    xi = jnp.concatenate([x[n2:] for x in xs], axis=0).astype(BF16)
    cs = cs_ref[...]
    mixed = []
    for gi in range(N_FGROUPS):
        sl = slice(gi * FGROUP_DIM, (gi + 1) * FGROUP_DIM)
        lhs = jnp.concatenate([xr[:, sl], xi[:, sl]], axis=1)
        mixed.append(jnp.dot(lhs, cs, preferred_element_type=F32) * scale)
    mixed = jnp.concatenate(mixed, axis=1).astype(BF16)
    delta = jnp.dot(mixed, w_ref[...], preferred_element_type=F32)
    o_ref[...] = x_ref[...] + jnp.swapaxes(delta.reshape(tk1, n2, d), 0, 1)


def _fourier_tables(seq_len):
    n1 = 64
    n2 = seq_len // n1
    c1, s1 = _dft_cos_sin(n1)
    f1 = np.concatenate([c1, -s1], axis=0)
    c2, s2 = _dft_cos_sin(n2)
    f2 = np.block([[c2, s2], [-s2, c2]])
    cc, sc = _dft_cos_sin(FGROUP_DIM)
    cs = np.concatenate([cc, sc], axis=0)
    ang = 2.0 * np.pi * (np.arange(n2)[:, None] * np.arange(n1)[None, :]) / seq_len
    twr = np.cos(ang)[:, :, None].astype(np.float32)
    twi = (-np.sin(ang))[:, :, None].astype(np.float32)
    return n1, n2, f1, f2, cs, twr, twi


def _fourier_layer(x, g, w_bf16):
    b, s, d = x.shape
    n1, n2, f1, f2, cs, twr, twi = _fourier_tables(s)
    ts2 = 16
    tk1 = 8
    f1 = jnp.asarray(f1, BF16)
    f2 = jnp.asarray(f2, BF16)
    cs = jnp.asarray(cs, BF16)
    twr = jnp.broadcast_to(jnp.asarray(twr), (n2, n1, LANES))
    twi = jnp.broadcast_to(jnp.asarray(twi), (n2, n1, LANES))
    g2 = g.reshape(1, d)

    z_shape = jax.ShapeDtypeStruct((b, n1, n2, d), BF16)
    zr, zi = pl.pallas_call(
        functools.partial(_fourier_a_kernel, n1=n1, ts2=ts2),
        grid=(b, n2 // ts2),
        in_specs=[
            pl.BlockSpec((None, n1, ts2, d), lambda bi, i: (bi, 0, i, 0)),
            pl.BlockSpec((1, d), lambda bi, i: (0, 0)),
            pl.BlockSpec((2 * n1, n1), lambda bi, i: (0, 0)),
            pl.BlockSpec((ts2, n1, LANES), lambda bi, i: (i, 0, 0)),
            pl.BlockSpec((ts2, n1, LANES), lambda bi, i: (i, 0, 0)),
        ],
        out_specs=[
            pl.BlockSpec((None, n1, ts2, d), lambda bi, i: (bi, 0, i, 0)),
            pl.BlockSpec((None, n1, ts2, d), lambda bi, i: (bi, 0, i, 0)),
        ],
        out_shape=[z_shape, z_shape],
        compiler_params=pltpu.CompilerParams(
            dimension_semantics=("parallel", "parallel"), vmem_limit_bytes=VMEM_LIMIT),
        name="fourier_a",
    )(x.reshape(b, n1, n2, d), g2, f1, twr, twi)

    scale = float(1.0 / np.sqrt(s * FGROUP_DIM))
    out = pl.pallas_call(
        functools.partial(_fourier_c_kernel, n2=n2, tk1=tk1, scale=scale),
        grid=(b, n1 // tk1),
        in_specs=[
            pl.BlockSpec((None, tk1, n2, d), lambda bi, i: (bi, i, 0, 0)),
            pl.BlockSpec((None, tk1, n2, d), lambda bi, i: (bi, i, 0, 0)),
            pl.BlockSpec((None, n2, tk1, d), lambda bi, i: (bi, 0, i, 0)),
            pl.BlockSpec((2 * n2, 2 * n2), lambda bi, i: (0, 0)),
            pl.BlockSpec((2 * FGROUP_DIM, FGROUP_DIM), lambda bi, i: (0, 0)),
            pl.BlockSpec((d, d), lambda bi, i: (0, 0)),
        ],
        out_specs=pl.BlockSpec((None, n2, tk1, d), lambda bi, i: (bi, 0, i, 0)),
        out_shape=jax.ShapeDtypeStruct((b, n2, n1, d), F32),
        compiler_params=pltpu.CompilerParams(
            dimension_semantics=("parallel", "parallel"), vmem_limit_bytes=VMEM_LIMIT),
        name="fourier_c",
    )(zr, zi, x.reshape(b, n2, n1, d), f2, cs, w_bf16)
    return out.reshape(b, s, d)


def _qkv_kernel(x_ref, g_ref, w_ref, qg_ref, kg_ref, cos_ref, sin_ref, q_ref, k_ref, vt_ref):
    h = _rms(x_ref[...], g_ref[...]).astype(BF16)
    qkv = jnp.dot(h, w_ref[...], preferred_element_type=F32)
    cos = cos_ref[...]
    sin = sin_ref[...]

    def norm_rope(y, gain):
        yn = _rms(y, gain)
        return yn * cos + pltpu.roll(yn, HEAD_DIM // 2, axis=1) * sin

    qg = qg_ref[...]
    kg = kg_ref[...]
    for hd in range(N_HEADS):
        sl = slice(hd * HEAD_DIM, (hd + 1) * HEAD_DIM)
        q_ref[:, sl] = (norm_rope(qkv[:, sl], qg) * ATTN_Q_SCALE).astype(BF16)
    for hd in range(N_KV_HEADS):
        sl = slice(hd * HEAD_DIM, (hd + 1) * HEAD_DIM)
        k_ref[:, sl] = norm_rope(qkv[:, Q_DIM + hd * HEAD_DIM:Q_DIM + (hd + 1) * HEAD_DIM], kg).astype(BF16)
    vt_ref[...] = qkv[:, Q_DIM + KV_DIM:].T.astype(BF16)


def _attn_kernel(q_ref, k_ref, vt_ref, o_ref, s0_ref, s1_ref, acc_ref, *, tq, tk):
    n_chunks = vt_ref.shape[0]
    n_items = (q_ref.shape[0] // tq) * n_chunks
    m_cols = GROUP * tq

    def scores(t, dst_ref):
        q_start = pl.multiple_of((t // n_chunks) * tq, tq)
        k_start = pl.multiple_of((t % n_chunks) * tk, tk)
        qb = q_ref[pl.ds(q_start, tq), :]
        q = jnp.concatenate([qb[:, gi * HEAD_DIM:(gi + 1) * HEAD_DIM] for gi in range(GROUP)], axis=0)
        kc = k_ref[pl.ds(k_start, tk), :]
        st = lax.dot_general(kc, q, (((1,), (1,)), ((), ())), preferred_element_type=F32)
        dst_ref[...] = st
        return jnp.max(st, axis=0, keepdims=True)

    def consume(t, src_ref, m, alpha, l):
        p = jnp.exp2(src_ref[...] - m)
        l = alpha * l + jnp.sum(p, axis=0, keepdims=True)
        pv = jnp.dot(vt_ref[t % n_chunks], p.astype(BF16), preferred_element_type=F32)
        acc_ref[...] = alpha * acc_ref[...] + pv
        return l

    def finalize(t, l):
        o_t = acc_ref[...] / l
        q_start = pl.multiple_of((t // n_chunks) * tq, tq)
        for gi in range(GROUP):
            o_ref[pl.ds(q_start, tq), gi * HEAD_DIM:(gi + 1) * HEAD_DIM] = (
                o_t[:, gi * tq:(gi + 1) * tq].T.astype(BF16))

    def step(t, cur_ref, nxt_ref, carry, maybe_last):
        m, alpha, l = carry
        cmax = scores(t + 1, nxt_ref)
        l = consume(t, cur_ref, m, alpha, l)
        if not maybe_last:
            m_next = jnp.maximum(m, cmax)
            return m_next, jnp.exp2(m - m_next), l
        last = (t % n_chunks) == n_chunks - 1
        pl.when(last)(lambda: finalize(t, l))
        m_next = jnp.where(last, cmax, jnp.maximum(m, cmax))
        alpha_next = jnp.where(last, 0.0, jnp.exp2(m - m_next))
        return m_next, alpha_next, l

    def body(tt, carry):
        carry = step(2 * tt, s0_ref, s1_ref, carry, False)
        return step(2 * tt + 1, s1_ref, s0_ref, carry, True)

    acc_ref[...] = jnp.zeros_like(acc_ref)
    m0 = scores(0, s0_ref)
    zero = jnp.zeros((1, m_cols), F32)
    carry = lax.fori_loop(0, n_items // 2 - 1, body, (m0, zero, zero))
    m, alpha, l = step(n_items - 2, s0_ref, s1_ref, carry, False)
    l = consume(n_items - 1, s1_ref, m, alpha, l)
    finalize(n_items - 1, l)


def _attn_bounded_kernel(q_ref, k_ref, vt_ref, o_ref, p0_ref, p1_ref, acc_ref, *, tq, tk):
    n_chunks = vt_ref.shape[0]
    n_items = (q_ref.shape[0] // tq) * n_chunks
    m_cols = GROUP * tq

    def probs(t, dst_ref):
        q_start = pl.multiple_of((t // n_chunks) * tq, tq)
        k_start = pl.multiple_of((t % n_chunks) * tk, tk)
        qb = q_ref[pl.ds(q_start, tq), :]
        q = jnp.concatenate([qb[:, gi * HEAD_DIM:(gi + 1) * HEAD_DIM] for gi in range(GROUP)], axis=0)
        kc = k_ref[pl.ds(k_start, tk), :]
        st = lax.dot_general(kc, q, (((1,), (1,)), ((), ())), preferred_element_type=F32)
        p = jnp.exp2(st)
        dst_ref[...] = p.astype(BF16)
        return jnp.sum(p, axis=0, keepdims=True)

    def accumulate(t, src_ref):
        acc_ref[...] += jnp.dot(vt_ref[t % n_chunks], src_ref[...], preferred_element_type=F32)

    def finalize(t, l):
        o_t = acc_ref[...] / l
        q_start = pl.multiple_of((t // n_chunks) * tq, tq)
        for gi in range(GROUP):
            o_ref[pl.ds(q_start, tq), gi * HEAD_DIM:(gi + 1) * HEAD_DIM] = (
                o_t[:, gi * tq:(gi + 1) * tq].T.astype(BF16))
        acc_ref[...] = jnp.zeros_like(acc_ref)

    def step(t, cur_ref, nxt_ref, carry, maybe_last):
        l, lsum = carry
        lsum_next = probs(t + 1, nxt_ref)
        accumulate(t, cur_ref)
        l = l + lsum
        if not maybe_last:
            return l, lsum_next
        last = (t % n_chunks) == n_chunks - 1
        pl.when(last)(lambda: finalize(t, l))
        return jnp.where(last, 0.0, l), lsum_next

    def body(tt, carry):
        carry = step(2 * tt, p0_ref, p1_ref, carry, False)
        return step(2 * tt + 1, p1_ref, p0_ref, carry, True)

    acc_ref[...] = jnp.zeros_like(acc_ref)
    lsum0 = probs(0, p0_ref)
    carry = lax.fori_loop(0, n_items // 2 - 1, body, (jnp.zeros((1, m_cols), F32), lsum0))
    l, lsum = step(n_items - 2, p0_ref, p1_ref, carry, False)
    accumulate(n_items - 1, p1_ref)
    finalize(n_items - 1, l + lsum)


def _rope_tables(seq_len):
    rows = seq_len // GRID_W
    row = jnp.repeat(jnp.arange(rows, dtype=F32), GRID_W)
    col = jnp.tile(jnp.arange(GRID_W, dtype=F32), rows)
    inv = ROPE_THETA ** (-jnp.arange(0, AXIS_ROT_DIM, 2, dtype=F32) / AXIS_ROT_DIM)
    ang = jnp.concatenate([row[:, None] * inv, col[:, None] * inv], axis=-1)
    cos, sin = jnp.cos(ang), jnp.sin(ang)
    return jnp.concatenate([cos, cos], axis=-1), jnp.concatenate([-sin, sin], axis=-1)


def _attention_qkv(x, g, w_qkv_bf16, q_gain, k_gain):
    b, s, d = x.shape
    t = b * s
    tq = 256
    tk = 1024
    tm = 512
    per_chunk = tk // tm
    cos, sin = _rope_tables(s)
    nblk = s // tm
    q, k, vt = pl.pallas_call(
        _qkv_kernel,
        grid=(t // tm,),
        in_specs=[
            pl.BlockSpec((tm, d), lambda i: (i, 0)),
            pl.BlockSpec((1, d), lambda i: (0, 0)),
            pl.BlockSpec((d, QKV_DIM), lambda i: (0, 0)),
            pl.BlockSpec((1, HEAD_DIM), lambda i: (0, 0)),
            pl.BlockSpec((1, HEAD_DIM), lambda i: (0, 0)),
            pl.BlockSpec((tm, HEAD_DIM), lambda i: (i % nblk, 0)),
            pl.BlockSpec((tm, HEAD_DIM), lambda i: (i % nblk, 0)),
        ],
        out_specs=[
            pl.BlockSpec((tm, Q_DIM), lambda i: (i, 0)),
            pl.BlockSpec((tm, KV_DIM), lambda i: (i, 0)),
            pl.BlockSpec((None, KV_DIM, tm), lambda i: (i // per_chunk, 0, i % per_chunk)),
        ],
        out_shape=[
            jax.ShapeDtypeStruct((t, Q_DIM), BF16),
            jax.ShapeDtypeStruct((t, KV_DIM), BF16),
            jax.ShapeDtypeStruct((t // tk, KV_DIM, tk), BF16),
        ],
        compiler_params=pltpu.CompilerParams(
            dimension_semantics=("parallel",), vmem_limit_bytes=VMEM_LIMIT),
        name="qkv_proj",
    )(x.reshape(t, d), g.reshape(1, d), w_qkv_bf16, q_gain.reshape(1, HEAD_DIM),
      k_gain.reshape(1, HEAD_DIM), cos, sin)

    def attention_call(body, buf_dtype, name):
        return pl.pallas_call(
            functools.partial(body, tq=tq, tk=tk),
            grid=(b, N_KV_HEADS),
            in_specs=[
                pl.BlockSpec((None, s, GROUP * HEAD_DIM), lambda bi, hi: (bi, 0, hi),
                             pipeline_mode=pl.Buffered(1)),
                pl.BlockSpec((None, s, HEAD_DIM), lambda bi, hi: (bi, 0, hi)),
                pl.BlockSpec((None, s // tk, HEAD_DIM, tk), lambda bi, hi: (bi, 0, hi, 0)),
            ],
            out_specs=pl.BlockSpec((None, s, GROUP * HEAD_DIM), lambda bi, hi: (bi, 0, hi)),
            out_shape=jax.ShapeDtypeStruct((b, s, Q_DIM), BF16),
            scratch_shapes=[
                pltpu.VMEM((tk, GROUP * tq), buf_dtype),
                pltpu.VMEM((tk, GROUP * tq), buf_dtype),
                pltpu.VMEM((HEAD_DIM, GROUP * tq), F32),
            ],
            compiler_params=pltpu.CompilerParams(
                dimension_semantics=("parallel", "parallel"), vmem_limit_bytes=VMEM_LIMIT),
            name=name,
        )

    score_bound = (HEAD_DIM * ATTN_Q_SCALE) * jnp.max(jnp.abs(q_gain)) * jnp.max(jnp.abs(k_gain))
    o = lax.cond(
        score_bound <= ATTN_BOUNDED_MAX_LOG2_SCORE,
        attention_call(_attn_bounded_kernel, BF16, "attention_bounded"),
        attention_call(_attn_kernel, F32, "attention"),
        q.reshape(b, s, Q_DIM), k.reshape(b, s, KV_DIM), vt.reshape(b, s // tk, KV_DIM, tk))
    return o.reshape(t, Q_DIM)


def _ffn_kernel(*refs, with_attn):
    if with_attn:
        x_ref, a_ref, wo_ref, g_ref, wg_ref, wu_ref, wd_ref, o_ref = refs
        x = x_ref[...] + jnp.dot(a_ref[...], wo_ref[...], preferred_element_type=F32)
    else:
        x_ref, g_ref, wg_ref, wu_ref, wd_ref, o_ref = refs
        x = x_ref[...]
    h = _rms(x, g_ref[...]).astype(BF16)
    gate = jnp.dot(h, wg_ref[...], preferred_element_type=F32)
    up = jnp.dot(h, wu_ref[...], preferred_element_type=F32)
    act = (gate * jax.nn.sigmoid(gate) * up).astype(BF16)
    o_ref[...] = x + jnp.dot(act, wd_ref[...], preferred_element_type=F32)


def _ffn_layer(x2d, g, wg, wu, wd, attn=None):
    t, d = x2d.shape
    dff = wg.shape[1]
    tm = 512
    resident = functools.partial(pl.BlockSpec, index_map=lambda i: (0, 0), pipeline_mode=pl.Buffered(1))
    row = lambda width: pl.BlockSpec((tm, width), lambda i: (i, 0))
    in_specs = [row(d)]
    args = [x2d]
    if attn is not None:
        a2d, wo = attn
        in_specs += [row(Q_DIM), resident((Q_DIM, d))]
        args += [a2d, wo]
    in_specs += [resident((1, d)), resident((d, dff)), resident((d, dff)), resident((dff, d))]
    args += [g.reshape(1, d), wg, wu, wd]
    return pl.pallas_call(
        functools.partial(_ffn_kernel, with_attn=attn is not None),
        grid=(t // tm,),
        in_specs=in_specs,
        out_specs=row(d),
        out_shape=jax.ShapeDtypeStruct((t, d), F32),
        compiler_params=pltpu.CompilerParams(
            dimension_semantics=("parallel",), vmem_limit_bytes=VMEM_LIMIT),
        name="swiglu",
    )(*args)


def _pair_split_perm():
    p = np.concatenate([np.arange(0, HEAD_DIM, 2), np.arange(1, HEAD_DIM, 2)])
    cols = [h * HEAD_DIM + p for h in range(N_HEADS + N_KV_HEADS)]
    cols.append(np.arange(Q_DIM + KV_DIM, QKV_DIM))
    return p, np.concatenate(cols)


def kernel(x_prompt, x_sample, norm_mix, norm_ffn, fourier_w, attn_w_qkv, attn_q_norm, attn_k_norm,
           attn_w_o, ffn_w_gate, ffn_w_up, ffn_w_down):
    depth = norm_mix.shape[0]
    head_perm, col_perm = _pair_split_perm()
    fw = fourier_w.astype(BF16)
    wqkv = attn_w_qkv[:, :, col_perm].astype(BF16)
    qn = attn_q_norm[:, head_perm]
    kn = attn_k_norm[:, head_perm]
    wo = attn_w_o.astype(BF16)
    wg = ffn_w_gate.astype(BF16)
    wu = ffn_w_up.astype(BF16)
    wd = ffn_w_down.astype(BF16)

    def run_trunk(x):
        b, s, d = x.shape
        for i in range(depth):
            j = i // 2
            if i % 2 == 0:
                x = _fourier_layer(x, norm_mix[i], fw[j])
                x2d = _ffn_layer(x.reshape(b * s, d), norm_ffn[i], wg[i], wu[i], wd[i])
            else:
                a2d = _attention_qkv(x, norm_mix[i], wqkv[j], qn[j], kn[j])
                x2d = _ffn_layer(x.reshape(b * s, d), norm_ffn[i], wg[i], wu[i], wd[i], attn=(a2d, wo[j]))
            x = x2d.reshape(b, s, d)
        return x

    return (run_trunk(x_prompt), run_trunk(x_sample))
```

```python
import functools

import numpy as np
import jax
import jax.numpy as jnp
from jax import lax
from jax.experimental import pallas as pl
from jax.experimental.pallas import tpu as pltpu

D_MODEL = 1024
HEAD_DIM = 128
N_HEADS = D_MODEL // HEAD_DIM
N_KV_HEADS = 2
GROUP = N_HEADS // N_KV_HEADS
Q_DIM = N_HEADS * HEAD_DIM
KV_DIM = N_KV_HEADS * HEAD_DIM
QKV_DIM = Q_DIM + 2 * KV_DIM
GRID_W = 64
AXIS_ROT_DIM = HEAD_DIM // 2
ROPE_THETA = 10000.0
N_FGROUPS = 8
FGROUP_DIM = D_MODEL // N_FGROUPS
EPS = 1e-6

ATTN_Q_SCALE = float(HEAD_DIM ** -0.5 * np.log2(np.e))
ATTN_BOUNDED_MAX_LOG2_SCORE = 60.0

LANES = 128
VMEM_LIMIT = 56 * 1024 * 1024

F32 = jnp.float32
BF16 = jnp.bfloat16


def _rms(x, g):
    ms = jnp.mean(x * x, axis=-1, keepdims=True)
    return x * lax.rsqrt(ms + EPS) * g


def _layer_spec(stacked, layer, single_buffer=False):
    zeros = (0,) * (stacked.ndim - 1)
    mode = dict(pipeline_mode=pl.Buffered(1)) if single_buffer else {}
    return pl.BlockSpec((None,) + stacked.shape[1:], lambda *_: (layer,) + zeros, **mode)


def _dft_cos_sin(n):
    k = np.arange(n)
    ang = 2.0 * np.pi * ((k[:, None] * k[None, :]) % n) / n
    return np.cos(ang), np.sin(ang)


def _fourier_a_kernel(x_ref, g_ref, f1_ref, zr_ref, zi_ref, *, n1, ts2):
    h = _rms(x_ref[...], g_ref[...])
    ht = jnp.swapaxes(h.astype(BF16), 0, 1)
    z = [jnp.dot(f1_ref[j], ht[j], preferred_element_type=F32) for j in range(ts2)]
    zr_ref[...] = jnp.swapaxes(jnp.stack([zj[:n1] for zj in z], axis=0).astype(BF16), 0, 1)
    zi_ref[...] = jnp.swapaxes(jnp.stack([zj[n1:] for zj in z], axis=0).astype(BF16), 0, 1)


def _fourier_c_kernel(zr_ref, zi_ref, x_ref, f2_ref, cs_ref, w_ref, o_ref, *, n2, tk1, scale):
    d = D_MODEL
    f2 = f2_ref[...]
    xs = []
    for j in range(tk1):
        zin = jnp.concatenate([zr_ref[j], zi_ref[j]], axis=0)
        xs.append(jnp.dot(f2, zin, preferred_element_type=F32))
    xr = jnp.concatenate([x[:n2] for x in xs], axis=0).astype(BF16)
    xi = jnp.concatenate([x[n2:] for x in xs], axis=0).astype(BF16)
    cs = cs_ref[...]
    mixed = []
    for gi in range(N_FGROUPS):
        sl = slice(gi * FGROUP_DIM, (gi + 1) * FGROUP_DIM)
        lhs = jnp.concatenate([xr[:, sl], xi[:, sl]], axis=1)
        mixed.append(jnp.dot(lhs, cs, preferred_element_type=F32) * scale)
    mixed = jnp.concatenate(mixed, axis=1).astype(BF16)
    delta = jnp.dot(mixed, w_ref[...], preferred_element_type=F32)
    o_ref[...] = x_ref[...] + jnp.swapaxes(delta.reshape(tk1, n2, d), 0, 1)


def _fourier_tables(seq_len):
    n1 = 64
    n2 = seq_len // n1
    s2_, k1_, s1_ = np.arange(n2)[:, None, None], np.arange(n1)[None, :, None], np.arange(n1)[None, None, :]
    ang = 2.0 * np.pi * ((k1_ * s1_ * n2 + k1_ * s2_) % seq_len) / seq_len
    f1 = np.concatenate([np.cos(ang), -np.sin(ang)], axis=1)
    c2, s2 = _dft_cos_sin(n2)
    f2 = np.block([[c2, s2], [-s2, c2]])
    cc, sc = _dft_cos_sin(FGROUP_DIM)
    cs = np.concatenate([cc, sc], axis=0)
    return n1, n2, f1, f2, cs


def _fourier_layer(x, g, w_bf16, layer):
    b, s, d = x.shape
    n1, n2, f1, f2, cs = _fourier_tables(s)
    ts2 = 16
    tk1 = 8
    f1 = jnp.asarray(f1, BF16)
    f2 = jnp.asarray(f2, BF16)
    cs = jnp.asarray(cs, BF16)
    g2 = g.reshape(1, d)

    z_shape = jax.ShapeDtypeStruct((b, n1, n2, d), BF16)
    zr, zi = pl.pallas_call(
        functools.partial(_fourier_a_kernel, n1=n1, ts2=ts2),
        grid=(b, n2 // ts2),
        in_specs=[
            pl.BlockSpec((None, n1, ts2, d), lambda bi, i: (bi, 0, i, 0)),
            pl.BlockSpec((1, d), lambda bi, i: (0, 0)),
            pl.BlockSpec((ts2, 2 * n1, n1), lambda bi, i: (i, 0, 0)),
        ],
        out_specs=[
            pl.BlockSpec((None, n1, ts2, d), lambda bi, i: (bi, 0, i, 0)),
            pl.BlockSpec((None, n1, ts2, d), lambda bi, i: (bi, 0, i, 0)),
        ],
        out_shape=[z_shape, z_shape],
        compiler_params=pltpu.CompilerParams(
            dimension_semantics=("parallel", "parallel"), vmem_limit_bytes=VMEM_LIMIT),
        name="fourier_a",
    )(x.reshape(b, n1, n2, d), g2, f1)

    scale = float(1.0 / np.sqrt(s * FGROUP_DIM))
    out = pl.pallas_call(
        functools.partial(_fourier_c_kernel, n2=n2, tk1=tk1, scale=scale),
        grid=(b, n1 // tk1),
        in_specs=[
            pl.BlockSpec((None, tk1, n2, d), lambda bi, i: (bi, i, 0, 0)),
            pl.BlockSpec((None, tk1, n2, d), lambda bi, i: (bi, i, 0, 0)),
            pl.BlockSpec((None, n2, tk1, d), lambda bi, i: (bi, 0, i, 0)),
            pl.BlockSpec((2 * n2, 2 * n2), lambda bi, i: (0, 0)),
            pl.BlockSpec((2 * FGROUP_DIM, FGROUP_DIM), lambda bi, i: (0, 0)),
            _layer_spec(w_bf16, layer),
        ],
        out_specs=pl.BlockSpec((None, n2, tk1, d), lambda bi, i: (bi, 0, i, 0)),
        out_shape=jax.ShapeDtypeStruct((b, n2, n1, d), F32),
        compiler_params=pltpu.CompilerParams(
            dimension_semantics=("parallel", "parallel"), vmem_limit_bytes=VMEM_LIMIT),
        name="fourier_c",
    )(zr, zi, x.reshape(b, n2, n1, d), f2, cs, w_bf16)
    return out.reshape(b, s, d)


def _qkv_kernel(x_ref, g_ref, w_ref, qg_ref, kg_ref, cos_ref, sin_ref, q_ref, k_ref, vt_ref):
    h = _rms(x_ref[...], g_ref[...]).astype(BF16)
    y = jnp.dot(h, w_ref[...], preferred_element_type=F32)
    cos = cos_ref[...]
    sin = sin_ref[...]
    ones = jnp.ones((2 * HEAD_DIM, HEAD_DIM), BF16)

    def norm_rope(y_own, y_partner, a_own, a_partner):
        sq = y_own * y_own
        hi = sq.astype(BF16)
        lo = (sq - hi.astype(F32)).astype(BF16)
        ss = jnp.dot(jnp.concatenate([hi, lo], axis=1), ones, preferred_element_type=F32)
        r = lax.rsqrt(ss * (1.0 / HEAD_DIM) + EPS)
        return (y_own * a_own + y_partner * a_partner) * r

    qg = qg_ref[...] * ATTN_Q_SCALE
    kg = kg_ref[...]
    aq_own, aq_partner = cos * qg[0:1], sin * qg[1:2]
    ak_own, ak_partner = cos * kg[0:1], sin * kg[1:2]
    k_off = 2 * Q_DIM
    for hd in range(N_HEADS):
        sl = slice(hd * HEAD_DIM, (hd + 1) * HEAD_DIM)
        sp = slice(Q_DIM + hd * HEAD_DIM, Q_DIM + (hd + 1) * HEAD_DIM)
        q_ref[:, sl] = norm_rope(y[:, sl], y[:, sp], aq_own, aq_partner).astype(BF16)
    for hd in range(N_KV_HEADS):
        sl = slice(hd * HEAD_DIM, (hd + 1) * HEAD_DIM)
        so = slice(k_off + hd * HEAD_DIM, k_off + (hd + 1) * HEAD_DIM)
        sp = slice(k_off + KV_DIM + hd * HEAD_DIM, k_off + KV_DIM + (hd + 1) * HEAD_DIM)
        k_ref[:, sl] = norm_rope(y[:, so], y[:, sp], ak_own, ak_partner).astype(BF16)
    vt_ref[...] = y[:, k_off + 2 * KV_DIM:].T.astype(BF16)


def _attn_kernel(q_ref, k_ref, vt_ref, o_ref, s0_ref, s1_ref, acc_ref, *, tq, tk):
    n_chunks = vt_ref.shape[0]
    n_items = (q_ref.shape[0] // tq) * n_chunks
    m_cols = GROUP * tq

    def scores(t, dst_ref):
        q_start = pl.multiple_of((t // n_chunks) * tq, tq)
        k_start = pl.multiple_of((t % n_chunks) * tk, tk)
        qb = q_ref[pl.ds(q_start, tq), :]
        q = jnp.concatenate([qb[:, gi * HEAD_DIM:(gi + 1) * HEAD_DIM] for gi in range(GROUP)], axis=0)
        kc = k_ref[pl.ds(k_start, tk), :]
        st = lax.dot_general(kc, q, (((1,), (1,)), ((), ())), preferred_element_type=F32)
        dst_ref[...] = st
        return jnp.max(st, axis=0, keepdims=True)

    def consume(t, src_ref, m, alpha, l):
        p = jnp.exp2(src_ref[...] - m)
        l = alpha * l + jnp.sum(p, axis=0, keepdims=True)
        pv = jnp.dot(vt_ref[t % n_chunks], p.astype(BF16), preferred_element_type=F32)
        acc_ref[...] = alpha * acc_ref[...] + pv
        return l

    def finalize(t, l):
        o_t = acc_ref[...] / l
        q_start = pl.multiple_of((t // n_chunks) * tq, tq)
        for gi in range(GROUP):
            o_ref[pl.ds(q_start, tq), gi * HEAD_DIM:(gi + 1) * HEAD_DIM] = (
                o_t[:, gi * tq:(gi + 1) * tq].T.astype(BF16))

    def step(t, cur_ref, nxt_ref, carry, maybe_last):
        m, alpha, l = carry
        cmax = scores(t + 1, nxt_ref)
        l = consume(t, cur_ref, m, alpha, l)
        if not maybe_last:
            m_next = jnp.maximum(m, cmax)
            return m_next, jnp.exp2(m - m_next), l
        last = (t % n_chunks) == n_chunks - 1
        pl.when(last)(lambda: finalize(t, l))
        m_next = jnp.where(last, cmax, jnp.maximum(m, cmax))
        alpha_next = jnp.where(last, 0.0, jnp.exp2(m - m_next))
        return m_next, alpha_next, l

    def body(tt, carry):
        carry = step(2 * tt, s0_ref, s1_ref, carry, False)
        return step(2 * tt + 1, s1_ref, s0_ref, carry, True)

    acc_ref[...] = jnp.zeros_like(acc_ref)
    m0 = scores(0, s0_ref)
    zero = jnp.zeros((1, m_cols), F32)
    carry = lax.fori_loop(0, n_items // 2 - 1, body, (m0, zero, zero))
    m, alpha, l = step(n_items - 2, s0_ref, s1_ref, carry, False)
    l = consume(n_items - 1, s1_ref, m, alpha, l)
    finalize(n_items - 1, l)


def _attn_bounded_kernel(q_ref, k_ref, vt_ref, o_ref, p0_ref, p1_ref, acc_ref, *, tq, tk):
    n_chunks = vt_ref.shape[0]
    n_items = (q_ref.shape[0] // tq) * n_chunks
    m_cols = GROUP * tq

    def probs(t, dst_ref):
        q_start = pl.multiple_of((t // n_chunks) * tq, tq)
        k_start = pl.multiple_of((t % n_chunks) * tk, tk)
        qb = q_ref[pl.ds(q_start, tq), :]
        q = jnp.concatenate([qb[:, gi * HEAD_DIM:(gi + 1) * HEAD_DIM] for gi in range(GROUP)], axis=0)
        kc = k_ref[pl.ds(k_start, tk), :]
        st = lax.dot_general(kc, q, (((1,), (1,)), ((), ())), preferred_element_type=F32)
        p = jnp.exp2(st)
        dst_ref[...] = p.astype(BF16)
        return jnp.sum(p, axis=0, keepdims=True)

    def accumulate(t, src_ref):
        acc_ref[...] += jnp.dot(vt_ref[t % n_chunks], src_ref[...], preferred_element_type=F32)

    def finalize(t, l):
        o_t = acc_ref[...] / l
        q_start = pl.multiple_of((t // n_chunks) * tq, tq)
        for gi in range(GROUP):
            o_ref[pl.ds(q_start, tq), gi * HEAD_DIM:(gi + 1) * HEAD_DIM] = (
                o_t[:, gi * tq:(gi + 1) * tq].T.astype(BF16))
        acc_ref[...] = jnp.zeros_like(acc_ref)

    def step(t, cur_ref, nxt_ref, carry, maybe_last):
        l, lsum = carry
        lsum_next = probs(t + 1, nxt_ref)
        accumulate(t, cur_ref)
        l = l + lsum
        if not maybe_last:
            return l, lsum_next
        last = (t % n_chunks) == n_chunks - 1
        pl.when(last)(lambda: finalize(t, l))
        return jnp.where(last, 0.0, l), lsum_next

    def body(tt, carry):
        carry = step(2 * tt, p0_ref, p1_ref, carry, False)
        return step(2 * tt + 1, p1_ref, p0_ref, carry, True)

    acc_ref[...] = jnp.zeros_like(acc_ref)
    lsum0 = probs(0, p0_ref)
    carry = lax.fori_loop(0, n_items // 2 - 1, body, (jnp.zeros((1, m_cols), F32), lsum0))
    l, lsum = step(n_items - 2, p0_ref, p1_ref, carry, False)
    accumulate(n_items - 1, p1_ref)
    finalize(n_items - 1, l + lsum)


def _rope_tables(seq_len):
    rows = seq_len // GRID_W
    row = jnp.repeat(jnp.arange(rows, dtype=F32), GRID_W)
    col = jnp.tile(jnp.arange(GRID_W, dtype=F32), rows)
    inv = ROPE_THETA ** (-jnp.arange(0, AXIS_ROT_DIM, 2, dtype=F32) / AXIS_ROT_DIM)
    ang = jnp.concatenate([row[:, None] * inv, col[:, None] * inv], axis=-1)
    cos, sin = jnp.cos(ang), jnp.sin(ang)
    return jnp.concatenate([cos, cos], axis=-1), jnp.concatenate([-sin, sin], axis=-1)


def _attention_qkv(x, g, w_qkv_bf16, layer, q_gain, k_gain):
    b, s, d = x.shape
    t = b * s
    tq = 256
    tk = 1024
    tm = 512
    per_chunk = tk // tm
    cos, sin = _rope_tables(s)
    nblk = s // tm
    q, k, vt = pl.pallas_call(
        _qkv_kernel,
        grid=(t // tm,),
        in_specs=[
            pl.BlockSpec((tm, d), lambda i: (i, 0)),
            pl.BlockSpec((1, d), lambda i: (0, 0)),
            _layer_spec(w_qkv_bf16, layer),
            pl.BlockSpec((2, HEAD_DIM), lambda i: (0, 0)),
            pl.BlockSpec((2, HEAD_DIM), lambda i: (0, 0)),
            pl.BlockSpec((tm, HEAD_DIM), lambda i: (i % nblk, 0)),
            pl.BlockSpec((tm, HEAD_DIM), lambda i: (i % nblk, 0)),
        ],
        out_specs=[
            pl.BlockSpec((tm, Q_DIM), lambda i: (i, 0)),
            pl.BlockSpec((tm, KV_DIM), lambda i: (i, 0)),
            pl.BlockSpec((None, KV_DIM, tm), lambda i: (i // per_chunk, 0, i % per_chunk)),
        ],
        out_shape=[
            jax.ShapeDtypeStruct((t, Q_DIM), BF16),
            jax.ShapeDtypeStruct((t, KV_DIM), BF16),
            jax.ShapeDtypeStruct((t // tk, KV_DIM, tk), BF16),
        ],
        compiler_params=pltpu.CompilerParams(
            dimension_semantics=("parallel",), vmem_limit_bytes=VMEM_LIMIT),
        name="qkv_proj",
    )(x.reshape(t, d), g.reshape(1, d), w_qkv_bf16, q_gain, k_gain, cos, sin)

    def attention_call(body, buf_dtype, name):
        return pl.pallas_call(
            functools.partial(body, tq=tq, tk=tk),
            grid=(b, N_KV_HEADS),
            in_specs=[
                pl.BlockSpec((None, s, GROUP * HEAD_DIM), lambda bi, hi: (bi, 0, hi),
                             pipeline_mode=pl.Buffered(1)),
                pl.BlockSpec((None, s, HEAD_DIM), lambda bi, hi: (bi, 0, hi)),
                pl.BlockSpec((None, s // tk, HEAD_DIM, tk), lambda bi, hi: (bi, 0, hi, 0)),
            ],
            out_specs=pl.BlockSpec((None, s, GROUP * HEAD_DIM), lambda bi, hi: (bi, 0, hi)),
            out_shape=jax.ShapeDtypeStruct((b, s, Q_DIM), BF16),
            scratch_shapes=[
                pltpu.VMEM((tk, GROUP * tq), buf_dtype),
                pltpu.VMEM((tk, GROUP * tq), buf_dtype),
                pltpu.VMEM((HEAD_DIM, GROUP * tq), F32),
            ],
            compiler_params=pltpu.CompilerParams(
                dimension_semantics=("parallel", "parallel"), vmem_limit_bytes=VMEM_LIMIT),
            name=name,
        )

    score_bound = (HEAD_DIM * ATTN_Q_SCALE) * jnp.max(jnp.abs(q_gain)) * jnp.max(jnp.abs(k_gain))
    o = lax.cond(
        score_bound <= ATTN_BOUNDED_MAX_LOG2_SCORE,
        attention_call(_attn_bounded_kernel, BF16, "attention_bounded"),
        attention_call(_attn_kernel, F32, "attention"),
        q.reshape(b, s, Q_DIM), k.reshape(b, s, KV_DIM), vt.reshape(b, s // tk, KV_DIM, tk))
    return o.reshape(t, Q_DIM)


def _ffn_kernel(*refs, with_attn):
    if with_attn:
        x_ref, a_ref, wo_ref, g_ref, wg_ref, wu_ref, wd_ref, o_ref = refs
        x = x_ref[...] + jnp.dot(a_ref[...], wo_ref[...], preferred_element_type=F32)
    else:
        x_ref, g_ref, wg_ref, wu_ref, wd_ref, o_ref = refs
        x = x_ref[...]
    h = _rms(x, g_ref[...]).astype(BF16)
    gate = jnp.dot(h, wg_ref[...], preferred_element_type=F32)
    up = jnp.dot(h, wu_ref[...], preferred_element_type=F32)
    act = (gate * jax.nn.sigmoid(gate) * up).astype(BF16)
    o_ref[...] = x + jnp.dot(act, wd_ref[...], preferred_element_type=F32)


def _ffn_layer(x2d, g, layer, wg, wu, wd, attn=None):
    t, d = x2d.shape
    dff = wg.shape[2]
    tm = 512
    resident = functools.partial(pl.BlockSpec, index_map=lambda i: (0, 0), pipeline_mode=pl.Buffered(1))
    row = lambda width: pl.BlockSpec((tm, width), lambda i: (i, 0))
    in_specs = [row(d)]
    args = [x2d]
    if attn is not None:
        a2d, wo, attn_layer = attn
        in_specs += [row(Q_DIM), _layer_spec(wo, attn_layer, single_buffer=True)]
        args += [a2d, wo]
    in_specs += [resident((1, d))] + [_layer_spec(w, layer, single_buffer=True) for w in (wg, wu, wd)]
    args += [g.reshape(1, d), wg, wu, wd]
    return pl.pallas_call(
        functools.partial(_ffn_kernel, with_attn=attn is not None),
        grid=(t // tm,),
        in_specs=in_specs,
        out_specs=row(d),
        out_shape=jax.ShapeDtypeStruct((t, d), F32),
        compiler_params=pltpu.CompilerParams(
            dimension_semantics=("parallel",), vmem_limit_bytes=VMEM_LIMIT),
        name="swiglu",
    )(*args)


def _pair_split_perm():
    own = np.concatenate([np.arange(0, HEAD_DIM, 2), np.arange(1, HEAD_DIM, 2)])
    partner = np.roll(own, HEAD_DIM // 2)
    q_heads = range(N_HEADS)
    k_heads = range(N_HEADS, N_HEADS + N_KV_HEADS)
    cols = ([h * HEAD_DIM + own for h in q_heads] + [h * HEAD_DIM + partner for h in q_heads]
            + [h * HEAD_DIM + own for h in k_heads] + [h * HEAD_DIM + partner for h in k_heads]
            + [np.arange(Q_DIM + KV_DIM, QKV_DIM)])
    return np.stack([own, partner]), np.concatenate(cols)


def kernel(x_prompt, x_sample, norm_mix, norm_ffn, fourier_w, attn_w_qkv, attn_q_norm, attn_k_norm,
           attn_w_o, ffn_w_gate, ffn_w_up, ffn_w_down):
    depth = norm_mix.shape[0]
    head_perm, col_perm = _pair_split_perm()
    fw = fourier_w.astype(BF16)
    wqkv = attn_w_qkv[:, :, col_perm].astype(BF16)
    qn = attn_q_norm[:, head_perm]
    kn = attn_k_norm[:, head_perm]
    wo = attn_w_o.astype(BF16)
    wg = ffn_w_gate.astype(BF16)
    wu = ffn_w_up.astype(BF16)
    wd = ffn_w_down.astype(BF16)

    def run_trunk(x):
        b, s, d = x.shape
        for i in range(depth):
            j = i // 2
            if i % 2 == 0:
                x = _fourier_layer(x, norm_mix[i], fw, j)
                x2d = _ffn_layer(x.reshape(b * s, d), norm_ffn[i], i, wg, wu, wd)
            else:
                a2d = _attention_qkv(x, norm_mix[i], wqkv, j, qn[j], kn[j])
                x2d = _ffn_layer(x.reshape(b * s, d), norm_ffn[i], i, wg, wu, wd, attn=(a2d, wo, j))
            x = x2d.reshape(b, s, d)
        return x

    return (run_trunk(x_prompt), run_trunk(x_sample))
```

```python
import functools

import numpy as np
import jax
import jax.numpy as jnp
from jax import lax
from jax.experimental import pallas as pl
from jax.experimental.pallas import tpu as pltpu

D_MODEL = 1024
HEAD_DIM = 128
N_HEADS = D_MODEL // HEAD_DIM
N_KV_HEADS = 2
GROUP = N_HEADS // N_KV_HEADS
Q_DIM = N_HEADS * HEAD_DIM
KV_DIM = N_KV_HEADS * HEAD_DIM
QKV_DIM = Q_DIM + 2 * KV_DIM
GRID_W = 64
AXIS_ROT_DIM = HEAD_DIM // 2
ROPE_THETA = 10000.0
N_FGROUPS = 8
FGROUP_DIM = D_MODEL // N_FGROUPS
EPS = 1e-6

ATTN_Q_SCALE = float(HEAD_DIM ** -0.5 * np.log2(np.e))
ATTN_BOUNDED_MAX_LOG2_SCORE = 60.0

LANES = 128
VMEM_LIMIT = 56 * 1024 * 1024

F32 = jnp.float32
BF16 = jnp.bfloat16


def _rms(x, g):
    ms = jnp.mean(x * x, axis=-1, keepdims=True)
    return x * lax.rsqrt(ms + EPS) * g


def _layer_spec(stacked, layer, single_buffer=False):
    zeros = (0,) * (stacked.ndim - 1)
    mode = dict(pipeline_mode=pl.Buffered(1)) if single_buffer else {}
    return pl.BlockSpec((None,) + stacked.shape[1:], lambda *_: (layer,) + zeros, **mode)


def _dft_cos_sin(n):
    k = np.arange(n)
    ang = 2.0 * np.pi * ((k[:, None] * k[None, :]) % n) / n
    return np.cos(ang), np.sin(ang)


def _fourier_a_kernel(x_ref, g_ref, f1_ref, zr_ref, zi_ref, *, n1, ts2):
    h = _rms(x_ref[...], g_ref[...])
    ht = jnp.swapaxes(h.astype(BF16), 0, 1)
    z = [jnp.dot(f1_ref[j], ht[j], preferred_element_type=F32) for j in range(ts2)]
    zr_ref[...] = jnp.swapaxes(jnp.stack([zj[:n1] for zj in z], axis=0).astype(BF16), 0, 1)
    zi_ref[...] = jnp.swapaxes(jnp.stack([zj[n1:] for zj in z], axis=0).astype(BF16), 0, 1)


def _fourier_c_kernel(zr_ref, zi_ref, x_ref, f2_ref, cs_ref, w_ref, o_ref, *, n2, tk1, scale):
    d = D_MODEL
    f2 = f2_ref[...]
    xs = []
    for j in range(tk1):
        zin = jnp.concatenate([zr_ref[j], zi_ref[j]], axis=0)
        xs.append(jnp.dot(f2, zin, preferred_element_type=F32))
    xr = jnp.concatenate([x[:n2] for x in xs], axis=0).astype(BF16)
    xi = jnp.concatenate([x[n2:] for x in xs], axis=0).astype(BF16)
    cs = cs_ref[...]
    mixed = []
    for gi in range(N_FGROUPS):
        sl = slice(gi * FGROUP_DIM, (gi + 1) * FGROUP_DIM)
        lhs = jnp.concatenate([xr[:, sl], xi[:, sl]], axis=1)
        mixed.append(jnp.dot(lhs, cs, preferred_element_type=F32) * scale)
    mixed = jnp.concatenate(mixed, axis=1).astype(BF16)
    delta = jnp.dot(mixed, w_ref[...], preferred_element_type=F32)
    o_ref[...] = x_ref[...] + jnp.swapaxes(delta.reshape(tk1, n2, d), 0, 1)


def _fourier_tables(seq_len):
    n1 = 64
    n2 = seq_len // n1
    s2_, k1_, s1_ = np.arange(n2)[:, None, None], np.arange(n1)[None, :, None], np.arange(n1)[None, None, :]
    ang = 2.0 * np.pi * ((k1_ * s1_ * n2 + k1_ * s2_) % seq_len) / seq_len
    f1 = np.concatenate([np.cos(ang), -np.sin(ang)], axis=1)
    c2, s2 = _dft_cos_sin(n2)
    f2 = np.block([[c2, s2], [-s2, c2]])
    cc, sc = _dft_cos_sin(FGROUP_DIM)
    cs = np.concatenate([cc, sc], axis=0)
    return n1, n2, f1, f2, cs


def _fourier_layer(x, g, w_bf16, layer):
    b, s, d = x.shape
    n1, n2, f1, f2, cs = _fourier_tables(s)
    ts2 = 16
    tk1 = 8
    f1 = jnp.asarray(f1, BF16)
    f2 = jnp.asarray(f2, BF16)
    cs = jnp.asarray(cs, BF16)
    g2 = g.reshape(1, d)

    z_shape = jax.ShapeDtypeStruct((b, n1, n2, d), BF16)
    zr, zi = pl.pallas_call(
        functools.partial(_fourier_a_kernel, n1=n1, ts2=ts2),
        grid=(b, n2 // ts2),
        in_specs=[
            pl.BlockSpec((None, n1, ts2, d), lambda bi, i: (bi, 0, i, 0)),
            pl.BlockSpec((1, d), lambda bi, i: (0, 0)),
            pl.BlockSpec((ts2, 2 * n1, n1), lambda bi, i: (i, 0, 0)),
        ],
        out_specs=[
            pl.BlockSpec((None, n1, ts2, d), lambda bi, i: (bi, 0, i, 0)),
            pl.BlockSpec((None, n1, ts2, d), lambda bi, i: (bi, 0, i, 0)),
        ],
        out_shape=[z_shape, z_shape],
        compiler_params=pltpu.CompilerParams(
            dimension_semantics=("parallel", "parallel"), vmem_limit_bytes=VMEM_LIMIT),
        name="fourier_a",
    )(x.reshape(b, n1, n2, d), g2, f1)

    scale = float(1.0 / np.sqrt(s * FGROUP_DIM))
    out = pl.pallas_call(
        functools.partial(_fourier_c_kernel, n2=n2, tk1=tk1, scale=scale),
        grid=(b, n1 // tk1),
        in_specs=[
            pl.BlockSpec((None, tk1, n2, d), lambda bi, i: (bi, i, 0, 0)),
            pl.BlockSpec((None, tk1, n2, d), lambda bi, i: (bi, i, 0, 0)),
            pl.BlockSpec((None, n2, tk1, d), lambda bi, i: (bi, 0, i, 0)),
            pl.BlockSpec((2 * n2, 2 * n2), lambda bi, i: (0, 0)),
            pl.BlockSpec((2 * FGROUP_DIM, FGROUP_DIM), lambda bi, i: (0, 0)),
            _layer_spec(w_bf16, layer),
        ],
        out_specs=pl.BlockSpec((None, n2, tk1, d), lambda bi, i: (bi, 0, i, 0)),
        out_shape=jax.ShapeDtypeStruct((b, n2, n1, d), F32),
        compiler_params=pltpu.CompilerParams(
            dimension_semantics=("parallel", "parallel"), vmem_limit_bytes=VMEM_LIMIT),
        name="fourier_c",
    )(zr, zi, x.reshape(b, n2, n1, d), f2, cs, w_bf16)
    return out.reshape(b, s, d)


def _qkv_kernel(x_ref, g_ref, w_ref, qg_ref, kg_ref, cos_ref, sin_ref, q_ref, k_ref, vt_ref):
    h = _rms(x_ref[...], g_ref[...]).astype(BF16)
    y = jnp.dot(h, w_ref[...], preferred_element_type=F32)
    cos = cos_ref[...]
    sin = sin_ref[...]
    ones = jnp.ones((2 * HEAD_DIM, HEAD_DIM), BF16)

    def norm_rope(y_head, a_own, a_partner):
        sq = y_head * y_head
        hi = sq.astype(BF16)
        lo = (sq - hi.astype(F32)).astype(BF16)
        ss = jnp.dot(jnp.concatenate([hi, lo], axis=1), ones, preferred_element_type=F32)
        r = lax.rsqrt(ss * (1.0 / HEAD_DIM) + EPS)
        return (y_head * a_own + pltpu.roll(y_head, HEAD_DIM // 2, axis=1) * a_partner) * r

    qg = qg_ref[...] * ATTN_Q_SCALE
    kg = kg_ref[...]
    aq_own, aq_partner = cos * qg[0:1], sin * qg[1:2]
    ak_own, ak_partner = cos * kg[0:1], sin * kg[1:2]
    for hd in range(N_HEADS):
        sl = slice(hd * HEAD_DIM, (hd + 1) * HEAD_DIM)
        q_ref[:, sl] = norm_rope(y[:, sl], aq_own, aq_partner).astype(BF16)
    for hd in range(N_KV_HEADS):
        sl = slice(hd * HEAD_DIM, (hd + 1) * HEAD_DIM)
        so = slice(Q_DIM + hd * HEAD_DIM, Q_DIM + (hd + 1) * HEAD_DIM)
        k_ref[:, sl] = norm_rope(y[:, so], ak_own, ak_partner).astype(BF16)
    vt_ref[...] = y[:, Q_DIM + KV_DIM:].T.astype(BF16)


def _attn_kernel(q_ref, k_ref, vt_ref, o_ref, s0_ref, s1_ref, acc_ref, *, tq, tk):
    n_chunks = vt_ref.shape[0]
    n_items = (q_ref.shape[0] // tq) * n_chunks
    m_cols = GROUP * tq

    def scores(t, dst_ref):
        q_start = pl.multiple_of((t // n_chunks) * tq, tq)
        k_start = pl.multiple_of((t % n_chunks) * tk, tk)
        qb = q_ref[pl.ds(q_start, tq), :]
        q = jnp.concatenate([qb[:, gi * HEAD_DIM:(gi + 1) * HEAD_DIM] for gi in range(GROUP)], axis=0)
        kc = k_ref[pl.ds(k_start, tk), :]
        st = lax.dot_general(kc, q, (((1,), (1,)), ((), ())), preferred_element_type=F32)
        dst_ref[...] = st
        return jnp.max(st, axis=0, keepdims=True)

    def consume(t, src_ref, m, alpha, l):
        p = jnp.exp2(src_ref[...] - m)
        l = alpha * l + jnp.sum(p, axis=0, keepdims=True)
        pv = jnp.dot(vt_ref[t % n_chunks], p.astype(BF16), preferred_element_type=F32)
        acc_ref[...] = alpha * acc_ref[...] + pv
        return l

    def finalize(t, l):
        o_t = acc_ref[...] / l
        q_start = pl.multiple_of((t // n_chunks) * tq, tq)
        for gi in range(GROUP):
            o_ref[pl.ds(q_start, tq), gi * HEAD_DIM:(gi + 1) * HEAD_DIM] = (
                o_t[:, gi * tq:(gi + 1) * tq].T.astype(BF16))

    def step(t, cur_ref, nxt_ref, carry, maybe_last):
        m, alpha, l = carry
        cmax = scores(t + 1, nxt_ref)
        l = consume(t, cur_ref, m, alpha, l)
        if not maybe_last:
            m_next = jnp.maximum(m, cmax)
            return m_next, jnp.exp2(m - m_next), l
        last = (t % n_chunks) == n_chunks - 1
        pl.when(last)(lambda: finalize(t, l))
        m_next = jnp.where(last, cmax, jnp.maximum(m, cmax))
        alpha_next = jnp.where(last, 0.0, jnp.exp2(m - m_next))
        return m_next, alpha_next, l

    def body(tt, carry):
        carry = step(2 * tt, s0_ref, s1_ref, carry, False)
        return step(2 * tt + 1, s1_ref, s0_ref, carry, True)

    acc_ref[...] = jnp.zeros_like(acc_ref)
    m0 = scores(0, s0_ref)
    zero = jnp.zeros((1, m_cols), F32)
    carry = lax.fori_loop(0, n_items // 2 - 1, body, (m0, zero, zero))
    m, alpha, l = step(n_items - 2, s0_ref, s1_ref, carry, False)
    l = consume(n_items - 1, s1_ref, m, alpha, l)
    finalize(n_items - 1, l)


def _attn_bounded_kernel(q_ref, k_ref, vt_ref, o_ref, p0_ref, p1_ref, acc_ref, *, tq, tk):
    n_chunks = vt_ref.shape[0]
    n_items = (q_ref.shape[0] // tq) * n_chunks
    m_cols = GROUP * tq

    def probs(t, dst_ref):
        q_start = pl.multiple_of((t // n_chunks) * tq, tq)
        k_start = pl.multiple_of((t % n_chunks) * tk, tk)
        qb = q_ref[pl.ds(q_start, tq), :]
        q = jnp.concatenate([qb[:, gi * HEAD_DIM:(gi + 1) * HEAD_DIM] for gi in range(GROUP)], axis=0)
        kc = k_ref[pl.ds(k_start, tk), :]
        st = lax.dot_general(kc, q, (((1,), (1,)), ((), ())), preferred_element_type=F32)
        p = jnp.exp2(st)
        dst_ref[...] = p.astype(BF16)
        return jnp.sum(p, axis=0, keepdims=True)

    def accumulate(t, src_ref):
        acc_ref[...] += jnp.dot(vt_ref[t % n_chunks], src_ref[...], preferred_element_type=F32)

    def finalize(t, l):
        o_t = acc_ref[...] / l
        q_start = pl.multiple_of((t // n_chunks) * tq, tq)
        for gi in range(GROUP):
            o_ref[pl.ds(q_start, tq), gi * HEAD_DIM:(gi + 1) * HEAD_DIM] = (
                o_t[:, gi * tq:(gi + 1) * tq].T.astype(BF16))
        acc_ref[...] = jnp.zeros_like(acc_ref)

    def step(t, cur_ref, nxt_ref, carry, maybe_last):
        l, lsum = carry
        lsum_next = probs(t + 1, nxt_ref)
        accumulate(t, cur_ref)
        l = l + lsum
        if not maybe_last:
            return l, lsum_next
        last = (t % n_chunks) == n_chunks - 1
        pl.when(last)(lambda: finalize(t, l))
        return jnp.where(last, 0.0, l), lsum_next

    def body(tt, carry):
        carry = step(4 * tt, p0_ref, p1_ref, carry, False)
        carry = step(4 * tt + 1, p1_ref, p0_ref, carry, False)
        carry = step(4 * tt + 2, p0_ref, p1_ref, carry, False)
        return step(4 * tt + 3, p1_ref, p0_ref, carry, True)

    acc_ref[...] = jnp.zeros_like(acc_ref)
    lsum0 = probs(0, p0_ref)
    carry = lax.fori_loop(0, n_items // 4 - 1, body, (jnp.zeros((1, m_cols), F32), lsum0))
    carry = step(n_items - 4, p0_ref, p1_ref, carry, False)
    carry = step(n_items - 3, p1_ref, p0_ref, carry, False)
    l, lsum = step(n_items - 2, p0_ref, p1_ref, carry, False)
    accumulate(n_items - 1, p1_ref)
    finalize(n_items - 1, l + lsum)


def _rope_tables(seq_len):
    rows = seq_len // GRID_W
    row = jnp.repeat(jnp.arange(rows, dtype=F32), GRID_W)
    col = jnp.tile(jnp.arange(GRID_W, dtype=F32), rows)
    inv = ROPE_THETA ** (-jnp.arange(0, AXIS_ROT_DIM, 2, dtype=F32) / AXIS_ROT_DIM)
    ang = jnp.concatenate([row[:, None] * inv, col[:, None] * inv], axis=-1)
    cos, sin = jnp.cos(ang), jnp.sin(ang)
    return jnp.concatenate([cos, cos], axis=-1), jnp.concatenate([-sin, sin], axis=-1)


def _attention_qkv(x, g, w_qkv_bf16, layer, q_gain, k_gain):
    b, s, d = x.shape
    t = b * s
    tq = 256
    tk = 1024
    tm = 512
    per_chunk = tk // tm
    assert s % (4 * tk) == 0 and s % tq == 0, "attention loops take key chunks four at a time"
    cos, sin = _rope_tables(s)
    nblk = s // tm
    q, k, vt = pl.pallas_call(
        _qkv_kernel,
        grid=(t // tm,),
        in_specs=[
            pl.BlockSpec((tm, d), lambda i: (i, 0)),
            pl.BlockSpec((1, d), lambda i: (0, 0)),
            _layer_spec(w_qkv_bf16, layer),
            pl.BlockSpec((2, HEAD_DIM), lambda i: (0, 0)),
            pl.BlockSpec((2, HEAD_DIM), lambda i: (0, 0)),
            pl.BlockSpec((tm, HEAD_DIM), lambda i: (i % nblk, 0)),
            pl.BlockSpec((tm, HEAD_DIM), lambda i: (i % nblk, 0)),
        ],
        out_specs=[
            pl.BlockSpec((tm, Q_DIM), lambda i: (i, 0)),
            pl.BlockSpec((tm, KV_DIM), lambda i: (i, 0)),
            pl.BlockSpec((None, KV_DIM, tm), lambda i: (i // per_chunk, 0, i % per_chunk)),
        ],
        out_shape=[
            jax.ShapeDtypeStruct((t, Q_DIM), BF16),
            jax.ShapeDtypeStruct((t, KV_DIM), BF16),
            jax.ShapeDtypeStruct((t // tk, KV_DIM, tk), BF16),
        ],
        compiler_params=pltpu.CompilerParams(
            dimension_semantics=("parallel",), vmem_limit_bytes=VMEM_LIMIT),
        name="qkv_proj",
    )(x.reshape(t, d), g.reshape(1, d), w_qkv_bf16, q_gain, k_gain, cos, sin)

    def attention_call(body, buf_dtype, name):
        return pl.pallas_call(
            functools.partial(body, tq=tq, tk=tk),
            grid=(b, N_KV_HEADS),
            in_specs=[
                pl.BlockSpec((None, s, GROUP * HEAD_DIM), lambda bi, hi: (bi, 0, hi),
                             pipeline_mode=pl.Buffered(1)),
                pl.BlockSpec((None, s, HEAD_DIM), lambda bi, hi: (bi, 0, hi)),
                pl.BlockSpec((None, s // tk, HEAD_DIM, tk), lambda bi, hi: (bi, 0, hi, 0)),
            ],
            out_specs=pl.BlockSpec((None, s, GROUP * HEAD_DIM), lambda bi, hi: (bi, 0, hi)),
            out_shape=jax.ShapeDtypeStruct((b, s, Q_DIM), BF16),
            scratch_shapes=[
                pltpu.VMEM((tk, GROUP * tq), buf_dtype),
                pltpu.VMEM((tk, GROUP * tq), buf_dtype),
                pltpu.VMEM((HEAD_DIM, GROUP * tq), F32),
            ],
            compiler_params=pltpu.CompilerParams(
                dimension_semantics=("parallel", "parallel"), vmem_limit_bytes=VMEM_LIMIT),
            name=name,
        )

    score_bound = (HEAD_DIM * ATTN_Q_SCALE) * jnp.max(jnp.abs(q_gain)) * jnp.max(jnp.abs(k_gain))
    o = lax.cond(
        score_bound <= ATTN_BOUNDED_MAX_LOG2_SCORE,
        attention_call(_attn_bounded_kernel, BF16, "attention_bounded"),
        attention_call(_attn_kernel, F32, "attention"),
        q.reshape(b, s, Q_DIM), k.reshape(b, s, KV_DIM), vt.reshape(b, s // tk, KV_DIM, tk))
    return o.reshape(t, Q_DIM)


def _ffn_kernel(*refs, with_attn):
    if with_attn:
        x_ref, a_ref, wo_ref, g_ref, wg_ref, wu_ref, wd_ref, o_ref = refs
        x = x_ref[...] + jnp.dot(a_ref[...], wo_ref[...], preferred_element_type=F32)
    else:
        x_ref, g_ref, wg_ref, wu_ref, wd_ref, o_ref = refs
        x = x_ref[...]
    h = _rms(x, g_ref[...]).astype(BF16)
    gate = jnp.dot(h, wg_ref[...], preferred_element_type=F32)
    up = jnp.dot(h, wu_ref[...], preferred_element_type=F32)
    act = (gate * jax.nn.sigmoid(gate) * up).astype(BF16)
    o_ref[...] = x + jnp.dot(act, wd_ref[...], preferred_element_type=F32)


def _ffn_layer(x2d, g, layer, wg, wu, wd, attn=None):
    t, d = x2d.shape
    dff = wg.shape[2]
    tm = 512
    resident = functools.partial(pl.BlockSpec, index_map=lambda i: (0, 0), pipeline_mode=pl.Buffered(1))
    row = lambda width: pl.BlockSpec((tm, width), lambda i: (i, 0))
    in_specs = [row(d)]
    args = [x2d]
    if attn is not None:
        a2d, wo, attn_layer = attn
        in_specs += [row(Q_DIM), _layer_spec(wo, attn_layer, single_buffer=True)]
        args += [a2d, wo]
    in_specs += [resident((1, d))] + [_layer_spec(w, layer, single_buffer=True) for w in (wg, wu, wd)]
    args += [g.reshape(1, d), wg, wu, wd]
    return pl.pallas_call(
        functools.partial(_ffn_kernel, with_attn=attn is not None),
        grid=(t // tm,),
        in_specs=in_specs,
        out_specs=row(d),
        out_shape=jax.ShapeDtypeStruct((t, d), F32),
        compiler_params=pltpu.CompilerParams(
            dimension_semantics=("parallel",), vmem_limit_bytes=VMEM_LIMIT),
        name="swiglu",
    )(*args)


def _pair_split_perm():
    own = np.concatenate([np.arange(0, HEAD_DIM, 2), np.arange(1, HEAD_DIM, 2)])
    partner = np.roll(own, HEAD_DIM // 2)
    cols = [h * HEAD_DIM + own for h in range(N_HEADS + N_KV_HEADS)]
    cols.append(np.arange(Q_DIM + KV_DIM, QKV_DIM))
    return np.stack([own, partner]), np.concatenate(cols)


def kernel(x_prompt, x_sample, norm_mix, norm_ffn, fourier_w, attn_w_qkv, attn_q_norm, attn_k_norm,
           attn_w_o, ffn_w_gate, ffn_w_up, ffn_w_down):
    depth = norm_mix.shape[0]
    head_perm, col_perm = _pair_split_perm()
    fw = fourier_w.astype(BF16)
    wqkv = attn_w_qkv[:, :, col_perm].astype(BF16)
    qn = attn_q_norm[:, head_perm]
    kn = attn_k_norm[:, head_perm]
    wo = attn_w_o.astype(BF16)
    wg = ffn_w_gate.astype(BF16)
    wu = ffn_w_up.astype(BF16)
    wd = ffn_w_down.astype(BF16)

    def run_trunk(x):
        b, s, d = x.shape
        for i in range(depth):
            j = i // 2
            if i % 2 == 0:
                x = _fourier_layer(x, norm_mix[i], fw, j)
                x2d = _ffn_layer(x.reshape(b * s, d), norm_ffn[i], i, wg, wu, wd)
            else:
                a2d = _attention_qkv(x, norm_mix[i], wqkv, j, qn[j], kn[j])
                x2d = _ffn_layer(x.reshape(b * s, d), norm_ffn[i], i, wg, wu, wd, attn=(a2d, wo, j))
            x = x2d.reshape(b, s, d)
        return x

    return (run_trunk(x_prompt), run_trunk(x_sample))
```

```python
import functools

import numpy as np
import jax
import jax.numpy as jnp
from jax import lax
from jax.experimental import pallas as pl
from jax.experimental.pallas import tpu as pltpu

D_MODEL = 1024
HEAD_DIM = 128
N_HEADS = D_MODEL // HEAD_DIM
N_KV_HEADS = 2
GROUP = N_HEADS // N_KV_HEADS
Q_DIM = N_HEADS * HEAD_DIM
KV_DIM = N_KV_HEADS * HEAD_DIM
QKV_DIM = Q_DIM + 2 * KV_DIM
GRID_W = 64
AXIS_ROT_DIM = HEAD_DIM // 2
ROPE_THETA = 10000.0
N_FGROUPS = 8
FGROUP_DIM = D_MODEL // N_FGROUPS
EPS = 1e-6

ATTN_Q_SCALE = float(HEAD_DIM ** -0.5 * np.log2(np.e))
ATTN_BOUNDED_MAX_LOG2_SCORE = 60.0

LANES = 128
VMEM_LIMIT = 56 * 1024 * 1024

F32 = jnp.float32
BF16 = jnp.bfloat16


def _rms(x, g):
    ms = jnp.mean(x * x, axis=-1, keepdims=True)
    return x * lax.rsqrt(ms + EPS) * g


def _layer_spec(stacked, layer, single_buffer=False):
    zeros = (0,) * (stacked.ndim - 1)
    mode = dict(pipeline_mode=pl.Buffered(1)) if single_buffer else {}
    return pl.BlockSpec((None,) + stacked.shape[1:], lambda *_: (layer,) + zeros, **mode)


def _dft_cos_sin(n):
    k = np.arange(n)
    ang = 2.0 * np.pi * ((k[:, None] * k[None, :]) % n) / n
    return np.cos(ang), np.sin(ang)


def _fourier_a_kernel(x_ref, g_ref, f1_ref, cs_ref, yr_ref, yi_ref, *, n1, ts2):
    d = D_MODEL
    h = _rms(x_ref[...], g_ref[...])
    ht = jnp.swapaxes(h.astype(BF16), 0, 1)
    z = [jnp.dot(f1_ref[j], ht[j], preferred_element_type=F32) for j in range(ts2)]
    zr = jnp.concatenate([zj[:n1] for zj in z], axis=0).astype(BF16)
    zi = jnp.concatenate([zj[n1:] for zj in z], axis=0).astype(BF16)
    cs = cs_ref[...]
    yr, yi = [], []
    for gi in range(N_FGROUPS):
        sl = slice(gi * FGROUP_DIM, (gi + 1) * FGROUP_DIM)
        y = jnp.dot(jnp.concatenate([zr[:, sl], zi[:, sl]], axis=1), cs, preferred_element_type=F32)
        yr.append(y[:, :FGROUP_DIM])
        yi.append(y[:, FGROUP_DIM:])
    yr = jnp.concatenate(yr, axis=1).astype(BF16).reshape(ts2, n1, d)
    yi = jnp.concatenate(yi, axis=1).astype(BF16).reshape(ts2, n1, d)
    yr_ref[...] = jnp.swapaxes(yr, 0, 1)
    yi_ref[...] = jnp.swapaxes(yi, 0, 1)


def _fourier_c_kernel(yr_ref, yi_ref, x_ref, f2_ref, w_ref, o_ref, *, n2, tk1, scale):
    d = D_MODEL
    f2 = f2_ref[...]
    mixed = [jnp.dot(f2, jnp.concatenate([yr_ref[j], yi_ref[j]], axis=0), preferred_element_type=F32)
             for j in range(tk1)]
    mixed = (jnp.concatenate(mixed, axis=0) * scale).astype(BF16)
    delta = jnp.dot(mixed, w_ref[...], preferred_element_type=F32)
    o_ref[...] = x_ref[...] + jnp.swapaxes(delta.reshape(tk1, n2, d), 0, 1)


def _fourier_tables(seq_len):
    n1 = 64
    n2 = seq_len // n1
    s2_, k1_, s1_ = np.arange(n2)[:, None, None], np.arange(n1)[None, :, None], np.arange(n1)[None, None, :]
    ang = 2.0 * np.pi * ((k1_ * s1_ * n2 + k1_ * s2_) % seq_len) / seq_len
    f1 = np.concatenate([np.cos(ang), -np.sin(ang)], axis=1)
    c2, s2 = _dft_cos_sin(n2)
    f2 = np.concatenate([c2, s2], axis=1)
    cc, sc = _dft_cos_sin(FGROUP_DIM)
    cs = np.block([[cc, -sc], [sc, cc]])
    return n1, n2, f1, f2, cs


def _fourier_layer(x, g, w_bf16, layer):
    b, s, d = x.shape
    n1, n2, f1, f2, cs = _fourier_tables(s)
    ts2 = 16
    tk1 = 8
    f1 = jnp.asarray(f1, BF16)
    f2 = jnp.asarray(f2, BF16)
    cs = jnp.asarray(cs, BF16)
    g2 = g.reshape(1, d)

    z_shape = jax.ShapeDtypeStruct((b, n1, n2, d), BF16)
    zr, zi = pl.pallas_call(
        functools.partial(_fourier_a_kernel, n1=n1, ts2=ts2),
        grid=(b, n2 // ts2),
        in_specs=[
            pl.BlockSpec((None, n1, ts2, d), lambda bi, i: (bi, 0, i, 0)),
            pl.BlockSpec((1, d), lambda bi, i: (0, 0)),
            pl.BlockSpec((ts2, 2 * n1, n1), lambda bi, i: (i, 0, 0)),
            pl.BlockSpec((2 * FGROUP_DIM, 2 * FGROUP_DIM), lambda bi, i: (0, 0)),
        ],
        out_specs=[
            pl.BlockSpec((None, n1, ts2, d), lambda bi, i: (bi, 0, i, 0)),
            pl.BlockSpec((None, n1, ts2, d), lambda bi, i: (bi, 0, i, 0)),
        ],
        out_shape=[z_shape, z_shape],
        compiler_params=pltpu.CompilerParams(
            dimension_semantics=("parallel", "parallel"), vmem_limit_bytes=VMEM_LIMIT),
        name="fourier_a",
    )(x.reshape(b, n1, n2, d), g2, f1, cs)

    scale = float(1.0 / np.sqrt(s * FGROUP_DIM))
    out = pl.pallas_call(
        functools.partial(_fourier_c_kernel, n2=n2, tk1=tk1, scale=scale),
        grid=(b, n1 // tk1),
        in_specs=[
            pl.BlockSpec((None, tk1, n2, d), lambda bi, i: (bi, i, 0, 0)),
            pl.BlockSpec((None, tk1, n2, d), lambda bi, i: (bi, i, 0, 0)),
            pl.BlockSpec((None, n2, tk1, d), lambda bi, i: (bi, 0, i, 0)),
            pl.BlockSpec((n2, 2 * n2), lambda bi, i: (0, 0)),
            _layer_spec(w_bf16, layer),
        ],
        out_specs=pl.BlockSpec((None, n2, tk1, d), lambda bi, i: (bi, 0, i, 0)),
        out_shape=jax.ShapeDtypeStruct((b, n2, n1, d), F32),
        compiler_params=pltpu.CompilerParams(
            dimension_semantics=("parallel", "parallel"), vmem_limit_bytes=VMEM_LIMIT),
        name="fourier_c",
    )(zr, zi, x.reshape(b, n2, n1, d), f2, w_bf16)
    return out.reshape(b, s, d)


def _qkv_kernel(x_ref, g_ref, w_ref, qg_ref, kg_ref, cos_ref, sin_ref, q_ref, k_ref, vt_ref):
    h = _rms(x_ref[...], g_ref[...]).astype(BF16)
    y = jnp.dot(h, w_ref[...], preferred_element_type=F32)
    cos = cos_ref[...]
    sin = sin_ref[...]
    ones = jnp.ones((2 * HEAD_DIM, HEAD_DIM), BF16)

    def norm_rope(y_head, a_own, a_partner):
        sq = y_head * y_head
        hi = sq.astype(BF16)
        lo = (sq - hi.astype(F32)).astype(BF16)
        ss = jnp.dot(jnp.concatenate([hi, lo], axis=1), ones, preferred_element_type=F32)
        r = lax.rsqrt(ss * (1.0 / HEAD_DIM) + EPS)
        return (y_head * a_own + pltpu.roll(y_head, HEAD_DIM // 2, axis=1) * a_partner) * r

    qg = qg_ref[...] * ATTN_Q_SCALE
    kg = kg_ref[...]
    aq_own, aq_partner = cos * qg[0:1], sin * qg[1:2]
    ak_own, ak_partner = cos * kg[0:1], sin * kg[1:2]
    for hd in range(N_HEADS):
        sl = slice(hd * HEAD_DIM, (hd + 1) * HEAD_DIM)
        q_ref[:, sl] = norm_rope(y[:, sl], aq_own, aq_partner).astype(BF16)
    for hd in range(N_KV_HEADS):
        sl = slice(hd * HEAD_DIM, (hd + 1) * HEAD_DIM)
        so = slice(Q_DIM + hd * HEAD_DIM, Q_DIM + (hd + 1) * HEAD_DIM)
        k_ref[:, sl] = norm_rope(y[:, so], ak_own, ak_partner).astype(BF16)
    vt_ref[...] = y[:, Q_DIM + KV_DIM:].T.astype(BF16)


def _attn_kernel(q_ref, k_ref, vt_ref, o_ref, s0_ref, s1_ref, acc_ref, *, tq, tk):
    n_chunks = vt_ref.shape[0]
    n_items = (q_ref.shape[0] // tq) * n_chunks
    m_cols = GROUP * tq

    def scores(t, dst_ref):
        q_start = pl.multiple_of((t // n_chunks) * tq, tq)
        k_start = pl.multiple_of((t % n_chunks) * tk, tk)
        qb = q_ref[pl.ds(q_start, tq), :]
        q = jnp.concatenate([qb[:, gi * HEAD_DIM:(gi + 1) * HEAD_DIM] for gi in range(GROUP)], axis=0)
        kc = k_ref[pl.ds(k_start, tk), :]
        st = lax.dot_general(kc, q, (((1,), (1,)), ((), ())), preferred_element_type=F32)
        dst_ref[...] = st
        return jnp.max(st, axis=0, keepdims=True)

    def consume(t, src_ref, m, alpha, l):
        p = jnp.exp2(src_ref[...] - m)
        l = alpha * l + jnp.sum(p, axis=0, keepdims=True)
        pv = jnp.dot(vt_ref[t % n_chunks], p.astype(BF16), preferred_element_type=F32)
        acc_ref[...] = alpha * acc_ref[...] + pv
        return l

    def finalize(t, l):
        o_t = acc_ref[...] / l
        q_start = pl.multiple_of((t // n_chunks) * tq, tq)
        for gi in range(GROUP):
            o_ref[pl.ds(q_start, tq), gi * HEAD_DIM:(gi + 1) * HEAD_DIM] = (
                o_t[:, gi * tq:(gi + 1) * tq].T.astype(BF16))

    def step(t, cur_ref, nxt_ref, carry, maybe_last):
        m, alpha, l = carry
        cmax = scores(t + 1, nxt_ref)
        l = consume(t, cur_ref, m, alpha, l)
        if not maybe_last:
            m_next = jnp.maximum(m, cmax)
            return m_next, jnp.exp2(m - m_next), l
        last = (t % n_chunks) == n_chunks - 1
        pl.when(last)(lambda: finalize(t, l))
        m_next = jnp.where(last, cmax, jnp.maximum(m, cmax))
        alpha_next = jnp.where(last, 0.0, jnp.exp2(m - m_next))
        return m_next, alpha_next, l

    def body(tt, carry):
        carry = step(2 * tt, s0_ref, s1_ref, carry, False)
        return step(2 * tt + 1, s1_ref, s0_ref, carry, True)

    acc_ref[...] = jnp.zeros_like(acc_ref)
    m0 = scores(0, s0_ref)
    zero = jnp.zeros((1, m_cols), F32)
    carry = lax.fori_loop(0, n_items // 2 - 1, body, (m0, zero, zero))
    m, alpha, l = step(n_items - 2, s0_ref, s1_ref, carry, False)
    l = consume(n_items - 1, s1_ref, m, alpha, l)
    finalize(n_items - 1, l)


def _attn_bounded_kernel(q_ref, k_ref, vt_ref, o_ref, p0_ref, p1_ref, acc_ref, *, tq, tk):
    n_chunks = vt_ref.shape[0]
    n_items = (q_ref.shape[0] // tq) * n_chunks
    m_cols = GROUP * tq

    def probs(t, dst_ref):
        q_start = pl.multiple_of((t // n_chunks) * tq, tq)
        k_start = pl.multiple_of((t % n_chunks) * tk, tk)
        qb = q_ref[pl.ds(q_start, tq), :]
        q = jnp.concatenate([qb[:, gi * HEAD_DIM:(gi + 1) * HEAD_DIM] for gi in range(GROUP)], axis=0)
        kc = k_ref[pl.ds(k_start, tk), :]
        st = lax.dot_general(kc, q, (((1,), (1,)), ((), ())), preferred_element_type=F32)
        p = jnp.exp2(st)
        dst_ref[...] = p.astype(BF16)
        return jnp.sum(p, axis=0, keepdims=True)

    def accumulate(t, src_ref):
        acc_ref[...] += jnp.dot(vt_ref[t % n_chunks], src_ref[...], preferred_element_type=F32)

    def finalize(t, l):
        o_t = acc_ref[...] / l
        q_start = pl.multiple_of((t // n_chunks) * tq, tq)
        for gi in range(GROUP):
            o_ref[pl.ds(q_start, tq), gi * HEAD_DIM:(gi + 1) * HEAD_DIM] = (
                o_t[:, gi * tq:(gi + 1) * tq].T.astype(BF16))
        acc_ref[...] = jnp.zeros_like(acc_ref)

    def step(t, cur_ref, nxt_ref, carry, maybe_last):
        l, lsum = carry
        lsum_next = probs(t + 1, nxt_ref)
        accumulate(t, cur_ref)
        l = l + lsum
        if not maybe_last:
            return l, lsum_next
        last = (t % n_chunks) == n_chunks - 1
        pl.when(last)(lambda: finalize(t, l))
        return jnp.where(last, 0.0, l), lsum_next

    def body(tt, carry):
        carry = step(4 * tt, p0_ref, p1_ref, carry, False)
        carry = step(4 * tt + 1, p1_ref, p0_ref, carry, False)
        carry = step(4 * tt + 2, p0_ref, p1_ref, carry, False)
        return step(4 * tt + 3, p1_ref, p0_ref, carry, True)

    acc_ref[...] = jnp.zeros_like(acc_ref)
    lsum0 = probs(0, p0_ref)
    carry = lax.fori_loop(0, n_items // 4 - 1, body, (jnp.zeros((1, m_cols), F32), lsum0))
    carry = step(n_items - 4, p0_ref, p1_ref, carry, False)
    carry = step(n_items - 3, p1_ref, p0_ref, carry, False)
    l, lsum = step(n_items - 2, p0_ref, p1_ref, carry, False)
    accumulate(n_items - 1, p1_ref)
    finalize(n_items - 1, l + lsum)


def _rope_tables(seq_len):
    rows = seq_len // GRID_W
    row = jnp.repeat(jnp.arange(rows, dtype=F32), GRID_W)
    col = jnp.tile(jnp.arange(GRID_W, dtype=F32), rows)
    inv = ROPE_THETA ** (-jnp.arange(0, AXIS_ROT_DIM, 2, dtype=F32) / AXIS_ROT_DIM)
    ang = jnp.concatenate([row[:, None] * inv, col[:, None] * inv], axis=-1)
    cos, sin = jnp.cos(ang), jnp.sin(ang)
    return jnp.concatenate([cos, cos], axis=-1), jnp.concatenate([-sin, sin], axis=-1)


def _attention_qkv(x, g, w_qkv_bf16, layer, q_gain, k_gain):
    b, s, d = x.shape
    t = b * s
    tq = 256
    tk = 1024
    tm = 512
    per_chunk = tk // tm
    assert s % (4 * tk) == 0 and s % tq == 0, "attention loops take key chunks four at a time"
    cos, sin = _rope_tables(s)
    nblk = s // tm
    q, k, vt = pl.pallas_call(
        _qkv_kernel,
        grid=(t // tm,),
        in_specs=[
            pl.BlockSpec((tm, d), lambda i: (i, 0)),
            pl.BlockSpec((1, d), lambda i: (0, 0)),
            _layer_spec(w_qkv_bf16, layer),
            pl.BlockSpec((2, HEAD_DIM), lambda i: (0, 0)),
            pl.BlockSpec((2, HEAD_DIM), lambda i: (0, 0)),
            pl.BlockSpec((tm, HEAD_DIM), lambda i: (i % nblk, 0)),
            pl.BlockSpec((tm, HEAD_DIM), lambda i: (i % nblk, 0)),
        ],
        out_specs=[
            pl.BlockSpec((tm, Q_DIM), lambda i: (i, 0)),
            pl.BlockSpec((tm, KV_DIM), lambda i: (i, 0)),
            pl.BlockSpec((None, KV_DIM, tm), lambda i: (i // per_chunk, 0, i % per_chunk)),
        ],
        out_shape=[
            jax.ShapeDtypeStruct((t, Q_DIM), BF16),
            jax.ShapeDtypeStruct((t, KV_DIM), BF16),
            jax.ShapeDtypeStruct((t // tk, KV_DIM, tk), BF16),
        ],
        compiler_params=pltpu.CompilerParams(
            dimension_semantics=("parallel",), vmem_limit_bytes=VMEM_LIMIT),
        name="qkv_proj",
    )(x.reshape(t, d), g.reshape(1, d), w_qkv_bf16, q_gain, k_gain, cos, sin)

    def attention_call(body, buf_dtype, name):
        return pl.pallas_call(
            functools.partial(body, tq=tq, tk=tk),
            grid=(b, N_KV_HEADS),
            in_specs=[
                pl.BlockSpec((None, s, GROUP * HEAD_DIM), lambda bi, hi: (bi, 0, hi),
                             pipeline_mode=pl.Buffered(1)),
                pl.BlockSpec((None, s, HEAD_DIM), lambda bi, hi: (bi, 0, hi)),
                pl.BlockSpec((None, s // tk, HEAD_DIM, tk), lambda bi, hi: (bi, 0, hi, 0)),
            ],
            out_specs=pl.BlockSpec((None, s, GROUP * HEAD_DIM), lambda bi, hi: (bi, 0, hi)),
            out_shape=jax.ShapeDtypeStruct((b, s, Q_DIM), BF16),
            scratch_shapes=[
                pltpu.VMEM((tk, GROUP * tq), buf_dtype),
                pltpu.VMEM((tk, GROUP * tq), buf_dtype),
                pltpu.VMEM((HEAD_DIM, GROUP * tq), F32),
            ],
            compiler_params=pltpu.CompilerParams(
                dimension_semantics=("parallel", "parallel"), vmem_limit_bytes=VMEM_LIMIT),
            name=name,
        )

    score_bound = (HEAD_DIM * ATTN_Q_SCALE) * jnp.max(jnp.abs(q_gain)) * jnp.max(jnp.abs(k_gain))
    o = lax.cond(
        score_bound <= ATTN_BOUNDED_MAX_LOG2_SCORE,
        attention_call(_attn_bounded_kernel, BF16, "attention_bounded"),
        attention_call(_attn_kernel, F32, "attention"),
        q.reshape(b, s, Q_DIM), k.reshape(b, s, KV_DIM), vt.reshape(b, s // tk, KV_DIM, tk))
    return o.reshape(t, Q_DIM)


def _ffn_kernel(*refs, with_attn):
    if with_attn:
        x_ref, a_ref, wo_ref, g_ref, wg_ref, wu_ref, wd_ref, o_ref = refs
        x = x_ref[...] + jnp.dot(a_ref[...], wo_ref[...], preferred_element_type=F32)
    else:
        x_ref, g_ref, wg_ref, wu_ref, wd_ref, o_ref = refs
        x = x_ref[...]
    h = _rms(x, g_ref[...]).astype(BF16)
    gate = jnp.dot(h, wg_ref[...], preferred_element_type=F32)
    up = jnp.dot(h, wu_ref[...], preferred_element_type=F32)
    act = (gate * jax.nn.sigmoid(gate) * up).astype(BF16)
    o_ref[...] = x + jnp.dot(act, wd_ref[...], preferred_element_type=F32)


def _ffn_layer(x2d, g, layer, wg, wu, wd, attn=None):
    t, d = x2d.shape
    dff = wg.shape[2]
    tm = 512
    resident = functools.partial(pl.BlockSpec, index_map=lambda i: (0, 0), pipeline_mode=pl.Buffered(1))
    row = lambda width: pl.BlockSpec((tm, width), lambda i: (i, 0))
    in_specs = [row(d)]
    args = [x2d]
    if attn is not None:
        a2d, wo, attn_layer = attn
        in_specs += [row(Q_DIM), _layer_spec(wo, attn_layer, single_buffer=True)]
        args += [a2d, wo]
    in_specs += [resident((1, d))] + [_layer_spec(w, layer, single_buffer=True) for w in (wg, wu, wd)]
    args += [g.reshape(1, d), wg, wu, wd]
    return pl.pallas_call(
        functools.partial(_ffn_kernel, with_attn=attn is not None),
        grid=(t // tm,),
        in_specs=in_specs,
        out_specs=row(d),
        out_shape=jax.ShapeDtypeStruct((t, d), F32),
        compiler_params=pltpu.CompilerParams(
            dimension_semantics=("parallel",), vmem_limit_bytes=VMEM_LIMIT),
        name="swiglu",
    )(*args)


def _pair_split_perm():
    own = np.concatenate([np.arange(0, HEAD_DIM, 2), np.arange(1, HEAD_DIM, 2)])
    partner = np.roll(own, HEAD_DIM // 2)
    cols = [h * HEAD_DIM + own for h in range(N_HEADS + N_KV_HEADS)]
    cols.append(np.arange(Q_DIM + KV_DIM, QKV_DIM))
    return np.stack([own, partner]), np.concatenate(cols)


def kernel(x_prompt, x_sample, norm_mix, norm_ffn, fourier_w, attn_w_qkv, attn_q_norm, attn_k_norm,
           attn_w_o, ffn_w_gate, ffn_w_up, ffn_w_down):
    depth = norm_mix.shape[0]
    head_perm, col_perm = _pair_split_perm()
    fw = fourier_w.astype(BF16)
    wqkv = attn_w_qkv[:, :, col_perm].astype(BF16)
    qn = attn_q_norm[:, head_perm]
    kn = attn_k_norm[:, head_perm]
    wo = attn_w_o.astype(BF16)
    wg = ffn_w_gate.astype(BF16)
    wu = ffn_w_up.astype(BF16)
    wd = ffn_w_down.astype(BF16)

    def run_trunk(x):
        b, s, d = x.shape
        for i in range(depth):
            j = i // 2
            if i % 2 == 0:
                x = _fourier_layer(x, norm_mix[i], fw, j)
                x2d = _ffn_layer(x.reshape(b * s, d), norm_ffn[i], i, wg, wu, wd)
            else:
                a2d = _attention_qkv(x, norm_mix[i], wqkv, j, qn[j], kn[j])
                x2d = _ffn_layer(x.reshape(b * s, d), norm_ffn[i], i, wg, wu, wd, attn=(a2d, wo, j))
            x = x2d.reshape(b, s, d)
        return x

    return (run_trunk(x_prompt), run_trunk(x_sample))
```

```python
import functools

import numpy as np
import jax
import jax.numpy as jnp
from jax import lax
from jax.experimental import pallas as pl
from jax.experimental.pallas import tpu as pltpu

D_MODEL = 1024
HEAD_DIM = 128
N_HEADS = D_MODEL // HEAD_DIM
N_KV_HEADS = 2
GROUP = N_HEADS // N_KV_HEADS
Q_DIM = N_HEADS * HEAD_DIM
KV_DIM = N_KV_HEADS * HEAD_DIM
QKV_DIM = Q_DIM + 2 * KV_DIM
GRID_W = 64
AXIS_ROT_DIM = HEAD_DIM // 2
ROPE_THETA = 10000.0
N_FGROUPS = 8
FGROUP_DIM = D_MODEL // N_FGROUPS
EPS = 1e-6

ATTN_Q_SCALE = float(HEAD_DIM ** -0.5 * np.log2(np.e))
ATTN_BOUNDED_MAX_LOG2_SCORE = 60.0

LANES = 128
VMEM_LIMIT = 56 * 1024 * 1024

F32 = jnp.float32
BF16 = jnp.bfloat16
FOURIER_DELTA_DTYPE = BF16


def _rms(x, g):
    ms = jnp.mean(x * x, axis=-1, keepdims=True)
    return x * lax.rsqrt(ms + EPS) * g


def _layer_spec(stacked, layer, single_buffer=False):
    zeros = (0,) * (stacked.ndim - 1)
    mode = dict(pipeline_mode=pl.Buffered(1)) if single_buffer else {}
    return pl.BlockSpec((None,) + stacked.shape[1:], lambda *_: (layer,) + zeros, **mode)


def _dft_cos_sin(n):
    k = np.arange(n)
    ang = 2.0 * np.pi * ((k[:, None] * k[None, :]) % n) / n
    return np.cos(ang), np.sin(ang)


def _fourier_a_kernel(x_ref, g_ref, f1_ref, cs_ref, yr_ref, yi_ref, *, n1, ts2):
    d = D_MODEL
    h = _rms(x_ref[...], g_ref[...])
    ht = jnp.swapaxes(h.astype(BF16), 0, 1)
    z = [jnp.dot(f1_ref[j], ht[j], preferred_element_type=F32) for j in range(ts2)]
    zr = jnp.concatenate([zj[:n1] for zj in z], axis=0).astype(BF16)
    zi = jnp.concatenate([zj[n1:] for zj in z], axis=0).astype(BF16)
    cs = cs_ref[...]
    yr, yi = [], []
    for gi in range(N_FGROUPS):
        sl = slice(gi * FGROUP_DIM, (gi + 1) * FGROUP_DIM)
        y = jnp.dot(jnp.concatenate([zr[:, sl], zi[:, sl]], axis=1), cs, preferred_element_type=F32)
        yr.append(y[:, :FGROUP_DIM])
        yi.append(y[:, FGROUP_DIM:])
    yr = jnp.concatenate(yr, axis=1).astype(BF16).reshape(ts2, n1, d)
    yi = jnp.concatenate(yi, axis=1).astype(BF16).reshape(ts2, n1, d)
    yr_ref[...] = jnp.swapaxes(yr, 0, 1)
    yi_ref[...] = jnp.swapaxes(yi, 0, 1)


def _fourier_c_kernel(yr_ref, yi_ref, f2_ref, w_ref, o_ref, *, n2, tk1, scale):
    d = D_MODEL
    f2 = f2_ref[...]
    mixed = [jnp.dot(f2, jnp.concatenate([yr_ref[j], yi_ref[j]], axis=0), preferred_element_type=F32)
             for j in range(tk1)]
    mixed = (jnp.concatenate(mixed, axis=0) * scale).astype(BF16)
    delta = jnp.dot(mixed, w_ref[...], preferred_element_type=F32).astype(o_ref.dtype)
    o_ref[...] = jnp.swapaxes(delta.reshape(tk1, n2, d), 0, 1)


def _fourier_tables(seq_len):
    n1 = 64
    n2 = seq_len // n1
    s2_, k1_, s1_ = np.arange(n2)[:, None, None], np.arange(n1)[None, :, None], np.arange(n1)[None, None, :]
    ang = 2.0 * np.pi * ((k1_ * s1_ * n2 + k1_ * s2_) % seq_len) / seq_len
    f1 = np.concatenate([np.cos(ang), -np.sin(ang)], axis=1)
    c2, s2 = _dft_cos_sin(n2)
    f2 = np.concatenate([c2, s2], axis=1)
    cc, sc = _dft_cos_sin(FGROUP_DIM)
    cs = np.block([[cc, -sc], [sc, cc]])
    return n1, n2, f1, f2, cs


def _fourier_layer(x, g, w_bf16, layer):
    b, s, d = x.shape
    n1, n2, f1, f2, cs = _fourier_tables(s)
    ts2 = 16
    tk1 = 32 // jnp.dtype(FOURIER_DELTA_DTYPE).itemsize
    f1 = jnp.asarray(f1, BF16)
    f2 = jnp.asarray(f2, BF16)
    cs = jnp.asarray(cs, BF16)
    g2 = g.reshape(1, d)

    z_shape = jax.ShapeDtypeStruct((b, n1, n2, d), BF16)
    zr, zi = pl.pallas_call(
        functools.partial(_fourier_a_kernel, n1=n1, ts2=ts2),
        grid=(b, n2 // ts2),
        in_specs=[
            pl.BlockSpec((None, n1, ts2, d), lambda bi, i: (bi, 0, i, 0)),
            pl.BlockSpec((1, d), lambda bi, i: (0, 0)),
            pl.BlockSpec((ts2, 2 * n1, n1), lambda bi, i: (i, 0, 0)),
            pl.BlockSpec((2 * FGROUP_DIM, 2 * FGROUP_DIM), lambda bi, i: (0, 0)),
        ],
        out_specs=[
            pl.BlockSpec((None, n1, ts2, d), lambda bi, i: (bi, 0, i, 0)),
            pl.BlockSpec((None, n1, ts2, d), lambda bi, i: (bi, 0, i, 0)),
        ],
        out_shape=[z_shape, z_shape],
        compiler_params=pltpu.CompilerParams(
            dimension_semantics=("parallel", "parallel"), vmem_limit_bytes=VMEM_LIMIT),
        name="fourier_a",
    )(x.reshape(b, n1, n2, d), g2, f1, cs)

    scale = float(1.0 / np.sqrt(s * FGROUP_DIM))
    delta = pl.pallas_call(
        functools.partial(_fourier_c_kernel, n2=n2, tk1=tk1, scale=scale),
        grid=(b, n1 // tk1),
        in_specs=[
            pl.BlockSpec((None, tk1, n2, d), lambda bi, i: (bi, i, 0, 0)),
            pl.BlockSpec((None, tk1, n2, d), lambda bi, i: (bi, i, 0, 0)),
            pl.BlockSpec((n2, 2 * n2), lambda bi, i: (0, 0)),
            _layer_spec(w_bf16, layer),
        ],
        out_specs=pl.BlockSpec((None, n2, tk1, d), lambda bi, i: (bi, 0, i, 0)),
        out_shape=jax.ShapeDtypeStruct((b, n2, n1, d), FOURIER_DELTA_DTYPE),
        compiler_params=pltpu.CompilerParams(
            dimension_semantics=("parallel", "parallel"), vmem_limit_bytes=VMEM_LIMIT),
        name="fourier_c",
    )(zr, zi, f2, w_bf16)
    return delta.reshape(b * s, d)


def _qkv_kernel(x_ref, g_ref, w_ref, qg_ref, kg_ref, cos_ref, sin_ref, q_ref, k_ref, vt_ref):
    h = _rms(x_ref[...], g_ref[...]).astype(BF16)
    y = jnp.dot(h, w_ref[...], preferred_element_type=F32)
    cos = cos_ref[...]
    sin = sin_ref[...]
    ones = jnp.ones((2 * HEAD_DIM, HEAD_DIM), BF16)

    def norm_rope(y_head, a_own, a_partner):
        sq = y_head * y_head
        hi = sq.astype(BF16)
        lo = (sq - hi.astype(F32)).astype(BF16)
        ss = jnp.dot(jnp.concatenate([hi, lo], axis=1), ones, preferred_element_type=F32)
        r = lax.rsqrt(ss * (1.0 / HEAD_DIM) + EPS)
        return (y_head * a_own + pltpu.roll(y_head, HEAD_DIM // 2, axis=1) * a_partner) * r

    qg = qg_ref[...] * ATTN_Q_SCALE
    kg = kg_ref[...]
    aq_own, aq_partner = cos * qg[0:1], sin * qg[1:2]
    ak_own, ak_partner = cos * kg[0:1], sin * kg[1:2]
    for hd in range(N_HEADS):
        sl = slice(hd * HEAD_DIM, (hd + 1) * HEAD_DIM)
        q_ref[:, sl] = norm_rope(y[:, sl], aq_own, aq_partner).astype(BF16)
    for hd in range(N_KV_HEADS):
        sl = slice(hd * HEAD_DIM, (hd + 1) * HEAD_DIM)
        so = slice(Q_DIM + hd * HEAD_DIM, Q_DIM + (hd + 1) * HEAD_DIM)
        k_ref[:, sl] = norm_rope(y[:, so], ak_own, ak_partner).astype(BF16)
    vt_ref[...] = y[:, Q_DIM + KV_DIM:].T.astype(BF16)


def _attn_kernel(q_ref, k_ref, vt_ref, o_ref, s0_ref, s1_ref, acc_ref, *, tq, tk):
    n_chunks = vt_ref.shape[0]
    n_items = (q_ref.shape[0] // tq) * n_chunks
    m_cols = GROUP * tq

    def scores(t, dst_ref):
        q_start = pl.multiple_of((t // n_chunks) * tq, tq)
        k_start = pl.multiple_of((t % n_chunks) * tk, tk)
        qb = q_ref[pl.ds(q_start, tq), :]
        q = jnp.concatenate([qb[:, gi * HEAD_DIM:(gi + 1) * HEAD_DIM] for gi in range(GROUP)], axis=0)
        kc = k_ref[pl.ds(k_start, tk), :]
        st = lax.dot_general(kc, q, (((1,), (1,)), ((), ())), preferred_element_type=F32)
        dst_ref[...] = st
        return jnp.max(st, axis=0, keepdims=True)

    def consume(t, src_ref, m, alpha, l):
        p = jnp.exp2(src_ref[...] - m)
        l = alpha * l + jnp.sum(p, axis=0, keepdims=True)
        pv = jnp.dot(vt_ref[t % n_chunks], p.astype(BF16), preferred_element_type=F32)
        acc_ref[...] = alpha * acc_ref[...] + pv
        return l

    def finalize(t, l):
        o_t = acc_ref[...] / l
        q_start = pl.multiple_of((t // n_chunks) * tq, tq)
        for gi in range(GROUP):
            o_ref[pl.ds(q_start, tq), gi * HEAD_DIM:(gi + 1) * HEAD_DIM] = (
                o_t[:, gi * tq:(gi + 1) * tq].T.astype(BF16))

    def step(t, cur_ref, nxt_ref, carry, maybe_last):
        m, alpha, l = carry
        cmax = scores(t + 1, nxt_ref)
        l = consume(t, cur_ref, m, alpha, l)
        if not maybe_last:
            m_next = jnp.maximum(m, cmax)
            return m_next, jnp.exp2(m - m_next), l
        last = (t % n_chunks) == n_chunks - 1
        pl.when(last)(lambda: finalize(t, l))
        m_next = jnp.where(last, cmax, jnp.maximum(m, cmax))
        alpha_next = jnp.where(last, 0.0, jnp.exp2(m - m_next))
        return m_next, alpha_next, l

    def body(tt, carry):
        carry = step(2 * tt, s0_ref, s1_ref, carry, False)
        return step(2 * tt + 1, s1_ref, s0_ref, carry, True)

    acc_ref[...] = jnp.zeros_like(acc_ref)
    m0 = scores(0, s0_ref)
    zero = jnp.zeros((1, m_cols), F32)
    carry = lax.fori_loop(0, n_items // 2 - 1, body, (m0, zero, zero))
    m, alpha, l = step(n_items - 2, s0_ref, s1_ref, carry, False)
    l = consume(n_items - 1, s1_ref, m, alpha, l)
    finalize(n_items - 1, l)


def _attn_bounded_kernel(q_ref, k_ref, vt_ref, o_ref, p0_ref, p1_ref, acc_ref, *, tq, tk):
    n_chunks = vt_ref.shape[0]
    n_items = (q_ref.shape[0] // tq) * n_chunks
    m_cols = GROUP * tq

    def probs(t, dst_ref):
        q_start = pl.multiple_of((t // n_chunks) * tq, tq)
        k_start = pl.multiple_of((t % n_chunks) * tk, tk)
        qb = q_ref[pl.ds(q_start, tq), :]
        q = jnp.concatenate([qb[:, gi * HEAD_DIM:(gi + 1) * HEAD_DIM] for gi in range(GROUP)], axis=0)
        kc = k_ref[pl.ds(k_start, tk), :]
        st = lax.dot_general(kc, q, (((1,), (1,)), ((), ())), preferred_element_type=F32)
        p = jnp.exp2(st)
        dst_ref[...] = p.astype(BF16)
        return jnp.sum(p, axis=0, keepdims=True)

    def accumulate(t, src_ref):
        acc_ref[...] += jnp.dot(vt_ref[t % n_chunks], src_ref[...], preferred_element_type=F32)

    def finalize(t, l):
        o_t = acc_ref[...] / l
        q_start = pl.multiple_of((t // n_chunks) * tq, tq)
        for gi in range(GROUP):
            o_ref[pl.ds(q_start, tq), gi * HEAD_DIM:(gi + 1) * HEAD_DIM] = (
                o_t[:, gi * tq:(gi + 1) * tq].T.astype(BF16))
        acc_ref[...] = jnp.zeros_like(acc_ref)

    def step(t, cur_ref, nxt_ref, carry, maybe_last):
        l, lsum = carry
        lsum_next = probs(t + 1, nxt_ref)
        accumulate(t, cur_ref)
        l = l + lsum
        if not maybe_last:
            return l, lsum_next
        last = (t % n_chunks) == n_chunks - 1
        pl.when(last)(lambda: finalize(t, l))
        return jnp.where(last, 0.0, l), lsum_next

    def body(tt, carry):
        carry = step(4 * tt, p0_ref, p1_ref, carry, False)
        carry = step(4 * tt + 1, p1_ref, p0_ref, carry, False)
        carry = step(4 * tt + 2, p0_ref, p1_ref, carry, False)
        return step(4 * tt + 3, p1_ref, p0_ref, carry, True)

    acc_ref[...] = jnp.zeros_like(acc_ref)
    lsum0 = probs(0, p0_ref)
    carry = lax.fori_loop(0, n_items // 4 - 1, body, (jnp.zeros((1, m_cols), F32), lsum0))
    carry = step(n_items - 4, p0_ref, p1_ref, carry, False)
    carry = step(n_items - 3, p1_ref, p0_ref, carry, False)
    l, lsum = step(n_items - 2, p0_ref, p1_ref, carry, False)
    accumulate(n_items - 1, p1_ref)
    finalize(n_items - 1, l + lsum)


def _rope_tables(seq_len):
    rows = seq_len // GRID_W
    row = jnp.repeat(jnp.arange(rows, dtype=F32), GRID_W)
    col = jnp.tile(jnp.arange(GRID_W, dtype=F32), rows)
    inv = ROPE_THETA ** (-jnp.arange(0, AXIS_ROT_DIM, 2, dtype=F32) / AXIS_ROT_DIM)
    ang = jnp.concatenate([row[:, None] * inv, col[:, None] * inv], axis=-1)
    cos, sin = jnp.cos(ang), jnp.sin(ang)
    return jnp.concatenate([cos, cos], axis=-1), jnp.concatenate([-sin, sin], axis=-1)


def _attention_qkv(x, g, w_qkv_bf16, layer, q_gain, k_gain):
    b, s, d = x.shape
    t = b * s
    tq = 256
    tk = 1024
    tm = 512
    per_chunk = tk // tm
    assert s % (4 * tk) == 0 and s % tq == 0, "attention loops take key chunks four at a time"
    cos, sin = _rope_tables(s)
    nblk = s // tm
    q, k, vt = pl.pallas_call(
        _qkv_kernel,
        grid=(t // tm,),
        in_specs=[
            pl.BlockSpec((tm, d), lambda i: (i, 0)),
            pl.BlockSpec((1, d), lambda i: (0, 0)),
            _layer_spec(w_qkv_bf16, layer),
            pl.BlockSpec((2, HEAD_DIM), lambda i: (0, 0)),
            pl.BlockSpec((2, HEAD_DIM), lambda i: (0, 0)),
            pl.BlockSpec((tm, HEAD_DIM), lambda i: (i % nblk, 0)),
            pl.BlockSpec((tm, HEAD_DIM), lambda i: (i % nblk, 0)),
        ],
        out_specs=[
            pl.BlockSpec((tm, Q_DIM), lambda i: (i, 0)),
            pl.BlockSpec((tm, KV_DIM), lambda i: (i, 0)),
            pl.BlockSpec((None, KV_DIM, tm), lambda i: (i // per_chunk, 0, i % per_chunk)),
        ],
        out_shape=[
            jax.ShapeDtypeStruct((t, Q_DIM), BF16),
            jax.ShapeDtypeStruct((t, KV_DIM), BF16),
            jax.ShapeDtypeStruct((t // tk, KV_DIM, tk), BF16),
        ],
        compiler_params=pltpu.CompilerParams(
            dimension_semantics=("parallel",), vmem_limit_bytes=VMEM_LIMIT),
        name="qkv_proj",
    )(x.reshape(t, d), g.reshape(1, d), w_qkv_bf16, q_gain, k_gain, cos, sin)

    def attention_call(body, buf_dtype, name):
        return pl.pallas_call(
            functools.partial(body, tq=tq, tk=tk),
            grid=(b, N_KV_HEADS),
            in_specs=[
                pl.BlockSpec((None, s, GROUP * HEAD_DIM), lambda bi, hi: (bi, 0, hi),
                             pipeline_mode=pl.Buffered(1)),
                pl.BlockSpec((None, s, HEAD_DIM), lambda bi, hi: (bi, 0, hi)),
                pl.BlockSpec((None, s // tk, HEAD_DIM, tk), lambda bi, hi: (bi, 0, hi, 0)),
            ],
            out_specs=pl.BlockSpec((None, s, GROUP * HEAD_DIM), lambda bi, hi: (bi, 0, hi)),
            out_shape=jax.ShapeDtypeStruct((b, s, Q_DIM), BF16),
            scratch_shapes=[
                pltpu.VMEM((tk, GROUP * tq), buf_dtype),
                pltpu.VMEM((tk, GROUP * tq), buf_dtype),
                pltpu.VMEM((HEAD_DIM, GROUP * tq), F32),
            ],
            compiler_params=pltpu.CompilerParams(
                dimension_semantics=("parallel", "parallel"), vmem_limit_bytes=VMEM_LIMIT),
            name=name,
        )

    score_bound = (HEAD_DIM * ATTN_Q_SCALE) * jnp.max(jnp.abs(q_gain)) * jnp.max(jnp.abs(k_gain))
    o = lax.cond(
        score_bound <= ATTN_BOUNDED_MAX_LOG2_SCORE,
        attention_call(_attn_bounded_kernel, BF16, "attention_bounded"),
        attention_call(_attn_kernel, F32, "attention"),
        q.reshape(b, s, Q_DIM), k.reshape(b, s, KV_DIM), vt.reshape(b, s // tk, KV_DIM, tk))
    return o.reshape(t, Q_DIM)


def _ffn_kernel(*refs, with_attn):
    if with_attn:
        x_ref, a_ref, wo_ref, g_ref, wg_ref, wu_ref, wd_ref, o_ref = refs
        x = x_ref[...] + jnp.dot(a_ref[...], wo_ref[...], preferred_element_type=F32)
    else:
        x_ref, delta_ref, g_ref, wg_ref, wu_ref, wd_ref, o_ref = refs
        x = x_ref[...] + delta_ref[...].astype(F32)
    h = _rms(x, g_ref[...]).astype(BF16)
    gate = jnp.dot(h, wg_ref[...], preferred_element_type=F32)
    up = jnp.dot(h, wu_ref[...], preferred_element_type=F32)
    act = (gate * jax.nn.sigmoid(gate) * up).astype(BF16)
    o_ref[...] = x + jnp.dot(act, wd_ref[...], preferred_element_type=F32)


def _ffn_layer(x2d, g, layer, wg, wu, wd, attn=None, delta=None):
    assert (attn is None) != (delta is None)
    t, d = x2d.shape
    tm = 512
    resident = functools.partial(pl.BlockSpec, index_map=lambda i: (0, 0), pipeline_mode=pl.Buffered(1))
    row = lambda width: pl.BlockSpec((tm, width), lambda i: (i, 0))
    in_specs = [row(d)]
    args = [x2d]
    if attn is not None:
        a2d, wo, attn_layer = attn
        in_specs += [row(Q_DIM), _layer_spec(wo, attn_layer, single_buffer=True)]
        args += [a2d, wo]
    else:
        in_specs += [row(d)]
        args += [delta]
    in_specs += [resident((1, d))] + [_layer_spec(w, layer, single_buffer=True) for w in (wg, wu, wd)]
    args += [g.reshape(1, d), wg, wu, wd]
    return pl.pallas_call(
        functools.partial(_ffn_kernel, with_attn=attn is not None),
        grid=(t // tm,),
        in_specs=in_specs,
        out_specs=row(d),
        out_shape=jax.ShapeDtypeStruct((t, d), F32),
        compiler_params=pltpu.CompilerParams(
            dimension_semantics=("parallel",), vmem_limit_bytes=VMEM_LIMIT),
        name="swiglu",
    )(*args)


def _pair_split_perm():
    own = np.concatenate([np.arange(0, HEAD_DIM, 2), np.arange(1, HEAD_DIM, 2)])
    partner = np.roll(own, HEAD_DIM // 2)
    cols = [h * HEAD_DIM + own for h in range(N_HEADS + N_KV_HEADS)]
    cols.append(np.arange(Q_DIM + KV_DIM, QKV_DIM))
    return np.stack([own, partner]), np.concatenate(cols)


def kernel(x_prompt, x_sample, norm_mix, norm_ffn, fourier_w, attn_w_qkv, attn_q_norm, attn_k_norm,
           attn_w_o, ffn_w_gate, ffn_w_up, ffn_w_down):
    depth = norm_mix.shape[0]
    head_perm, col_perm = _pair_split_perm()
    fw = fourier_w.astype(BF16)
    wqkv = attn_w_qkv[:, :, col_perm].astype(BF16)
    qn = attn_q_norm[:, head_perm]
    kn = attn_k_norm[:, head_perm]
    wo = attn_w_o.astype(BF16)
    wg = ffn_w_gate.astype(BF16)
    wu = ffn_w_up.astype(BF16)
    wd = ffn_w_down.astype(BF16)

    def run_trunk(x):
        b, s, d = x.shape
        for i in range(depth):
            j = i // 2
            if i % 2 == 0:
                delta = _fourier_layer(x, norm_mix[i], fw, j)
                x2d = _ffn_layer(x.reshape(b * s, d), norm_ffn[i], i, wg, wu, wd, delta=delta)
            else:
                a2d = _attention_qkv(x, norm_mix[i], wqkv, j, qn[j], kn[j])
                x2d = _ffn_layer(x.reshape(b * s, d), norm_ffn[i], i, wg, wu, wd, attn=(a2d, wo, j))
            x = x2d.reshape(b, s, d)
        return x

    return (run_trunk(x_prompt), run_trunk(x_sample))
```

```python
import functools

import numpy as np
import jax
import jax.numpy as jnp
from jax import lax
from jax.experimental import pallas as pl
from jax.experimental.pallas import tpu as pltpu

D_MODEL = 1024
HEAD_DIM = 128
N_HEADS = D_MODEL // HEAD_DIM
N_KV_HEADS = 2
GROUP = N_HEADS // N_KV_HEADS
Q_DIM = N_HEADS * HEAD_DIM
KV_DIM = N_KV_HEADS * HEAD_DIM
QKV_DIM = Q_DIM + 2 * KV_DIM
GRID_W = 64
AXIS_ROT_DIM = HEAD_DIM // 2
ROPE_THETA = 10000.0
N_FGROUPS = 8
FGROUP_DIM = D_MODEL // N_FGROUPS
EPS = 1e-6

ATTN_Q_SCALE = float(HEAD_DIM ** -0.5 * np.log2(np.e))
ATTN_BOUNDED_MAX_LOG2_SCORE = 60.0

LANES = 128
VMEM_LIMIT = 56 * 1024 * 1024

F32 = jnp.float32
BF16 = jnp.bfloat16
FOURIER_DELTA_DTYPE = BF16


def _rms(x, g):
    ms = jnp.mean(x * x, axis=-1, keepdims=True)
    return x * lax.rsqrt(ms + EPS) * g


def _layer_spec(stacked, layer, single_buffer=False):
    zeros = (0,) * (stacked.ndim - 1)
    mode = dict(pipeline_mode=pl.Buffered(1)) if single_buffer else {}
    return pl.BlockSpec((None,) + stacked.shape[1:], lambda *_: (layer,) + zeros, **mode)


def _dft_cos_sin(n):
    k = np.arange(n)
    ang = 2.0 * np.pi * ((k[:, None] * k[None, :]) % n) / n
    return np.cos(ang), np.sin(ang)


def _fourier_a_kernel(x_ref, g_ref, f1_ref, cs_ref, yr_ref, yi_ref, *, n1, ts2):
    d = D_MODEL
    h = _rms(x_ref[...], g_ref[...])
    ht = jnp.swapaxes(h.astype(BF16), 0, 1)
    z = [jnp.dot(f1_ref[j], ht[j], preferred_element_type=F32) for j in range(ts2)]
    zr = jnp.concatenate([zj[:n1] for zj in z], axis=0).astype(BF16)
    zi = jnp.concatenate([zj[n1:] for zj in z], axis=0).astype(BF16)
    cs = cs_ref[...]
    yr, yi = [], []
    for gi in range(N_FGROUPS):
        sl = slice(gi * FGROUP_DIM, (gi + 1) * FGROUP_DIM)
        y = jnp.dot(jnp.concatenate([zr[:, sl], zi[:, sl]], axis=1), cs, preferred_element_type=F32)
        yr.append(y[:, :FGROUP_DIM])
        yi.append(y[:, FGROUP_DIM:])
    yr = jnp.concatenate(yr, axis=1).astype(BF16).reshape(ts2, n1, d)
    yi = jnp.concatenate(yi, axis=1).astype(BF16).reshape(ts2, n1, d)
    yr_ref[...] = jnp.swapaxes(yr, 0, 1)
    yi_ref[...] = jnp.swapaxes(yi, 0, 1)


def _fourier_c_kernel(yr_ref, yi_ref, f2_ref, w_ref, o_ref, *, n2, tk1, scale):
    d = D_MODEL
    f2 = f2_ref[...]
    mixed = [jnp.dot(f2, jnp.concatenate([yr_ref[j], yi_ref[j]], axis=0), preferred_element_type=F32)
             for j in range(tk1)]
    mixed = (jnp.concatenate(mixed, axis=0) * scale).astype(BF16)
    delta = jnp.dot(mixed, w_ref[...], preferred_element_type=F32).astype(o_ref.dtype)
    o_ref[...] = jnp.swapaxes(delta.reshape(tk1, n2, d), 0, 1)


def _fourier_tables(seq_len):
    n1 = 64
    n2 = seq_len // n1
    s2_, k1_, s1_ = np.arange(n2)[:, None, None], np.arange(n1)[None, :, None], np.arange(n1)[None, None, :]
    ang = 2.0 * np.pi * ((k1_ * s1_ * n2 + k1_ * s2_) % seq_len) / seq_len
    f1 = np.concatenate([np.cos(ang), -np.sin(ang)], axis=1)
    c2, s2 = _dft_cos_sin(n2)
    f2 = np.concatenate([c2, s2], axis=1)
    cc, sc = _dft_cos_sin(FGROUP_DIM)
    cs = np.block([[cc, -sc], [sc, cc]])
    return n1, n2, f1, f2, cs


def _fourier_layer(x, g, w_bf16, layer):
    b, s, d = x.shape
    n1, n2, f1, f2, cs = _fourier_tables(s)
    ts2 = 16
    tk1 = 32 // jnp.dtype(FOURIER_DELTA_DTYPE).itemsize
    f1 = jnp.asarray(f1, BF16)
    f2 = jnp.asarray(f2, BF16)
    cs = jnp.asarray(cs, BF16)
    g2 = g.reshape(1, d)

    z_shape = jax.ShapeDtypeStruct((b, n1, n2, d), BF16)
    zr, zi = pl.pallas_call(
        functools.partial(_fourier_a_kernel, n1=n1, ts2=ts2),
        grid=(b, n2 // ts2),
        in_specs=[
            pl.BlockSpec((None, n1, ts2, d), lambda bi, i: (bi, 0, i, 0)),
            pl.BlockSpec((1, d), lambda bi, i: (0, 0)),
            pl.BlockSpec((ts2, 2 * n1, n1), lambda bi, i: (i, 0, 0)),
            pl.BlockSpec((2 * FGROUP_DIM, 2 * FGROUP_DIM), lambda bi, i: (0, 0)),
        ],
        out_specs=[
            pl.BlockSpec((None, n1, ts2, d), lambda bi, i: (bi, 0, i, 0)),
            pl.BlockSpec((None, n1, ts2, d), lambda bi, i: (bi, 0, i, 0)),
        ],
        out_shape=[z_shape, z_shape],
        compiler_params=pltpu.CompilerParams(
            dimension_semantics=("parallel", "parallel"), vmem_limit_bytes=VMEM_LIMIT),
        name="fourier_a",
    )(x.reshape(b, n1, n2, d), g2, f1, cs)

    scale = float(1.0 / np.sqrt(s * FGROUP_DIM))
    delta = pl.pallas_call(
        functools.partial(_fourier_c_kernel, n2=n2, tk1=tk1, scale=scale),
        grid=(b, n1 // tk1),
        in_specs=[
            pl.BlockSpec((None, tk1, n2, d), lambda bi, i: (bi, i, 0, 0)),
            pl.BlockSpec((None, tk1, n2, d), lambda bi, i: (bi, i, 0, 0)),
            pl.BlockSpec((n2, 2 * n2), lambda bi, i: (0, 0)),
            _layer_spec(w_bf16, layer),
        ],
        out_specs=pl.BlockSpec((None, n2, tk1, d), lambda bi, i: (bi, 0, i, 0)),
        out_shape=jax.ShapeDtypeStruct((b, n2, n1, d), FOURIER_DELTA_DTYPE),
        compiler_params=pltpu.CompilerParams(
            dimension_semantics=("parallel", "parallel"), vmem_limit_bytes=VMEM_LIMIT),
        name="fourier_c",
    )(zr, zi, f2, w_bf16)
    return delta.reshape(b * s, d)


def _qkv_kernel(x_ref, g_ref, w_ref, qg_ref, kg_ref, cos_ref, sin_ref, q_ref, k_ref, vt_ref):
    h = _rms(x_ref[...], g_ref[...]).astype(BF16)
    y = jnp.dot(h, w_ref[...], preferred_element_type=F32)
    cos = cos_ref[...]
    sin = sin_ref[...]
    ones = jnp.ones((2 * HEAD_DIM, HEAD_DIM), BF16)
    even_lane = (lax.broadcasted_iota(jnp.int32, (1, HEAD_DIM), 1) % 2) == 0

    def norm_rope(y_head, a_own, a_partner):
        sq = y_head * y_head
        hi = sq.astype(BF16)
        lo = (sq - hi.astype(F32)).astype(BF16)
        ss = jnp.dot(jnp.concatenate([hi, lo], axis=1), ones, preferred_element_type=F32)
        r = lax.rsqrt(ss * (1.0 / HEAD_DIM) + EPS)
        partner = jnp.where(even_lane, pltpu.roll(y_head, HEAD_DIM - 1, axis=1),
                            pltpu.roll(y_head, 1, axis=1))
        return (y_head * a_own + partner * a_partner) * r

    qg = qg_ref[...] * ATTN_Q_SCALE
    kg = kg_ref[...]
    aq_own, aq_partner = cos * qg[0:1], sin * qg[1:2]
    ak_own, ak_partner = cos * kg[0:1], sin * kg[1:2]
    for hd in range(N_HEADS):
        sl = slice(hd * HEAD_DIM, (hd + 1) * HEAD_DIM)
        q_ref[:, sl] = norm_rope(y[:, sl], aq_own, aq_partner).astype(BF16)
    for hd in range(N_KV_HEADS):
        sl = slice(hd * HEAD_DIM, (hd + 1) * HEAD_DIM)
        so = slice(Q_DIM + hd * HEAD_DIM, Q_DIM + (hd + 1) * HEAD_DIM)
        k_ref[:, sl] = norm_rope(y[:, so], ak_own, ak_partner).astype(BF16)
    vt_ref[...] = y[:, Q_DIM + KV_DIM:].T.astype(BF16)


def _attn_kernel(q_ref, k_ref, vt_ref, o_ref, s0_ref, s1_ref, acc_ref, *, tq, tk):
    n_chunks = vt_ref.shape[0]
    n_items = (q_ref.shape[0] // tq) * n_chunks
    m_cols = GROUP * tq

    def scores(t, dst_ref):
        q_start = pl.multiple_of((t // n_chunks) * tq, tq)
        k_start = pl.multiple_of((t % n_chunks) * tk, tk)
        qb = q_ref[pl.ds(q_start, tq), :]
        q = jnp.concatenate([qb[:, gi * HEAD_DIM:(gi + 1) * HEAD_DIM] for gi in range(GROUP)], axis=0)
        kc = k_ref[pl.ds(k_start, tk), :]
        st = lax.dot_general(kc, q, (((1,), (1,)), ((), ())), preferred_element_type=F32)
        dst_ref[...] = st
        return jnp.max(st, axis=0, keepdims=True)

    def consume(t, src_ref, m, alpha, l):
        p = jnp.exp2(src_ref[...] - m)
        l = alpha * l + jnp.sum(p, axis=0, keepdims=True)
        pv = jnp.dot(vt_ref[t % n_chunks], p.astype(BF16), preferred_element_type=F32)
        acc_ref[...] = alpha * acc_ref[...] + pv
        return l

    def finalize(t, l):
        o_t = acc_ref[...] / l
        q_start = pl.multiple_of((t // n_chunks) * tq, tq)
        for gi in range(GROUP):
            o_ref[pl.ds(q_start, tq), gi * HEAD_DIM:(gi + 1) * HEAD_DIM] = (
                o_t[:, gi * tq:(gi + 1) * tq].T.astype(BF16))

    def step(t, cur_ref, nxt_ref, carry, maybe_last):
        m, alpha, l = carry
        cmax = scores(t + 1, nxt_ref)
        l = consume(t, cur_ref, m, alpha, l)
        if not maybe_last:
            m_next = jnp.maximum(m, cmax)
            return m_next, jnp.exp2(m - m_next), l
        last = (t % n_chunks) == n_chunks - 1
        pl.when(last)(lambda: finalize(t, l))
        m_next = jnp.where(last, cmax, jnp.maximum(m, cmax))
        alpha_next = jnp.where(last, 0.0, jnp.exp2(m - m_next))
        return m_next, alpha_next, l

    def body(tt, carry):
        carry = step(2 * tt, s0_ref, s1_ref, carry, False)
        return step(2 * tt + 1, s1_ref, s0_ref, carry, True)

    acc_ref[...] = jnp.zeros_like(acc_ref)
    m0 = scores(0, s0_ref)
    zero = jnp.zeros((1, m_cols), F32)
    carry = lax.fori_loop(0, n_items // 2 - 1, body, (m0, zero, zero))
    m, alpha, l = step(n_items - 2, s0_ref, s1_ref, carry, False)
    l = consume(n_items - 1, s1_ref, m, alpha, l)
    finalize(n_items - 1, l)


def _attn_bounded_kernel(q_ref, k_ref, vt_ref, o_ref, p0_ref, p1_ref, acc_ref, *, tq, tk):
    n_chunks = vt_ref.shape[0]
    n_items = (q_ref.shape[0] // tq) * n_chunks
    m_cols = GROUP * tq

    def probs(t, dst_ref):
        q_start = pl.multiple_of((t // n_chunks) * tq, tq)
        k_start = pl.multiple_of((t % n_chunks) * tk, tk)
        qb = q_ref[pl.ds(q_start, tq), :]
        q = jnp.concatenate([qb[:, gi * HEAD_DIM:(gi + 1) * HEAD_DIM] for gi in range(GROUP)], axis=0)
        kc = k_ref[pl.ds(k_start, tk), :]
        st = lax.dot_general(kc, q, (((1,), (1,)), ((), ())), preferred_element_type=F32)
        p = jnp.exp2(st)
        dst_ref[...] = p.astype(BF16)
        return jnp.sum(p, axis=0, keepdims=True)

    def accumulate(t, src_ref):
        acc_ref[...] += jnp.dot(vt_ref[t % n_chunks], src_ref[...], preferred_element_type=F32)

    def finalize(t, l):
        o_t = acc_ref[...] / l
        q_start = pl.multiple_of((t // n_chunks) * tq, tq)
        for gi in range(GROUP):
            o_ref[pl.ds(q_start, tq), gi * HEAD_DIM:(gi + 1) * HEAD_DIM] = (
                o_t[:, gi * tq:(gi + 1) * tq].T.astype(BF16))
        acc_ref[...] = jnp.zeros_like(acc_ref)

    def step(t, cur_ref, nxt_ref, carry, maybe_last):
        l, lsum = carry
        lsum_next = probs(t + 1, nxt_ref)
        accumulate(t, cur_ref)
        l = l + lsum
        if not maybe_last:
            return l, lsum_next
        last = (t % n_chunks) == n_chunks - 1
        pl.when(last)(lambda: finalize(t, l))
        return jnp.where(last, 0.0, l), lsum_next

    def body(tt, carry):
        carry = step(4 * tt, p0_ref, p1_ref, carry, False)
        carry = step(4 * tt + 1, p1_ref, p0_ref, carry, False)
        carry = step(4 * tt + 2, p0_ref, p1_ref, carry, False)
        return step(4 * tt + 3, p1_ref, p0_ref, carry, True)

    acc_ref[...] = jnp.zeros_like(acc_ref)
    lsum0 = probs(0, p0_ref)
    carry = lax.fori_loop(0, n_items // 4 - 1, body, (jnp.zeros((1, m_cols), F32), lsum0))
    carry = step(n_items - 4, p0_ref, p1_ref, carry, False)
    carry = step(n_items - 3, p1_ref, p0_ref, carry, False)
    l, lsum = step(n_items - 2, p0_ref, p1_ref, carry, False)
    accumulate(n_items - 1, p1_ref)
    finalize(n_items - 1, l + lsum)


def _rope_tables(seq_len):
    f32 = np.float32
    rows = seq_len // GRID_W
    row = np.repeat(np.arange(rows, dtype=f32), GRID_W)
    col = np.tile(np.arange(GRID_W, dtype=f32), rows)
    inv = (f32(ROPE_THETA) ** (-np.arange(0, AXIS_ROT_DIM, 2, dtype=f32) / f32(AXIS_ROT_DIM))).astype(f32)
    ang = np.concatenate([row[:, None] * inv, col[:, None] * inv], axis=-1)
    cos = np.repeat(np.cos(ang), 2, axis=-1)
    sin = np.repeat(np.sin(ang), 2, axis=-1) * np.tile(np.array([-1.0, 1.0], f32), HEAD_DIM // 2)
    return jnp.asarray(cos, F32), jnp.asarray(sin, F32)


def _attention_qkv(x, g, w_qkv_bf16, layer, q_gain, k_gain):
    b, s, d = x.shape
    t = b * s
    tq = 256
    tk = 1024
    tm = 512
    per_chunk = tk // tm
    assert s % (4 * tk) == 0 and s % tq == 0, "attention loops take key chunks four at a time"
    cos, sin = _rope_tables(s)
    nblk = s // tm
    q, k, vt = pl.pallas_call(
        _qkv_kernel,
        grid=(t // tm,),
        in_specs=[
            pl.BlockSpec((tm, d), lambda i: (i, 0)),
            pl.BlockSpec((1, d), lambda i: (0, 0)),
            _layer_spec(w_qkv_bf16, layer),
            pl.BlockSpec((2, HEAD_DIM), lambda i: (0, 0)),
            pl.BlockSpec((2, HEAD_DIM), lambda i: (0, 0)),
            pl.BlockSpec((tm, HEAD_DIM), lambda i: (i % nblk, 0)),
            pl.BlockSpec((tm, HEAD_DIM), lambda i: (i % nblk, 0)),
        ],
        out_specs=[
            pl.BlockSpec((tm, Q_DIM), lambda i: (i, 0)),
            pl.BlockSpec((tm, KV_DIM), lambda i: (i, 0)),
            pl.BlockSpec((None, KV_DIM, tm), lambda i: (i // per_chunk, 0, i % per_chunk)),
        ],
        out_shape=[
            jax.ShapeDtypeStruct((t, Q_DIM), BF16),
            jax.ShapeDtypeStruct((t, KV_DIM), BF16),
            jax.ShapeDtypeStruct((t // tk, KV_DIM, tk), BF16),
        ],
        compiler_params=pltpu.CompilerParams(
            dimension_semantics=("parallel",), vmem_limit_bytes=VMEM_LIMIT),
        name="qkv_proj",
    )(x.reshape(t, d), g.reshape(1, d), w_qkv_bf16, q_gain, k_gain, cos, sin)

    def attention_call(body, buf_dtype, name):
        return pl.pallas_call(
            functools.partial(body, tq=tq, tk=tk),
            grid=(b, N_KV_HEADS),
            in_specs=[
                pl.BlockSpec((None, s, GROUP * HEAD_DIM), lambda bi, hi: (bi, 0, hi),
                             pipeline_mode=pl.Buffered(1)),
                pl.BlockSpec((None, s, HEAD_DIM), lambda bi, hi: (bi, 0, hi)),
                pl.BlockSpec((None, s // tk, HEAD_DIM, tk), lambda bi, hi: (bi, 0, hi, 0)),
            ],
            out_specs=pl.BlockSpec((None, s, GROUP * HEAD_DIM), lambda bi, hi: (bi, 0, hi)),
            out_shape=jax.ShapeDtypeStruct((b, s, Q_DIM), BF16),
            scratch_shapes=[
                pltpu.VMEM((tk, GROUP * tq), buf_dtype),
                pltpu.VMEM((tk, GROUP * tq), buf_dtype),
                pltpu.VMEM((HEAD_DIM, GROUP * tq), F32),
            ],
            compiler_params=pltpu.CompilerParams(
                dimension_semantics=("parallel", "parallel"), vmem_limit_bytes=VMEM_LIMIT),
            name=name,
        )

    score_bound = (HEAD_DIM * ATTN_Q_SCALE) * jnp.max(jnp.abs(q_gain)) * jnp.max(jnp.abs(k_gain))
    o = lax.cond(
        score_bound <= ATTN_BOUNDED_MAX_LOG2_SCORE,
        attention_call(_attn_bounded_kernel, BF16, "attention_bounded"),
        attention_call(_attn_kernel, F32, "attention"),
        q.reshape(b, s, Q_DIM), k.reshape(b, s, KV_DIM), vt.reshape(b, s // tk, KV_DIM, tk))
    return o.reshape(t, Q_DIM)


def _ffn_kernel(*refs, with_attn):
    if with_attn:
        x_ref, a_ref, wo_ref, g_ref, wg_ref, wu_ref, wd_ref, o_ref = refs
        x = x_ref[...] + jnp.dot(a_ref[...], wo_ref[...], preferred_element_type=F32)
    else:
        x_ref, delta_ref, g_ref, wg_ref, wu_ref, wd_ref, o_ref = refs
        x = x_ref[...] + delta_ref[...].astype(F32)
    h = _rms(x, g_ref[...]).astype(BF16)
    gate = jnp.dot(h, wg_ref[...], preferred_element_type=F32)
    up = jnp.dot(h, wu_ref[...], preferred_element_type=F32)
    act = (gate * jax.nn.sigmoid(gate) * up).astype(BF16)
    o_ref[...] = x + jnp.dot(act, wd_ref[...], preferred_element_type=F32)


def _ffn_layer(x2d, g, layer, wg, wu, wd, attn=None, delta=None):
    assert (attn is None) != (delta is None)
    t, d = x2d.shape
    tm = 512
    resident = functools.partial(pl.BlockSpec, index_map=lambda i: (0, 0), pipeline_mode=pl.Buffered(1))
    row = lambda width: pl.BlockSpec((tm, width), lambda i: (i, 0))
    in_specs = [row(d)]
    args = [x2d]
    if attn is not None:
        a2d, wo, attn_layer = attn
        in_specs += [row(Q_DIM), _layer_spec(wo, attn_layer, single_buffer=True)]
        args += [a2d, wo]
    else:
        in_specs += [row(d)]
        args += [delta]
    in_specs += [resident((1, d))] + [_layer_spec(w, layer, single_buffer=True) for w in (wg, wu, wd)]
    args += [g.reshape(1, d), wg, wu, wd]
    return pl.pallas_call(
        functools.partial(_ffn_kernel, with_attn=attn is not None),
        grid=(t // tm,),
        in_specs=in_specs,
        out_specs=row(d),
        out_shape=jax.ShapeDtypeStruct((t, d), F32),
        compiler_params=pltpu.CompilerParams(
            dimension_semantics=("parallel",), vmem_limit_bytes=VMEM_LIMIT),
        name="swiglu",
    )(*args)


def _with_pair_partner(gain):
    n_layers = gain.shape[0]
    partner = gain.reshape(n_layers, HEAD_DIM // 2, 2)[:, :, ::-1].reshape(n_layers, HEAD_DIM)
    return jnp.stack([gain, partner], axis=1)


def kernel(x_prompt, x_sample, norm_mix, norm_ffn, fourier_w, attn_w_qkv, attn_q_norm, attn_k_norm,
           attn_w_o, ffn_w_gate, ffn_w_up, ffn_w_down):
    depth = norm_mix.shape[0]
    fw = fourier_w.astype(BF16)
    wqkv = attn_w_qkv.astype(BF16)
    qn = _with_pair_partner(attn_q_norm)
    kn = _with_pair_partner(attn_k_norm)
    wo = attn_w_o.astype(BF16)
    wg = ffn_w_gate.astype(BF16)
    wu = ffn_w_up.astype(BF16)
    wd = ffn_w_down.astype(BF16)

    def run_trunk(x):
        b, s, d = x.shape
        for i in range(depth):
            j = i // 2
            if i % 2 == 0:
                delta = _fourier_layer(x, norm_mix[i], fw, j)
                x2d = _ffn_layer(x.reshape(b * s, d), norm_ffn[i], i, wg, wu, wd, delta=delta)
            else:
                a2d = _attention_qkv(x, norm_mix[i], wqkv, j, qn[j], kn[j])
                x2d = _ffn_layer(x.reshape(b * s, d), norm_ffn[i], i, wg, wu, wd, attn=(a2d, wo, j))
            x = x2d.reshape(b, s, d)
        return x

    return (run_trunk(x_prompt), run_trunk(x_sample))
```

```python
import functools

import numpy as np
import jax
import jax.numpy as jnp
from jax import lax
from jax.experimental import pallas as pl
from jax.experimental.pallas import tpu as pltpu

D_MODEL = 1024
HEAD_DIM = 128
N_HEADS = D_MODEL // HEAD_DIM
N_KV_HEADS = 2
GROUP = N_HEADS // N_KV_HEADS
Q_DIM = N_HEADS * HEAD_DIM
KV_DIM = N_KV_HEADS * HEAD_DIM
QKV_DIM = Q_DIM + 2 * KV_DIM
GRID_W = 64
AXIS_ROT_DIM = HEAD_DIM // 2
ROPE_THETA = 10000.0
N_FGROUPS = 8
FGROUP_DIM = D_MODEL // N_FGROUPS
EPS = 1e-6

ATTN_Q_SCALE = float(HEAD_DIM ** -0.5 * np.log2(np.e))
ATTN_BOUNDED_MAX_LOG2_SCORE = 60.0

LANES = 128
VMEM_LIMIT = 56 * 1024 * 1024

F32 = jnp.float32
BF16 = jnp.bfloat16
FOURIER_DELTA_DTYPE = BF16


def _rms(x, g):
    ms = jnp.mean(x * x, axis=-1, keepdims=True)
    return x * lax.rsqrt(ms + EPS) * g


def _layer_spec(stacked, layer, single_buffer=False):
    zeros = (0,) * (stacked.ndim - 1)
    mode = dict(pipeline_mode=pl.Buffered(1)) if single_buffer else {}
    return pl.BlockSpec((None,) + stacked.shape[1:], lambda *_: (layer,) + zeros, **mode)


def _dft_cos_sin(n):
    k = np.arange(n)
    ang = 2.0 * np.pi * ((k[:, None] * k[None, :]) % n) / n
    return np.cos(ang), np.sin(ang)


def _fourier_a_kernel(x_ref, g_ref, f1_ref, cs_ref, yr_ref, yi_ref, *, n1, ts2):
    d = D_MODEL
    h = _rms(x_ref[...], g_ref[...])
    ht = jnp.swapaxes(h.astype(BF16), 0, 1)
    z = [jnp.dot(f1_ref[j], ht[j], preferred_element_type=F32) for j in range(ts2)]
    zr = jnp.concatenate([zj[:n1] for zj in z], axis=0).astype(BF16)
    zi = jnp.concatenate([zj[n1:] for zj in z], axis=0).astype(BF16)
    cs = cs_ref[...]
    yr, yi = [], []
    for gi in range(N_FGROUPS):
        sl = slice(gi * FGROUP_DIM, (gi + 1) * FGROUP_DIM)
        y = jnp.dot(jnp.concatenate([zr[:, sl], zi[:, sl]], axis=1), cs, preferred_element_type=F32)
        yr.append(y[:, :FGROUP_DIM])
        yi.append(y[:, FGROUP_DIM:])
    yr = jnp.concatenate(yr, axis=1).astype(BF16).reshape(ts2, n1, d)
    yi = jnp.concatenate(yi, axis=1).astype(BF16).reshape(ts2, n1, d)
    yr_ref[...] = jnp.swapaxes(yr, 0, 1)
    yi_ref[...] = jnp.swapaxes(yi, 0, 1)


def _fourier_c_kernel(yr_ref, yi_ref, f2_ref, w_ref, o_ref, *, n2, tk1, scale):
    d = D_MODEL
    f2 = f2_ref[...]
    mixed = [jnp.dot(f2, jnp.concatenate([yr_ref[j], yi_ref[j]], axis=0), preferred_element_type=F32)
             for j in range(tk1)]
    mixed = (jnp.concatenate(mixed, axis=0) * scale).astype(BF16)
    delta = jnp.dot(mixed, w_ref[...], preferred_element_type=F32).astype(o_ref.dtype)
    o_ref[...] = jnp.swapaxes(delta.reshape(tk1, n2, d), 0, 1)


def _fourier_tables(seq_len):
    n1 = 64
    n2 = seq_len // n1
    s2_, k1_, s1_ = np.arange(n2)[:, None, None], np.arange(n1)[None, :, None], np.arange(n1)[None, None, :]
    ang = 2.0 * np.pi * ((k1_ * s1_ * n2 + k1_ * s2_) % seq_len) / seq_len
    f1 = np.concatenate([np.cos(ang), -np.sin(ang)], axis=1)
    c2, s2 = _dft_cos_sin(n2)
    f2 = np.concatenate([c2, s2], axis=1)
    cc, sc = _dft_cos_sin(FGROUP_DIM)
    cs = np.block([[cc, -sc], [sc, cc]])
    return n1, n2, f1, f2, cs


def _fourier_layer(x, g, w_bf16, layer):
    b, s, d = x.shape
    n1, n2, f1, f2, cs = _fourier_tables(s)
    ts2 = 16
    tk1 = 32 // jnp.dtype(FOURIER_DELTA_DTYPE).itemsize
    f1 = jnp.asarray(f1, BF16)
    f2 = jnp.asarray(f2, BF16)
    cs = jnp.asarray(cs, BF16)
    g2 = g.reshape(1, d)

    z_shape = jax.ShapeDtypeStruct((b, n1, n2, d), BF16)
    zr, zi = pl.pallas_call(
        functools.partial(_fourier_a_kernel, n1=n1, ts2=ts2),
        grid=(b, n2 // ts2),
        in_specs=[
            pl.BlockSpec((None, n1, ts2, d), lambda bi, i: (bi, 0, i, 0)),
            pl.BlockSpec((1, d), lambda bi, i: (0, 0)),
            pl.BlockSpec((ts2, 2 * n1, n1), lambda bi, i: (i, 0, 0)),
            pl.BlockSpec((2 * FGROUP_DIM, 2 * FGROUP_DIM), lambda bi, i: (0, 0)),
        ],
        out_specs=[
            pl.BlockSpec((None, n1, ts2, d), lambda bi, i: (bi, 0, i, 0)),
            pl.BlockSpec((None, n1, ts2, d), lambda bi, i: (bi, 0, i, 0)),
        ],
        out_shape=[z_shape, z_shape],
        compiler_params=pltpu.CompilerParams(
            dimension_semantics=("parallel", "parallel"), vmem_limit_bytes=VMEM_LIMIT),
        name="fourier_a",
    )(x.reshape(b, n1, n2, d), g2, f1, cs)

    scale = float(1.0 / np.sqrt(s * FGROUP_DIM))
    delta = pl.pallas_call(
        functools.partial(_fourier_c_kernel, n2=n2, tk1=tk1, scale=scale),
        grid=(b, n1 // tk1),
        in_specs=[
            pl.BlockSpec((None, tk1, n2, d), lambda bi, i: (bi, i, 0, 0)),
            pl.BlockSpec((None, tk1, n2, d), lambda bi, i: (bi, i, 0, 0)),
            pl.BlockSpec((n2, 2 * n2), lambda bi, i: (0, 0)),
            _layer_spec(w_bf16, layer),
        ],
        out_specs=pl.BlockSpec((None, n2, tk1, d), lambda bi, i: (bi, 0, i, 0)),
        out_shape=jax.ShapeDtypeStruct((b, n2, n1, d), FOURIER_DELTA_DTYPE),
        compiler_params=pltpu.CompilerParams(
            dimension_semantics=("parallel", "parallel"), vmem_limit_bytes=VMEM_LIMIT),
        name="fourier_c",
    )(zr, zi, f2, w_bf16)
    return delta.reshape(b * s, d)


def _qkv_kernel(x_ref, g_ref, w_ref, qg_ref, kg_ref, cos_ref, sin_ref, q_ref, k_ref, vt_ref):
    h = _rms(x_ref[...], g_ref[...]).astype(BF16)
    y = jnp.dot(h, w_ref[...], preferred_element_type=F32)
    cos = cos_ref[...]
    sin = sin_ref[...]
    ones = jnp.ones((2 * HEAD_DIM, HEAD_DIM), BF16)
    even_lane = (lax.broadcasted_iota(jnp.int32, (1, HEAD_DIM), 1) % 2) == 0

    def norm_rope(y_head, a_own, a_partner):
        sq = y_head * y_head
        hi = sq.astype(BF16)
        lo = (sq - hi.astype(F32)).astype(BF16)
        ss = jnp.dot(jnp.concatenate([hi, lo], axis=1), ones, preferred_element_type=F32)
        r = lax.rsqrt(ss * (1.0 / HEAD_DIM) + EPS)
        partner = jnp.where(even_lane, pltpu.roll(y_head, HEAD_DIM - 1, axis=1),
                            pltpu.roll(y_head, 1, axis=1))
        return (y_head * a_own + partner * a_partner) * r

    qg = qg_ref[...] * ATTN_Q_SCALE
    kg = kg_ref[...]
    aq_own, aq_partner = cos * qg[0:1], sin * qg[1:2]
    ak_own, ak_partner = cos * kg[0:1], sin * kg[1:2]
    for hd in range(N_HEADS):
        sl = slice(hd * HEAD_DIM, (hd + 1) * HEAD_DIM)
        q_ref[:, sl] = norm_rope(y[:, sl], aq_own, aq_partner).astype(BF16)
    for hd in range(N_KV_HEADS):
        sl = slice(hd * HEAD_DIM, (hd + 1) * HEAD_DIM)
        so = slice(Q_DIM + hd * HEAD_DIM, Q_DIM + (hd + 1) * HEAD_DIM)
        k_ref[:, sl] = norm_rope(y[:, so], ak_own, ak_partner).astype(BF16)
    vt_ref[...] = y[:, Q_DIM + KV_DIM:].T.astype(BF16)


def _attn_kernel(q_ref, k_ref, vt_ref, o_ref, s0_ref, s1_ref, acc_ref, *, tq, tk):
    n_chunks = vt_ref.shape[0]
    n_items = (q_ref.shape[0] // tq) * n_chunks
    m_cols = GROUP * tq

    def scores(t, dst_ref):
        q_start = pl.multiple_of((t // n_chunks) * tq, tq)
        k_start = pl.multiple_of((t % n_chunks) * tk, tk)
        qb = q_ref[pl.ds(q_start, tq), :]
        q = jnp.concatenate([qb[:, gi * HEAD_DIM:(gi + 1) * HEAD_DIM] for gi in range(GROUP)], axis=0)
        kc = k_ref[pl.ds(k_start, tk), :]
        st = lax.dot_general(kc, q, (((1,), (1,)), ((), ())), preferred_element_type=F32)
        dst_ref[...] = st
        return jnp.max(st, axis=0, keepdims=True)

    def consume(t, src_ref, m, alpha, l):
        p = jnp.exp2(src_ref[...] - m)
        l = alpha * l + jnp.sum(p, axis=0, keepdims=True)
        pv = jnp.dot(vt_ref[t % n_chunks], p.astype(BF16), preferred_element_type=F32)
        acc_ref[...] = alpha * acc_ref[...] + pv
        return l

    def finalize(t, l):
        o_t = acc_ref[...] / l
        q_start = pl.multiple_of((t // n_chunks) * tq, tq)
        for gi in range(GROUP):
            o_ref[pl.ds(q_start, tq), gi * HEAD_DIM:(gi + 1) * HEAD_DIM] = (
                o_t[:, gi * tq:(gi + 1) * tq].T.astype(BF16))

    def step(t, cur_ref, nxt_ref, carry, maybe_last):
        m, alpha, l = carry
        cmax = scores(t + 1, nxt_ref)
        l = consume(t, cur_ref, m, alpha, l)
        if not maybe_last:
            m_next = jnp.maximum(m, cmax)
            return m_next, jnp.exp2(m - m_next), l
        last = (t % n_chunks) == n_chunks - 1
        pl.when(last)(lambda: finalize(t, l))
        m_next = jnp.where(last, cmax, jnp.maximum(m, cmax))
        alpha_next = jnp.where(last, 0.0, jnp.exp2(m - m_next))
        return m_next, alpha_next, l

    def body(tt, carry):
        carry = step(2 * tt, s0_ref, s1_ref, carry, False)
        return step(2 * tt + 1, s1_ref, s0_ref, carry, True)

    acc_ref[...] = jnp.zeros_like(acc_ref)
    m0 = scores(0, s0_ref)
    zero = jnp.zeros((1, m_cols), F32)
    carry = lax.fori_loop(0, n_items // 2 - 1, body, (m0, zero, zero))
    m, alpha, l = step(n_items - 2, s0_ref, s1_ref, carry, False)
    l = consume(n_items - 1, s1_ref, m, alpha, l)
    finalize(n_items - 1, l)


def _attn_bounded_kernel(q_ref, k_ref, vt_ref, o_ref, p0_ref, p1_ref, acc_ref, *, tq, tk):
    n_chunks = vt_ref.shape[0]
    n_tiles = q_ref.shape[0] // tq
    bufs = (p0_ref, p1_ref)

    def probs(qi, j, dst_ref):
        q_start = pl.multiple_of(qi * tq, tq)
        qb = q_ref[pl.ds(q_start, tq), :]
        q = jnp.concatenate([qb[:, gi * HEAD_DIM:(gi + 1) * HEAD_DIM] for gi in range(GROUP)], axis=0)
        kc = k_ref[j * tk:(j + 1) * tk, :]
        st = lax.dot_general(kc, q, (((1,), (1,)), ((), ())), preferred_element_type=F32)
        p = jnp.exp2(st)
        dst_ref[...] = p.astype(BF16)
        return jnp.sum(p, axis=0, keepdims=True)

    def tile(qi, lsum, has_next):
        l = None
        for j in range(n_chunks):
            cur_ref, nxt_ref = bufs[j % 2], bufs[(j + 1) % 2]
            if j + 1 < n_chunks:
                lsum_next = probs(qi, j + 1, nxt_ref)
            elif has_next:
                lsum_next = probs(qi + 1, 0, nxt_ref)
            else:
                lsum_next = None
            pv = jnp.dot(vt_ref[j], cur_ref[...], preferred_element_type=F32)
            if j == 0:
                acc_ref[...] = pv
            else:
                acc_ref[...] += pv
            l = lsum if l is None else l + lsum
            lsum = lsum_next
        o_t = acc_ref[...] / l
        q_start = pl.multiple_of(qi * tq, tq)
        for gi in range(GROUP):
            o_ref[pl.ds(q_start, tq), gi * HEAD_DIM:(gi + 1) * HEAD_DIM] = (
                o_t[:, gi * tq:(gi + 1) * tq].T.astype(BF16))
        return lsum

    lsum = probs(0, 0, p0_ref)
    lsum = lax.fori_loop(0, n_tiles - 1, lambda qi, ls: tile(qi, ls, True), lsum)
    tile(n_tiles - 1, lsum, False)


def _rope_tables(seq_len):
    f32 = np.float32
    rows = seq_len // GRID_W
    row = np.repeat(np.arange(rows, dtype=f32), GRID_W)
    col = np.tile(np.arange(GRID_W, dtype=f32), rows)
    inv = (f32(ROPE_THETA) ** (-np.arange(0, AXIS_ROT_DIM, 2, dtype=f32) / f32(AXIS_ROT_DIM))).astype(f32)
    ang = np.concatenate([row[:, None] * inv, col[:, None] * inv], axis=-1)
    cos = np.repeat(np.cos(ang), 2, axis=-1)
    sin = np.repeat(np.sin(ang), 2, axis=-1) * np.tile(np.array([-1.0, 1.0], f32), HEAD_DIM // 2)
    return jnp.asarray(cos, F32), jnp.asarray(sin, F32)


def _attention_qkv(x, g, w_qkv_bf16, layer, q_gain, k_gain):
    b, s, d = x.shape
    t = b * s
    tq = 256
    tk = 1024
    tm = 512
    per_chunk = tk // tm
    assert s % (4 * tk) == 0 and s % tq == 0, "attention loops take key chunks four at a time"
    cos, sin = _rope_tables(s)
    nblk = s // tm
    q, k, vt = pl.pallas_call(
        _qkv_kernel,
        grid=(t // tm,),
        in_specs=[
            pl.BlockSpec((tm, d), lambda i: (i, 0)),
            pl.BlockSpec((1, d), lambda i: (0, 0)),
            _layer_spec(w_qkv_bf16, layer),
            pl.BlockSpec((2, HEAD_DIM), lambda i: (0, 0)),
            pl.BlockSpec((2, HEAD_DIM), lambda i: (0, 0)),
            pl.BlockSpec((tm, HEAD_DIM), lambda i: (i % nblk, 0)),
            pl.BlockSpec((tm, HEAD_DIM), lambda i: (i % nblk, 0)),
        ],
        out_specs=[
            pl.BlockSpec((tm, Q_DIM), lambda i: (i, 0)),
            pl.BlockSpec((tm, KV_DIM), lambda i: (i, 0)),
            pl.BlockSpec((None, KV_DIM, tm), lambda i: (i // per_chunk, 0, i % per_chunk)),
        ],
        out_shape=[
            jax.ShapeDtypeStruct((t, Q_DIM), BF16),
            jax.ShapeDtypeStruct((t, KV_DIM), BF16),
            jax.ShapeDtypeStruct((t // tk, KV_DIM, tk), BF16),
        ],
        compiler_params=pltpu.CompilerParams(
            dimension_semantics=("parallel",), vmem_limit_bytes=VMEM_LIMIT),
        name="qkv_proj",
    )(x.reshape(t, d), g.reshape(1, d), w_qkv_bf16, q_gain, k_gain, cos, sin)

    def attention_call(body, buf_dtype, name):
        return pl.pallas_call(
            functools.partial(body, tq=tq, tk=tk),
            grid=(b, N_KV_HEADS),
            in_specs=[
                pl.BlockSpec((None, s, GROUP * HEAD_DIM), lambda bi, hi: (bi, 0, hi),
                             pipeline_mode=pl.Buffered(1)),
                pl.BlockSpec((None, s, HEAD_DIM), lambda bi, hi: (bi, 0, hi)),
                pl.BlockSpec((None, s // tk, HEAD_DIM, tk), lambda bi, hi: (bi, 0, hi, 0)),
            ],
            out_specs=pl.BlockSpec((None, s, GROUP * HEAD_DIM), lambda bi, hi: (bi, 0, hi)),
            out_shape=jax.ShapeDtypeStruct((b, s, Q_DIM), BF16),
            scratch_shapes=[
                pltpu.VMEM((tk, GROUP * tq), buf_dtype),
                pltpu.VMEM((tk, GROUP * tq), buf_dtype),
                pltpu.VMEM((HEAD_DIM, GROUP * tq), F32),
            ],
            compiler_params=pltpu.CompilerParams(
                dimension_semantics=("parallel", "parallel"), vmem_limit_bytes=VMEM_LIMIT),
            name=name,
        )

    score_bound = (HEAD_DIM * ATTN_Q_SCALE) * jnp.max(jnp.abs(q_gain)) * jnp.max(jnp.abs(k_gain))
    o = lax.cond(
        score_bound <= ATTN_BOUNDED_MAX_LOG2_SCORE,
        attention_call(_attn_bounded_kernel, BF16, "attention_bounded"),
        attention_call(_attn_kernel, F32, "attention"),
        q.reshape(b, s, Q_DIM), k.reshape(b, s, KV_DIM), vt.reshape(b, s // tk, KV_DIM, tk))
    return o.reshape(t, Q_DIM)


def _ffn_kernel(*refs, with_attn):
    if with_attn:
        x_ref, a_ref, wo_ref, g_ref, wg_ref, wu_ref, wd_ref, o_ref = refs
        x = x_ref[...] + jnp.dot(a_ref[...], wo_ref[...], preferred_element_type=F32)
    else:
        x_ref, delta_ref, g_ref, wg_ref, wu_ref, wd_ref, o_ref = refs
        x = x_ref[...] + delta_ref[...].astype(F32)
    r = lax.rsqrt(jnp.mean(x * x, axis=-1, keepdims=True) + EPS)
    xg = (x * g_ref[...]).astype(BF16)
    gate = jnp.dot(xg, wg_ref[...], preferred_element_type=F32) * r
    up = jnp.dot(xg, wu_ref[...], preferred_element_type=F32) * r
    act = (gate * jax.nn.sigmoid(gate) * up).astype(BF16)
    o_ref[...] = x + jnp.dot(act, wd_ref[...], preferred_element_type=F32)


def _ffn_layer(x2d, g, layer, wg, wu, wd, attn=None, delta=None):
    assert (attn is None) != (delta is None)
    t, d = x2d.shape
    tm = 512
    resident = functools.partial(pl.BlockSpec, index_map=lambda i: (0, 0), pipeline_mode=pl.Buffered(1))
    row = lambda width: pl.BlockSpec((tm, width), lambda i: (i, 0))
    in_specs = [row(d)]
    args = [x2d]
    if attn is not None:
        a2d, wo, attn_layer = attn
        in_specs += [row(Q_DIM), _layer_spec(wo, attn_layer, single_buffer=True)]
        args += [a2d, wo]
    else:
        in_specs += [row(d)]
        args += [delta]
    in_specs += [resident((1, d))] + [_layer_spec(w, layer, single_buffer=True) for w in (wg, wu, wd)]
    args += [g.reshape(1, d), wg, wu, wd]
    return pl.pallas_call(
        functools.partial(_ffn_kernel, with_attn=attn is not None),
        grid=(t // tm,),
        in_specs=in_specs,
        out_specs=row(d),
        out_shape=jax.ShapeDtypeStruct((t, d), F32),
        compiler_params=pltpu.CompilerParams(
            dimension_semantics=("parallel",), vmem_limit_bytes=VMEM_LIMIT),
        name="swiglu",
    )(*args)


def _with_pair_partner(gain):
    n_layers = gain.shape[0]
    partner = gain.reshape(n_layers, HEAD_DIM // 2, 2)[:, :, ::-1].reshape(n_layers, HEAD_DIM)
    return jnp.stack([gain, partner], axis=1)


def kernel(x_prompt, x_sample, norm_mix, norm_ffn, fourier_w, attn_w_qkv, attn_q_norm, attn_k_norm,
           attn_w_o, ffn_w_gate, ffn_w_up, ffn_w_down):
    depth = norm_mix.shape[0]
    fw = fourier_w.astype(BF16)
    wqkv = attn_w_qkv.astype(BF16)
    qn = _with_pair_partner(attn_q_norm)
    kn = _with_pair_partner(attn_k_norm)
    wo = attn_w_o.astype(BF16)
    wg = ffn_w_gate.astype(BF16)
    wu = ffn_w_up.astype(BF16)
    wd = ffn_w_down.astype(BF16)

    def run_trunk(x):
        b, s, d = x.shape
        for i in range(depth):
            j = i // 2
            if i % 2 == 0:
                delta = _fourier_layer(x, norm_mix[i], fw, j)
                x2d = _ffn_layer(x.reshape(b * s, d), norm_ffn[i], i, wg, wu, wd, delta=delta)
            else:
                a2d = _attention_qkv(x, norm_mix[i], wqkv, j, qn[j], kn[j])
                x2d = _ffn_layer(x.reshape(b * s, d), norm_ffn[i], i, wg, wu, wd, attn=(a2d, wo, j))
            x = x2d.reshape(b, s, d)
        return x

    return (run_trunk(x_prompt), run_trunk(x_sample))
```

```python
import functools

import numpy as np
import jax
import jax.numpy as jnp
from jax import lax
from jax.experimental import pallas as pl
from jax.experimental.pallas import tpu as pltpu

D_MODEL = 1024
HEAD_DIM = 128
N_HEADS = D_MODEL // HEAD_DIM
N_KV_HEADS = 2
GROUP = N_HEADS // N_KV_HEADS
Q_DIM = N_HEADS * HEAD_DIM
KV_DIM = N_KV_HEADS * HEAD_DIM
QKV_DIM = Q_DIM + 2 * KV_DIM
GRID_W = 64
AXIS_ROT_DIM = HEAD_DIM // 2
ROPE_THETA = 10000.0
N_FGROUPS = 8
FGROUP_DIM = D_MODEL // N_FGROUPS
EPS = 1e-6

ATTN_Q_SCALE = float(HEAD_DIM ** -0.5 * np.log2(np.e))
ATTN_BOUNDED_MAX_LOG2_SCORE = 60.0

LANES = 128
VMEM_LIMIT = 56 * 1024 * 1024

F32 = jnp.float32
BF16 = jnp.bfloat16
FOURIER_DELTA_DTYPE = BF16


def _rms(x, g):
    ms = jnp.mean(x * x, axis=-1, keepdims=True)
    return x * lax.rsqrt(ms + EPS) * g


def _layer_spec(stacked, layer, single_buffer=False):
    zeros = (0,) * (stacked.ndim - 1)
    mode = dict(pipeline_mode=pl.Buffered(1)) if single_buffer else {}
    return pl.BlockSpec((None,) + stacked.shape[1:], lambda *_: (layer,) + zeros, **mode)


def _dft_cos_sin(n):
    k = np.arange(n)
    ang = 2.0 * np.pi * ((k[:, None] * k[None, :]) % n) / n
    return np.cos(ang), np.sin(ang)


def _fourier_a_kernel(x_ref, g_ref, f1_ref, cs_ref, yr_ref, yi_ref, *, n1, ts2):
    d = D_MODEL
    h = _rms(x_ref[...], g_ref[...])
    ht = jnp.swapaxes(h.astype(BF16), 0, 1)
    z = [jnp.dot(f1_ref[j], ht[j], preferred_element_type=F32) for j in range(ts2)]
    zr = jnp.concatenate([zj[:n1] for zj in z], axis=0).astype(BF16)
    zi = jnp.concatenate([zj[n1:] for zj in z], axis=0).astype(BF16)
    cs = cs_ref[...]
    yr, yi = [], []
    for gi in range(N_FGROUPS):
        sl = slice(gi * FGROUP_DIM, (gi + 1) * FGROUP_DIM)
        y = jnp.dot(jnp.concatenate([zr[:, sl], zi[:, sl]], axis=1), cs, preferred_element_type=F32)
        yr.append(y[:, :FGROUP_DIM])
        yi.append(y[:, FGROUP_DIM:])
    yr = jnp.concatenate(yr, axis=1).astype(BF16).reshape(ts2, n1, d)
    yi = jnp.concatenate(yi, axis=1).astype(BF16).reshape(ts2, n1, d)
    yr_ref[...] = jnp.swapaxes(yr, 0, 1)
    yi_ref[...] = jnp.swapaxes(yi, 0, 1)


def _fourier_c_kernel(yr_ref, yi_ref, f2_ref, w_ref, o_ref, *, n2, tk1, scale):
    d = D_MODEL
    f2 = f2_ref[...]
    mixed = [jnp.dot(f2, jnp.concatenate([yr_ref[j], yi_ref[j]], axis=0), preferred_element_type=F32)
             for j in range(tk1)]
    mixed = (jnp.concatenate(mixed, axis=0) * scale).astype(BF16)
    delta = jnp.dot(mixed, w_ref[...], preferred_element_type=F32).astype(o_ref.dtype)
    o_ref[...] = jnp.swapaxes(delta.reshape(tk1, n2, d), 0, 1)


def _fourier_tables(seq_len):
    n1 = 64
    n2 = seq_len // n1
    s2_, k1_, s1_ = np.arange(n2)[:, None, None], np.arange(n1)[None, :, None], np.arange(n1)[None, None, :]
    ang = 2.0 * np.pi * ((k1_ * s1_ * n2 + k1_ * s2_) % seq_len) / seq_len
    f1 = np.concatenate([np.cos(ang), -np.sin(ang)], axis=1)
    c2, s2 = _dft_cos_sin(n2)
    f2 = np.concatenate([c2, s2], axis=1)
    cc, sc = _dft_cos_sin(FGROUP_DIM)
    cs = np.block([[cc, -sc], [sc, cc]])
    return n1, n2, f1, f2, cs


def _fourier_layer(x, g, w_bf16, layer):
    b, s, d = x.shape
    n1, n2, f1, f2, cs = _fourier_tables(s)
    ts2 = 16
    tk1 = 32 // jnp.dtype(FOURIER_DELTA_DTYPE).itemsize
    f1 = jnp.asarray(f1, BF16)
    f2 = jnp.asarray(f2, BF16)
    cs = jnp.asarray(cs, BF16)
    g2 = g.reshape(1, d)

    z_shape = jax.ShapeDtypeStruct((b, n1, n2, d), BF16)
    zr, zi = pl.pallas_call(
        functools.partial(_fourier_a_kernel, n1=n1, ts2=ts2),
        grid=(b, n2 // ts2),
        in_specs=[
            pl.BlockSpec((None, n1, ts2, d), lambda bi, i: (bi, 0, i, 0)),
            pl.BlockSpec((1, d), lambda bi, i: (0, 0)),
            pl.BlockSpec((ts2, 2 * n1, n1), lambda bi, i: (i, 0, 0)),
            pl.BlockSpec((2 * FGROUP_DIM, 2 * FGROUP_DIM), lambda bi, i: (0, 0)),
        ],
        out_specs=[
            pl.BlockSpec((None, n1, ts2, d), lambda bi, i: (bi, 0, i, 0)),
            pl.BlockSpec((None, n1, ts2, d), lambda bi, i: (bi, 0, i, 0)),
        ],
        out_shape=[z_shape, z_shape],
        compiler_params=pltpu.CompilerParams(
            dimension_semantics=("parallel", "parallel"), vmem_limit_bytes=VMEM_LIMIT),
        name="fourier_a",
    )(x.reshape(b, n1, n2, d), g2, f1, cs)

    scale = float(1.0 / np.sqrt(s * FGROUP_DIM))
    delta = pl.pallas_call(
        functools.partial(_fourier_c_kernel, n2=n2, tk1=tk1, scale=scale),
        grid=(b, n1 // tk1),
        in_specs=[
            pl.BlockSpec((None, tk1, n2, d), lambda bi, i: (bi, i, 0, 0)),
            pl.BlockSpec((None, tk1, n2, d), lambda bi, i: (bi, i, 0, 0)),
            pl.BlockSpec((n2, 2 * n2), lambda bi, i: (0, 0)),
            _layer_spec(w_bf16, layer),
        ],
        out_specs=pl.BlockSpec((None, n2, tk1, d), lambda bi, i: (bi, 0, i, 0)),
        out_shape=jax.ShapeDtypeStruct((b, n2, n1, d), FOURIER_DELTA_DTYPE),
        compiler_params=pltpu.CompilerParams(
            dimension_semantics=("parallel", "parallel"), vmem_limit_bytes=VMEM_LIMIT),
        name="fourier_c",
    )(zr, zi, f2, w_bf16)
    return delta.reshape(b * s, d)


def _qkv_kernel(x_ref, g_ref, w_ref, qg_ref, kg_ref, cos_ref, sin_ref, q_ref, k_ref, vt_ref):
    h = _rms(x_ref[...], g_ref[...]).astype(BF16)
    y = jnp.dot(h, w_ref[...], preferred_element_type=F32)
    cos = cos_ref[...]
    sin = sin_ref[...]
    ones = jnp.ones((2 * HEAD_DIM, HEAD_DIM), BF16)
    even_lane = (lax.broadcasted_iota(jnp.int32, (1, HEAD_DIM), 1) % 2) == 0

    def norm_rope(y_head, a_own, a_partner):
        sq = y_head * y_head
        hi = sq.astype(BF16)
        lo = (sq - hi.astype(F32)).astype(BF16)
        ss = jnp.dot(jnp.concatenate([hi, lo], axis=1), ones, preferred_element_type=F32)
        r = lax.rsqrt(ss * (1.0 / HEAD_DIM) + EPS)
        partner = jnp.where(even_lane, pltpu.roll(y_head, HEAD_DIM - 1, axis=1),
                            pltpu.roll(y_head, 1, axis=1))
        return (y_head * a_own + partner * a_partner) * r

    qg = qg_ref[...] * ATTN_Q_SCALE
    kg = kg_ref[...]
    aq_own, aq_partner = cos * qg[0:1], sin * qg[1:2]
    ak_own, ak_partner = cos * kg[0:1], sin * kg[1:2]
    for hd in range(N_HEADS):
        sl = slice(hd * HEAD_DIM, (hd + 1) * HEAD_DIM)
        q_ref[:, sl] = norm_rope(y[:, sl], aq_own, aq_partner).astype(BF16)
    for hd in range(N_KV_HEADS):
        sl = slice(hd * HEAD_DIM, (hd + 1) * HEAD_DIM)
        so = slice(Q_DIM + hd * HEAD_DIM, Q_DIM + (hd + 1) * HEAD_DIM)
        k_ref[:, sl] = norm_rope(y[:, so], ak_own, ak_partner).astype(BF16)
    vt_ref[...] = y[:, Q_DIM + KV_DIM:].T.astype(BF16)


def _attn_kernel(q_ref, k_ref, vt_ref, o_ref, s0_ref, s1_ref, acc_ref, *, tq, tk):
    n_chunks = vt_ref.shape[0]
    n_items = (q_ref.shape[0] // tq) * n_chunks
    m_cols = GROUP * tq

    def scores(t, dst_ref):
        q_start = pl.multiple_of((t // n_chunks) * tq, tq)
        k_start = pl.multiple_of((t % n_chunks) * tk, tk)
        qb = q_ref[pl.ds(q_start, tq), :]
        q = jnp.concatenate([qb[:, gi * HEAD_DIM:(gi + 1) * HEAD_DIM] for gi in range(GROUP)], axis=0)
        kc = k_ref[pl.ds(k_start, tk), :]
        st = lax.dot_general(kc, q, (((1,), (1,)), ((), ())), preferred_element_type=F32)
        dst_ref[...] = st
        return jnp.max(st, axis=0, keepdims=True)

    def consume(t, src_ref, m, alpha, l):
        p = jnp.exp2(src_ref[...] - m)
        l = alpha * l + jnp.sum(p, axis=0, keepdims=True)
        pv = jnp.dot(vt_ref[t % n_chunks], p.astype(BF16), preferred_element_type=F32)
        acc_ref[...] = alpha * acc_ref[...] + pv
        return l

    def finalize(t, l):
        o_t = acc_ref[...] / l
        q_start = pl.multiple_of((t // n_chunks) * tq, tq)
        for gi in range(GROUP):
            o_ref[pl.ds(q_start, tq), gi * HEAD_DIM:(gi + 1) * HEAD_DIM] = (
                o_t[:, gi * tq:(gi + 1) * tq].T.astype(BF16))

    def step(t, cur_ref, nxt_ref, carry, maybe_last):
        m, alpha, l = carry
        cmax = scores(t + 1, nxt_ref)
        l = consume(t, cur_ref, m, alpha, l)
        if not maybe_last:
            m_next = jnp.maximum(m, cmax)
            return m_next, jnp.exp2(m - m_next), l
        last = (t % n_chunks) == n_chunks - 1
        pl.when(last)(lambda: finalize(t, l))
        m_next = jnp.where(last, cmax, jnp.maximum(m, cmax))
        alpha_next = jnp.where(last, 0.0, jnp.exp2(m - m_next))
        return m_next, alpha_next, l

    def body(tt, carry):
        carry = step(2 * tt, s0_ref, s1_ref, carry, False)
        return step(2 * tt + 1, s1_ref, s0_ref, carry, True)

    acc_ref[...] = jnp.zeros_like(acc_ref)
    m0 = scores(0, s0_ref)
    zero = jnp.zeros((1, m_cols), F32)
    carry = lax.fori_loop(0, n_items // 2 - 1, body, (m0, zero, zero))
    m, alpha, l = step(n_items - 2, s0_ref, s1_ref, carry, False)
    l = consume(n_items - 1, s1_ref, m, alpha, l)
    finalize(n_items - 1, l)


def _attn_bounded_kernel(q_ref, k_ref, vt_ref, o_ref, p0_ref, p1_ref, acc_ref, *, tq, tk):
    n_chunks = vt_ref.shape[0]
    n_tiles = q_ref.shape[0] // tq
    bufs = (p0_ref, p1_ref)

    def probs(qi, j, dst_ref):
        q_start = pl.multiple_of(qi * tq, tq)
        qb = q_ref[pl.ds(q_start, tq), :]
        q = jnp.concatenate([qb[:, gi * HEAD_DIM:(gi + 1) * HEAD_DIM] for gi in range(GROUP)], axis=0)
        kc = k_ref[j * tk:(j + 1) * tk, :]
        st = lax.dot_general(kc, q, (((1,), (1,)), ((), ())), preferred_element_type=F32)
        p = jnp.exp2(st)
        dst_ref[...] = p.astype(BF16)
        return jnp.sum(p, axis=0, keepdims=True)

    def tile(qi, lsum, has_next):
        l = None
        for j in range(n_chunks):
            cur_ref, nxt_ref = bufs[j % 2], bufs[(j + 1) % 2]
            if j + 1 < n_chunks:
                lsum_next = probs(qi, j + 1, nxt_ref)
            elif has_next:
                lsum_next = probs(qi + 1, 0, nxt_ref)
            else:
                lsum_next = None
            pv = jnp.dot(vt_ref[j], cur_ref[...], preferred_element_type=F32)
            if j == 0:
                acc_ref[...] = pv
            else:
                acc_ref[...] += pv
            l = lsum if l is None else l + lsum
            lsum = lsum_next
        o_t = acc_ref[...] / l
        q_start = pl.multiple_of(qi * tq, tq)
        for gi in range(GROUP):
            o_ref[pl.ds(q_start, tq), gi * HEAD_DIM:(gi + 1) * HEAD_DIM] = (
                o_t[:, gi * tq:(gi + 1) * tq].T.astype(BF16))
        return lsum

    lsum = probs(0, 0, p0_ref)
    lsum = lax.fori_loop(0, n_tiles - 1, lambda qi, ls: tile(qi, ls, True), lsum)
    tile(n_tiles - 1, lsum, False)


def _rope_tables(seq_len):
    f32 = np.float32
    rows = seq_len // GRID_W
    row = np.repeat(np.arange(rows, dtype=f32), GRID_W)
    col = np.tile(np.arange(GRID_W, dtype=f32), rows)
    inv = (f32(ROPE_THETA) ** (-np.arange(0, AXIS_ROT_DIM, 2, dtype=f32) / f32(AXIS_ROT_DIM))).astype(f32)
    ang = np.concatenate([row[:, None] * inv, col[:, None] * inv], axis=-1)
    cos = np.repeat(np.cos(ang), 2, axis=-1)
    sin = np.repeat(np.sin(ang), 2, axis=-1) * np.tile(np.array([-1.0, 1.0], f32), HEAD_DIM // 2)
    return jnp.asarray(cos, F32), jnp.asarray(sin, F32)


def _attention_qkv(x, g, w_qkv_bf16, layer, q_gain, k_gain):
    b, s, d = x.shape
    t = b * s
    tq = 256
    tk = 1024
    tm = 1024
    per_chunk = tk // tm
    assert s % (2 * tk) == 0 and s % tq == 0, "attention loops take key chunks in pairs"
    cos, sin = _rope_tables(s)
    nblk = s // tm
    q, k, vt = pl.pallas_call(
        _qkv_kernel,
        grid=(t // tm,),
        in_specs=[
            pl.BlockSpec((tm, d), lambda i: (i, 0)),
            pl.BlockSpec((1, d), lambda i: (0, 0)),
            _layer_spec(w_qkv_bf16, layer),
            pl.BlockSpec((2, HEAD_DIM), lambda i: (0, 0)),
            pl.BlockSpec((2, HEAD_DIM), lambda i: (0, 0)),
            pl.BlockSpec((tm, HEAD_DIM), lambda i: (i % nblk, 0)),
            pl.BlockSpec((tm, HEAD_DIM), lambda i: (i % nblk, 0)),
        ],
        out_specs=[
            pl.BlockSpec((tm, Q_DIM), lambda i: (i, 0)),
            pl.BlockSpec((tm, KV_DIM), lambda i: (i, 0)),
            pl.BlockSpec((None, KV_DIM, tm), lambda i: (i // per_chunk, 0, i % per_chunk)),
        ],
        out_shape=[
            jax.ShapeDtypeStruct((t, Q_DIM), BF16),
            jax.ShapeDtypeStruct((t, KV_DIM), BF16),
            jax.ShapeDtypeStruct((t // tk, KV_DIM, tk), BF16),
        ],
        compiler_params=pltpu.CompilerParams(
            dimension_semantics=("parallel",), vmem_limit_bytes=VMEM_LIMIT),
        name="qkv_proj",
    )(x.reshape(t, d), g.reshape(1, d), w_qkv_bf16, q_gain, k_gain, cos, sin)

    q_bytes = s * GROUP * HEAD_DIM * jnp.dtype(BF16).itemsize
    q_mode = dict(pipeline_mode=pl.Buffered(1)) if 2 * q_bytes > VMEM_LIMIT // 4 else {}

    def attention_call(body, buf_dtype, name):
        return pl.pallas_call(
            functools.partial(body, tq=tq, tk=tk),
            grid=(b, N_KV_HEADS),
            in_specs=[
                pl.BlockSpec((None, s, GROUP * HEAD_DIM), lambda bi, hi: (bi, 0, hi), **q_mode),
                pl.BlockSpec((None, s, HEAD_DIM), lambda bi, hi: (bi, 0, hi)),
                pl.BlockSpec((None, s // tk, HEAD_DIM, tk), lambda bi, hi: (bi, 0, hi, 0)),
            ],
            out_specs=pl.BlockSpec((None, s, GROUP * HEAD_DIM), lambda bi, hi: (bi, 0, hi)),
            out_shape=jax.ShapeDtypeStruct((b, s, Q_DIM), BF16),
            scratch_shapes=[
                pltpu.VMEM((tk, GROUP * tq), buf_dtype),
                pltpu.VMEM((tk, GROUP * tq), buf_dtype),
                pltpu.VMEM((HEAD_DIM, GROUP * tq), F32),
            ],
            compiler_params=pltpu.CompilerParams(
                dimension_semantics=("parallel", "parallel"), vmem_limit_bytes=VMEM_LIMIT),
            name=name,
        )

    score_bound = (HEAD_DIM * ATTN_Q_SCALE) * jnp.max(jnp.abs(q_gain)) * jnp.max(jnp.abs(k_gain))
    o = lax.cond(
        score_bound <= ATTN_BOUNDED_MAX_LOG2_SCORE,
        attention_call(_attn_bounded_kernel, BF16, "attention_bounded"),
        attention_call(_attn_kernel, F32, "attention"),
        q.reshape(b, s, Q_DIM), k.reshape(b, s, KV_DIM), vt.reshape(b, s // tk, KV_DIM, tk))
    return o.reshape(t, Q_DIM)


def _ffn_kernel(*refs, with_attn):
    if with_attn:
        x_ref, a_ref, wo_ref, g_ref, wg_ref, wu_ref, wd_ref, o_ref = refs
        x = x_ref[...] + jnp.dot(a_ref[...], wo_ref[...], preferred_element_type=F32)
    else:
        x_ref, delta_ref, g_ref, wg_ref, wu_ref, wd_ref, o_ref = refs
        x = x_ref[...] + delta_ref[...].astype(F32)
    r = lax.rsqrt(jnp.mean(x * x, axis=-1, keepdims=True) + EPS)
    xg = (x * g_ref[...]).astype(BF16)
    gate = jnp.dot(xg, wg_ref[...], preferred_element_type=F32) * r
    up = jnp.dot(xg, wu_ref[...], preferred_element_type=F32) * r
    act = (gate * jax.nn.sigmoid(gate) * up).astype(BF16)
    o_ref[...] = x + jnp.dot(act, wd_ref[...], preferred_element_type=F32)


def _ffn_layer(x2d, g, layer, wg, wu, wd, attn=None, delta=None):
    assert (attn is None) != (delta is None)
    t, d = x2d.shape
    tm = 512
    resident = functools.partial(pl.BlockSpec, index_map=lambda i: (0, 0), pipeline_mode=pl.Buffered(1))
    row = lambda width: pl.BlockSpec((tm, width), lambda i: (i, 0))
    in_specs = [row(d)]
    args = [x2d]
    if attn is not None:
        a2d, wo, attn_layer = attn
        in_specs += [row(Q_DIM), _layer_spec(wo, attn_layer, single_buffer=True)]
        args += [a2d, wo]
    else:
        in_specs += [row(d)]
        args += [delta]
    in_specs += [resident((1, d))] + [_layer_spec(w, layer, single_buffer=True) for w in (wg, wu, wd)]
    args += [g.reshape(1, d), wg, wu, wd]
    return pl.pallas_call(
        functools.partial(_ffn_kernel, with_attn=attn is not None),
        grid=(t // tm,),
        in_specs=in_specs,
        out_specs=row(d),
        out_shape=jax.ShapeDtypeStruct((t, d), F32),
        compiler_params=pltpu.CompilerParams(
            dimension_semantics=("parallel",), vmem_limit_bytes=VMEM_LIMIT),
        name="swiglu",
    )(*args)


def _with_pair_partner(gain):
    n_layers = gain.shape[0]
    partner = gain.reshape(n_layers, HEAD_DIM // 2, 2)[:, :, ::-1].reshape(n_layers, HEAD_DIM)
    return jnp.stack([gain, partner], axis=1)


def kernel(x_prompt, x_sample, norm_mix, norm_ffn, fourier_w, attn_w_qkv, attn_q_norm, attn_k_norm,
           attn_w_o, ffn_w_gate, ffn_w_up, ffn_w_down):
    depth = norm_mix.shape[0]
    fw = fourier_w.astype(BF16)
    wqkv = attn_w_qkv.astype(BF16)
    qn = _with_pair_partner(attn_q_norm)
    kn = _with_pair_partner(attn_k_norm)
    wo = attn_w_o.astype(BF16)
    wg = ffn_w_gate.astype(BF16)
    wu = ffn_w_up.astype(BF16)
    wd = ffn_w_down.astype(BF16)

    def run_trunk(x):
        b, s, d = x.shape
        for i in range(depth):
            j = i // 2
            if i % 2 == 0:
                delta = _fourier_layer(x, norm_mix[i], fw, j)
                x2d = _ffn_layer(x.reshape(b * s, d), norm_ffn[i], i, wg, wu, wd, delta=delta)
            else:
                a2d = _attention_qkv(x, norm_mix[i], wqkv, j, qn[j], kn[j])
                x2d = _ffn_layer(x.reshape(b * s, d), norm_ffn[i], i, wg, wu, wd, attn=(a2d, wo, j))
            x = x2d.reshape(b, s, d)
        return x

    return (run_trunk(x_prompt), run_trunk(x_sample))
```

```python
import functools

import numpy as np
import jax
import jax.numpy as jnp
from jax import lax
from jax.experimental import pallas as pl
from jax.experimental.pallas import tpu as pltpu

D_MODEL = 1024
HEAD_DIM = 128
N_HEADS = D_MODEL // HEAD_DIM
N_KV_HEADS = 2
GROUP = N_HEADS // N_KV_HEADS
Q_DIM = N_HEADS * HEAD_DIM
KV_DIM = N_KV_HEADS * HEAD_DIM
QKV_DIM = Q_DIM + 2 * KV_DIM
GRID_W = 64
AXIS_ROT_DIM = HEAD_DIM // 2
ROPE_THETA = 10000.0
N_FGROUPS = 8
FGROUP_DIM = D_MODEL // N_FGROUPS
EPS = 1e-6

ATTN_Q_SCALE = float(HEAD_DIM ** -0.5 * np.log2(np.e))
ATTN_BOUNDED_MAX_LOG2_SCORE = 60.0

F32 = jnp.float32
BF16 = jnp.bfloat16
FOURIER_DELTA_DTYPE = BF16

VMEM_LIMIT = 56 * 1024 * 1024
SUBLANE_BYTES = 32
FOURIER_N1 = 64
FFN_TOKEN_TILE = 1024
FFN_HIDDEN_CHUNK = 768
QKV_TOKEN_TILE = 1024
ATTN_Q_TILE = 256
ATTN_K_TILE = 1024
ATTN_Q_PREFETCH_MAX_BYTES = VMEM_LIMIT // 4


def _rms(x, g):
    ms = jnp.mean(x * x, axis=-1, keepdims=True)
    return x * lax.rsqrt(ms + EPS) * g


def _layer_spec(stacked, layer, single_buffer=False):
    zeros = (0,) * (stacked.ndim - 1)
    mode = dict(pipeline_mode=pl.Buffered(1)) if single_buffer else {}
    return pl.BlockSpec((None,) + stacked.shape[1:], lambda *_: (layer,) + zeros, **mode)


def _dft_cos_sin(n):
    k = np.arange(n)
    ang = 2.0 * np.pi * ((k[:, None] * k[None, :]) % n) / n
    return np.cos(ang), np.sin(ang)


def _fourier_a_kernel(x_ref, g_ref, f1_ref, cs_ref, yr_ref, yi_ref, *, n1, ts2):
    d = D_MODEL
    h = _rms(x_ref[...], g_ref[...])
    ht = jnp.swapaxes(h.astype(BF16), 0, 1)
    z = [jnp.dot(f1_ref[j], ht[j], preferred_element_type=F32) for j in range(ts2)]
    zr = jnp.concatenate([zj[:n1] for zj in z], axis=0).astype(BF16)
    zi = jnp.concatenate([zj[n1:] for zj in z], axis=0).astype(BF16)
    cs = cs_ref[...]
    yr, yi = [], []
    for gi in range(N_FGROUPS):
        sl = slice(gi * FGROUP_DIM, (gi + 1) * FGROUP_DIM)
        y = jnp.dot(jnp.concatenate([zr[:, sl], zi[:, sl]], axis=1), cs, preferred_element_type=F32)
        yr.append(y[:, :FGROUP_DIM])
        yi.append(y[:, FGROUP_DIM:])
    yr = jnp.concatenate(yr, axis=1).astype(BF16).reshape(ts2, n1, d)
    yi = jnp.concatenate(yi, axis=1).astype(BF16).reshape(ts2, n1, d)
    yr_ref[...] = jnp.swapaxes(yr, 0, 1)
    yi_ref[...] = jnp.swapaxes(yi, 0, 1)


def _fourier_c_kernel(yr_ref, yi_ref, f2_ref, w_ref, o_ref, *, n2, tk1, scale):
    d = D_MODEL
    f2 = f2_ref[...]
    mixed = [jnp.dot(f2, jnp.concatenate([yr_ref[j], yi_ref[j]], axis=0), preferred_element_type=F32)
             for j in range(tk1)]
    mixed = (jnp.concatenate(mixed, axis=0) * scale).astype(BF16)
    delta = jnp.dot(mixed, w_ref[...], preferred_element_type=F32).astype(o_ref.dtype)
    o_ref[...] = jnp.swapaxes(delta.reshape(tk1, n2, d), 0, 1)


def _fourier_tables(seq_len):
    n1 = FOURIER_N1
    n2 = seq_len // n1
    s2_, k1_, s1_ = np.arange(n2)[:, None, None], np.arange(n1)[None, :, None], np.arange(n1)[None, None, :]
    ang = 2.0 * np.pi * ((k1_ * s1_ * n2 + k1_ * s2_) % seq_len) / seq_len
    f1 = np.concatenate([np.cos(ang), -np.sin(ang)], axis=1)
    c2, s2 = _dft_cos_sin(n2)
    f2 = np.concatenate([c2, s2], axis=1)
    cc, sc = _dft_cos_sin(FGROUP_DIM)
    cs = np.block([[cc, -sc], [sc, cc]])
    return n1, n2, f1, f2, cs


def _fourier_layer(x, g, w_bf16, layer):
    b, s, d = x.shape
    n1, n2, f1, f2, cs = _fourier_tables(s)
    ts2 = SUBLANE_BYTES // jnp.dtype(BF16).itemsize
    tk1 = SUBLANE_BYTES // jnp.dtype(FOURIER_DELTA_DTYPE).itemsize
    assert d == D_MODEL and s == n1 * n2 and n2 % ts2 == 0 and n1 % tk1 == 0
    f1 = jnp.asarray(f1, BF16)
    f2 = jnp.asarray(f2, BF16)
    cs = jnp.asarray(cs, BF16)
    g2 = g.reshape(1, d)

    y_shape = jax.ShapeDtypeStruct((b, n1, n2, d), BF16)
    yr, yi = pl.pallas_call(
        functools.partial(_fourier_a_kernel, n1=n1, ts2=ts2),
        grid=(b, n2 // ts2),
        in_specs=[
            pl.BlockSpec((None, n1, ts2, d), lambda bi, i: (bi, 0, i, 0)),
            pl.BlockSpec((1, d), lambda bi, i: (0, 0)),
            pl.BlockSpec((ts2, 2 * n1, n1), lambda bi, i: (i, 0, 0)),
            pl.BlockSpec((2 * FGROUP_DIM, 2 * FGROUP_DIM), lambda bi, i: (0, 0)),
        ],
        out_specs=[
            pl.BlockSpec((None, n1, ts2, d), lambda bi, i: (bi, 0, i, 0)),
            pl.BlockSpec((None, n1, ts2, d), lambda bi, i: (bi, 0, i, 0)),
        ],
        out_shape=[y_shape, y_shape],
        compiler_params=pltpu.CompilerParams(
            dimension_semantics=("parallel", "parallel"), vmem_limit_bytes=VMEM_LIMIT),
        name="fourier_a",
    )(x.reshape(b, n1, n2, d), g2, f1, cs)

    scale = float(1.0 / np.sqrt(s * FGROUP_DIM))
    delta = pl.pallas_call(
        functools.partial(_fourier_c_kernel, n2=n2, tk1=tk1, scale=scale),
        grid=(b, n1 // tk1),
        in_specs=[
            pl.BlockSpec((None, tk1, n2, d), lambda bi, i: (bi, i, 0, 0)),
            pl.BlockSpec((None, tk1, n2, d), lambda bi, i: (bi, i, 0, 0)),
            pl.BlockSpec((n2, 2 * n2), lambda bi, i: (0, 0)),
            _layer_spec(w_bf16, layer),
        ],
        out_specs=pl.BlockSpec((None, n2, tk1, d), lambda bi, i: (bi, 0, i, 0)),
        out_shape=jax.ShapeDtypeStruct((b, n2, n1, d), FOURIER_DELTA_DTYPE),
        compiler_params=pltpu.CompilerParams(
            dimension_semantics=("parallel", "parallel"), vmem_limit_bytes=VMEM_LIMIT),
        name="fourier_c",
    )(yr, yi, f2, w_bf16)
    return delta.reshape(b * s, d)


def _qkv_kernel(x_ref, g_ref, w_ref, qg_ref, kg_ref, cos_ref, sin_ref, q_ref, k_ref, vt_ref):
    h = _rms(x_ref[...], g_ref[...]).astype(BF16)
    y = jnp.dot(h, w_ref[...], preferred_element_type=F32)
    cos = cos_ref[...]
    sin = sin_ref[...]
    ones = jnp.ones((2 * HEAD_DIM, HEAD_DIM), BF16)
    even_lane = (lax.broadcasted_iota(jnp.int32, (1, HEAD_DIM), 1) % 2) == 0

    def norm_rope(y_head, a_own, a_partner):
        sq = y_head * y_head
        hi = sq.astype(BF16)
        lo = (sq - hi.astype(F32)).astype(BF16)
        ss = jnp.dot(jnp.concatenate([hi, lo], axis=1), ones, preferred_element_type=F32)
        r = lax.rsqrt(ss * (1.0 / HEAD_DIM) + EPS)
        partner = jnp.where(even_lane, pltpu.roll(y_head, HEAD_DIM - 1, axis=1),
                            pltpu.roll(y_head, 1, axis=1))
        return (y_head * a_own + partner * a_partner) * r

    qg = qg_ref[...] * ATTN_Q_SCALE
    kg = kg_ref[...]
    aq_own, aq_partner = cos * qg[0:1], sin * qg[1:2]
    ak_own, ak_partner = cos * kg[0:1], sin * kg[1:2]
    for hd in range(N_HEADS):
        sl = slice(hd * HEAD_DIM, (hd + 1) * HEAD_DIM)
        q_ref[:, sl] = norm_rope(y[:, sl], aq_own, aq_partner).astype(BF16)
    for hd in range(N_KV_HEADS):
        sl = slice(hd * HEAD_DIM, (hd + 1) * HEAD_DIM)
        so = slice(Q_DIM + hd * HEAD_DIM, Q_DIM + (hd + 1) * HEAD_DIM)
        k_ref[:, sl] = norm_rope(y[:, so], ak_own, ak_partner).astype(BF16)
    vt_ref[...] = y[:, Q_DIM + KV_DIM:].T.astype(BF16)


def _attn_kernel(q_ref, k_ref, vt_ref, o_ref, s0_ref, s1_ref, acc_ref, *, tq, tk):
    n_chunks = vt_ref.shape[0]
    n_items = (q_ref.shape[0] // tq) * n_chunks
    m_cols = GROUP * tq

    def scores(t, dst_ref):
        q_start = pl.multiple_of((t // n_chunks) * tq, tq)
        k_start = pl.multiple_of((t % n_chunks) * tk, tk)
        qb = q_ref[pl.ds(q_start, tq), :]
        q = jnp.concatenate([qb[:, gi * HEAD_DIM:(gi + 1) * HEAD_DIM] for gi in range(GROUP)], axis=0)
        kc = k_ref[pl.ds(k_start, tk), :]
        st = lax.dot_general(kc, q, (((1,), (1,)), ((), ())), preferred_element_type=F32)
        dst_ref[...] = st
        return jnp.max(st, axis=0, keepdims=True)

    def consume(t, src_ref, m, alpha, l):
        p = jnp.exp2(src_ref[...] - m)
        l = alpha * l + jnp.sum(p, axis=0, keepdims=True)
        pv = jnp.dot(vt_ref[t % n_chunks], p.astype(BF16), preferred_element_type=F32)
        acc_ref[...] = alpha * acc_ref[...] + pv
        return l

    def finalize(t, l):
        o_t = acc_ref[...] / l
        q_start = pl.multiple_of((t // n_chunks) * tq, tq)
        for gi in range(GROUP):
            o_ref[pl.ds(q_start, tq), gi * HEAD_DIM:(gi + 1) * HEAD_DIM] = (
                o_t[:, gi * tq:(gi + 1) * tq].T.astype(BF16))

    def step(t, cur_ref, nxt_ref, carry, maybe_last):
        m, alpha, l = carry
        cmax = scores(t + 1, nxt_ref)
        l = consume(t, cur_ref, m, alpha, l)
        if not maybe_last:
            m_next = jnp.maximum(m, cmax)
            return m_next, jnp.exp2(m - m_next), l
        last = (t % n_chunks) == n_chunks - 1
        pl.when(last)(lambda: finalize(t, l))
        m_next = jnp.where(last, cmax, jnp.maximum(m, cmax))
        alpha_next = jnp.where(last, 0.0, jnp.exp2(m - m_next))
        return m_next, alpha_next, l

    def body(tt, carry):
        carry = step(2 * tt, s0_ref, s1_ref, carry, False)
        return step(2 * tt + 1, s1_ref, s0_ref, carry, True)

    acc_ref[...] = jnp.zeros_like(acc_ref)
    m0 = scores(0, s0_ref)
    zero = jnp.zeros((1, m_cols), F32)
    carry = lax.fori_loop(0, n_items // 2 - 1, body, (m0, zero, zero))
    m, alpha, l = step(n_items - 2, s0_ref, s1_ref, carry, False)
    l = consume(n_items - 1, s1_ref, m, alpha, l)
    finalize(n_items - 1, l)


def _attn_bounded_kernel(q_ref, k_ref, vt_ref, o_ref, p0_ref, p1_ref, acc_ref, *, tq, tk):
    n_chunks = vt_ref.shape[0]
    n_tiles = q_ref.shape[0] // tq
    bufs = (p0_ref, p1_ref)

    def probs(qi, j, dst_ref):
        q_start = pl.multiple_of(qi * tq, tq)
        qb = q_ref[pl.ds(q_start, tq), :]
        q = jnp.concatenate([qb[:, gi * HEAD_DIM:(gi + 1) * HEAD_DIM] for gi in range(GROUP)], axis=0)
        kc = k_ref[j * tk:(j + 1) * tk, :]
        st = lax.dot_general(kc, q, (((1,), (1,)), ((), ())), preferred_element_type=F32)
        p = jnp.exp2(st)
        dst_ref[...] = p.astype(BF16)
        return jnp.sum(p, axis=0, keepdims=True)

    def tile(qi, lsum, has_next):
        l = None
        for j in range(n_chunks):
            cur_ref, nxt_ref = bufs[j % 2], bufs[(j + 1) % 2]
            if j + 1 < n_chunks:
                lsum_next = probs(qi, j + 1, nxt_ref)
            elif has_next:
                lsum_next = probs(qi + 1, 0, nxt_ref)
            else:
                lsum_next = None
            pv = jnp.dot(vt_ref[j], cur_ref[...], preferred_element_type=F32)
            if j == 0:
                acc_ref[...] = pv
            else:
                acc_ref[...] += pv
            l = lsum if l is None else l + lsum
            lsum = lsum_next
        o_t = acc_ref[...] / l
        q_start = pl.multiple_of(qi * tq, tq)
        for gi in range(GROUP):
            o_ref[pl.ds(q_start, tq), gi * HEAD_DIM:(gi + 1) * HEAD_DIM] = (
                o_t[:, gi * tq:(gi + 1) * tq].T.astype(BF16))
        return lsum

    lsum = probs(0, 0, p0_ref)
    lsum = lax.fori_loop(0, n_tiles - 1, lambda qi, ls: tile(qi, ls, True), lsum)
    tile(n_tiles - 1, lsum, False)


def _rope_tables(seq_len):
    f32 = np.float32
    rows = seq_len // GRID_W
    row = np.repeat(np.arange(rows, dtype=f32), GRID_W)
    col = np.tile(np.arange(GRID_W, dtype=f32), rows)
    inv = (f32(ROPE_THETA) ** (-np.arange(0, AXIS_ROT_DIM, 2, dtype=f32) / f32(AXIS_ROT_DIM))).astype(f32)
    ang = np.concatenate([row[:, None] * inv, col[:, None] * inv], axis=-1)
    cos = np.repeat(np.cos(ang), 2, axis=-1)
    sin = np.repeat(np.sin(ang), 2, axis=-1) * np.tile(np.array([-1.0, 1.0], f32), HEAD_DIM // 2)
    return jnp.asarray(cos, F32), jnp.asarray(sin, F32)


def _attention_qkv(x, g, w_qkv_bf16, layer, q_gain, k_gain):
    b, s, d = x.shape
    t = b * s
    tq = ATTN_Q_TILE
    tk = ATTN_K_TILE
    tm = QKV_TOKEN_TILE
    per_chunk = tk // tm
    assert s % (2 * tk) == 0 and s % tq == 0, "attention loops take key chunks in pairs"
    cos, sin = _rope_tables(s)
    nblk = s // tm
    q, k, vt = pl.pallas_call(
        _qkv_kernel,
        grid=(t // tm,),
        in_specs=[
            pl.BlockSpec((tm, d), lambda i: (i, 0)),
            pl.BlockSpec((1, d), lambda i: (0, 0)),
            _layer_spec(w_qkv_bf16, layer),
            pl.BlockSpec((2, HEAD_DIM), lambda i: (0, 0)),
            pl.BlockSpec((2, HEAD_DIM), lambda i: (0, 0)),
            pl.BlockSpec((tm, HEAD_DIM), lambda i: (i % nblk, 0)),
            pl.BlockSpec((tm, HEAD_DIM), lambda i: (i % nblk, 0)),
        ],
        out_specs=[
            pl.BlockSpec((tm, Q_DIM), lambda i: (i, 0)),
            pl.BlockSpec((tm, KV_DIM), lambda i: (i, 0)),
            pl.BlockSpec((None, KV_DIM, tm), lambda i: (i // per_chunk, 0, i % per_chunk)),
        ],
        out_shape=[
            jax.ShapeDtypeStruct((t, Q_DIM), BF16),
            jax.ShapeDtypeStruct((t, KV_DIM), BF16),
            jax.ShapeDtypeStruct((t // tk, KV_DIM, tk), BF16),
        ],
        compiler_params=pltpu.CompilerParams(
            dimension_semantics=("parallel",), vmem_limit_bytes=VMEM_LIMIT),
        name="qkv_proj",
    )(x.reshape(t, d), g.reshape(1, d), w_qkv_bf16, q_gain, k_gain, cos, sin)

    q_bytes = s * GROUP * HEAD_DIM * jnp.dtype(BF16).itemsize
    q_mode = dict(pipeline_mode=pl.Buffered(1)) if 2 * q_bytes > ATTN_Q_PREFETCH_MAX_BYTES else {}

    def attention_call(body, buf_dtype, name):
        return pl.pallas_call(
            functools.partial(body, tq=tq, tk=tk),
            grid=(b, N_KV_HEADS),
            in_specs=[
                pl.BlockSpec((None, s, GROUP * HEAD_DIM), lambda bi, hi: (bi, 0, hi), **q_mode),
                pl.BlockSpec((None, s, HEAD_DIM), lambda bi, hi: (bi, 0, hi)),
                pl.BlockSpec((None, s // tk, HEAD_DIM, tk), lambda bi, hi: (bi, 0, hi, 0)),
            ],
            out_specs=pl.BlockSpec((None, s, GROUP * HEAD_DIM), lambda bi, hi: (bi, 0, hi)),
            out_shape=jax.ShapeDtypeStruct((b, s, Q_DIM), BF16),
            scratch_shapes=[
                pltpu.VMEM((tk, GROUP * tq), buf_dtype),
                pltpu.VMEM((tk, GROUP * tq), buf_dtype),
                pltpu.VMEM((HEAD_DIM, GROUP * tq), F32),
            ],
            compiler_params=pltpu.CompilerParams(
                dimension_semantics=("parallel", "parallel"), vmem_limit_bytes=VMEM_LIMIT),
            name=name,
        )

    score_bound = (HEAD_DIM * ATTN_Q_SCALE) * jnp.max(jnp.abs(q_gain)) * jnp.max(jnp.abs(k_gain))
    o = lax.cond(
        score_bound <= ATTN_BOUNDED_MAX_LOG2_SCORE,
        attention_call(_attn_bounded_kernel, BF16, "attention_bounded"),
        attention_call(_attn_kernel, F32, "attention"),
        q.reshape(b, s, Q_DIM), k.reshape(b, s, KV_DIM), vt.reshape(b, s // tk, KV_DIM, tk))
    return o.reshape(t, Q_DIM)


def _ffn_kernel(*refs, with_attn):
    if with_attn:
        x_ref, a_ref, wo_ref, g_ref, wg_ref, wu_ref, wd_ref, o_ref = refs
        x = x_ref[...] + jnp.dot(a_ref[...], wo_ref[...], preferred_element_type=F32)
    else:
        x_ref, delta_ref, g_ref, wg_ref, wu_ref, wd_ref, o_ref = refs
        x = x_ref[...] + delta_ref[...].astype(F32)
    r = lax.rsqrt(jnp.mean(x * x, axis=-1, keepdims=True) + EPS)
    xg = (x * g_ref[...]).astype(BF16)
    d_ff = wg_ref.shape[1]
    out = x
    for c0 in range(0, d_ff, FFN_HIDDEN_CHUNK):
        c1 = min(c0 + FFN_HIDDEN_CHUNK, d_ff)
        gate = jnp.dot(xg, wg_ref[:, c0:c1], preferred_element_type=F32) * r
        up = jnp.dot(xg, wu_ref[:, c0:c1], preferred_element_type=F32) * r
        act = (gate * jax.nn.sigmoid(gate) * up).astype(BF16)
        out = out + jnp.dot(act, wd_ref[c0:c1, :], preferred_element_type=F32)
    o_ref[...] = out


def _ffn_layer(x2d, g, layer, wg, wu, wd, attn=None, delta=None):
    assert (attn is None) != (delta is None)
    t, d = x2d.shape
    tm = FFN_TOKEN_TILE
    resident = functools.partial(pl.BlockSpec, index_map=lambda i: (0, 0), pipeline_mode=pl.Buffered(1))
    row = lambda width: pl.BlockSpec((tm, width), lambda i: (i, 0))
    in_specs = [row(d)]
    args = [x2d]
    if attn is not None:
        a2d, wo, attn_layer = attn
        in_specs += [row(Q_DIM), _layer_spec(wo, attn_layer, single_buffer=True)]
        args += [a2d, wo]
    else:
        in_specs += [row(d)]
        args += [delta]
    in_specs += [resident((1, d))] + [_layer_spec(w, layer, single_buffer=True) for w in (wg, wu, wd)]
    args += [g.reshape(1, d), wg, wu, wd]
    return pl.pallas_call(
        functools.partial(_ffn_kernel, with_attn=attn is not None),
        grid=(t // tm,),
        in_specs=in_specs,
        out_specs=row(d),
        out_shape=jax.ShapeDtypeStruct((t, d), F32),
        compiler_params=pltpu.CompilerParams(
            dimension_semantics=("parallel",), vmem_limit_bytes=VMEM_LIMIT),
        name="swiglu",
    )(*args)


def _with_pair_partner(gain):
    n_layers = gain.shape[0]
    partner = gain.reshape(n_layers, HEAD_DIM // 2, 2)[:, :, ::-1].reshape(n_layers, HEAD_DIM)
    return jnp.stack([gain, partner], axis=1)


def kernel(x_prompt, x_sample, norm_mix, norm_ffn, fourier_w, attn_w_qkv, attn_q_norm, attn_k_norm,
           attn_w_o, ffn_w_gate, ffn_w_up, ffn_w_down):
    depth = norm_mix.shape[0]
    fw = fourier_w.astype(BF16)
    wqkv = attn_w_qkv.astype(BF16)
    qn = _with_pair_partner(attn_q_norm)
    kn = _with_pair_partner(attn_k_norm)
    wo = attn_w_o.astype(BF16)
    wg = ffn_w_gate.astype(BF16)
    wu = ffn_w_up.astype(BF16)
    wd = ffn_w_down.astype(BF16)

    def run_trunk(x):
        b, s, d = x.shape
        for i in range(depth):
            j = i // 2
            if i % 2 == 0:
                delta = _fourier_layer(x, norm_mix[i], fw, j)
                x2d = _ffn_layer(x.reshape(b * s, d), norm_ffn[i], i, wg, wu, wd, delta=delta)
            else:
                a2d = _attention_qkv(x, norm_mix[i], wqkv, j, qn[j], kn[j])
                x2d = _ffn_layer(x.reshape(b * s, d), norm_ffn[i], i, wg, wu, wd, attn=(a2d, wo, j))
            x = x2d.reshape(b, s, d)
        return x

    return (run_trunk(x_prompt), run_trunk(x_sample))
```

```python
import functools

import numpy as np
import jax
import jax.numpy as jnp
from jax import lax
from jax.experimental import pallas as pl
from jax.experimental.pallas import tpu as pltpu

D_MODEL = 1024
HEAD_DIM = 128
N_HEADS = D_MODEL // HEAD_DIM
N_KV_HEADS = 2
GROUP = N_HEADS // N_KV_HEADS
Q_DIM = N_HEADS * HEAD_DIM
KV_DIM = N_KV_HEADS * HEAD_DIM
QKV_DIM = Q_DIM + 2 * KV_DIM
GRID_W = 64
AXIS_ROT_DIM = HEAD_DIM // 2
ROPE_THETA = 10000.0
N_FGROUPS = 8
FGROUP_DIM = D_MODEL // N_FGROUPS
EPS = 1e-6

ATTN_Q_SCALE = float(HEAD_DIM ** -0.5 * np.log2(np.e))
ATTN_BOUNDED_MAX_LOG2_SCORE = 60.0

F32 = jnp.float32
BF16 = jnp.bfloat16
FOURIER_DELTA_DTYPE = BF16

VMEM_LIMIT = 56 * 1024 * 1024
SUBLANE_BYTES = 32
FOURIER_N1 = 64
FFN_TOKEN_TILE = 1024
FFN_HIDDEN_CHUNK = 768
QKV_TOKEN_TILE = 1024
ATTN_Q_TILE = 256
ATTN_K_TILE = 1024
ATTN_Q_PREFETCH_MAX_BYTES = VMEM_LIMIT // 4
ATTN_ITEMS_PER_BLOCK = 8


def _rms(x, g):
    ms = jnp.mean(x * x, axis=-1, keepdims=True)
    return x * lax.rsqrt(ms + EPS) * g


def _layer_spec(stacked, layer, single_buffer=False):
    zeros = (0,) * (stacked.ndim - 1)
    mode = dict(pipeline_mode=pl.Buffered(1)) if single_buffer else {}
    return pl.BlockSpec((None,) + stacked.shape[1:], lambda *_: (layer,) + zeros, **mode)


def _dft_cos_sin(n):
    k = np.arange(n)
    ang = 2.0 * np.pi * ((k[:, None] * k[None, :]) % n) / n
    return np.cos(ang), np.sin(ang)


def _fourier_a_kernel(x_ref, g_ref, f1_ref, cs_ref, yr_ref, yi_ref, *, n1, ts2):
    d = D_MODEL
    h = _rms(x_ref[...], g_ref[...])
    ht = jnp.swapaxes(h.astype(BF16), 0, 1)
    z = [jnp.dot(f1_ref[j], ht[j], preferred_element_type=F32) for j in range(ts2)]
    zr = jnp.concatenate([zj[:n1] for zj in z], axis=0).astype(BF16)
    zi = jnp.concatenate([zj[n1:] for zj in z], axis=0).astype(BF16)
    cs = cs_ref[...]
    yr, yi = [], []
    for gi in range(N_FGROUPS):
        sl = slice(gi * FGROUP_DIM, (gi + 1) * FGROUP_DIM)
        y = jnp.dot(jnp.concatenate([zr[:, sl], zi[:, sl]], axis=1), cs, preferred_element_type=F32)
        yr.append(y[:, :FGROUP_DIM])
        yi.append(y[:, FGROUP_DIM:])
    yr = jnp.concatenate(yr, axis=1).astype(BF16).reshape(ts2, n1, d)
    yi = jnp.concatenate(yi, axis=1).astype(BF16).reshape(ts2, n1, d)
    yr_ref[...] = jnp.swapaxes(yr, 0, 1)
    yi_ref[...] = jnp.swapaxes(yi, 0, 1)


def _fourier_c_kernel(yr_ref, yi_ref, f2_ref, w_ref, o_ref, *, n2, tk1, scale):
    d = D_MODEL
    f2 = f2_ref[...]
    mixed = [jnp.dot(f2, jnp.concatenate([yr_ref[j], yi_ref[j]], axis=0), preferred_element_type=F32)
             for j in range(tk1)]
    mixed = (jnp.concatenate(mixed, axis=0) * scale).astype(BF16)
    delta = jnp.dot(mixed, w_ref[...], preferred_element_type=F32).astype(o_ref.dtype)
    o_ref[...] = jnp.swapaxes(delta.reshape(tk1, n2, d), 0, 1)


def _fourier_tables(seq_len):
    n1 = FOURIER_N1
    n2 = seq_len // n1
    s2_, k1_, s1_ = np.arange(n2)[:, None, None], np.arange(n1)[None, :, None], np.arange(n1)[None, None, :]
    ang = 2.0 * np.pi * ((k1_ * s1_ * n2 + k1_ * s2_) % seq_len) / seq_len
    f1 = np.concatenate([np.cos(ang), -np.sin(ang)], axis=1)
    c2, s2 = _dft_cos_sin(n2)
    f2 = np.concatenate([c2, s2], axis=1)
    cc, sc = _dft_cos_sin(FGROUP_DIM)
    cs = np.block([[cc, -sc], [sc, cc]])
    return n1, n2, f1, f2, cs


def _fourier_layer(x, g, w_bf16, layer):
    b, s, d = x.shape
    n1, n2, f1, f2, cs = _fourier_tables(s)
    ts2 = SUBLANE_BYTES // jnp.dtype(BF16).itemsize
    tk1 = SUBLANE_BYTES // jnp.dtype(FOURIER_DELTA_DTYPE).itemsize
    assert d == D_MODEL and s == n1 * n2 and n2 % ts2 == 0 and n1 % tk1 == 0
    f1 = jnp.asarray(f1, BF16)
    f2 = jnp.asarray(f2, BF16)
    cs = jnp.asarray(cs, BF16)
    g2 = g.reshape(1, d)

    y_shape = jax.ShapeDtypeStruct((b, n1, n2, d), BF16)
    yr, yi = pl.pallas_call(
        functools.partial(_fourier_a_kernel, n1=n1, ts2=ts2),
        grid=(b, n2 // ts2),
        in_specs=[
            pl.BlockSpec((None, n1, ts2, d), lambda bi, i: (bi, 0, i, 0)),
            pl.BlockSpec((1, d), lambda bi, i: (0, 0)),
            pl.BlockSpec((ts2, 2 * n1, n1), lambda bi, i: (i, 0, 0)),
            pl.BlockSpec((2 * FGROUP_DIM, 2 * FGROUP_DIM), lambda bi, i: (0, 0)),
        ],
        out_specs=[
            pl.BlockSpec((None, n1, ts2, d), lambda bi, i: (bi, 0, i, 0)),
            pl.BlockSpec((None, n1, ts2, d), lambda bi, i: (bi, 0, i, 0)),
        ],
        out_shape=[y_shape, y_shape],
        compiler_params=pltpu.CompilerParams(
            dimension_semantics=("parallel", "parallel"), vmem_limit_bytes=VMEM_LIMIT),
        name="fourier_a",
    )(x.reshape(b, n1, n2, d), g2, f1, cs)

    scale = float(1.0 / np.sqrt(s * FGROUP_DIM))
    delta = pl.pallas_call(
        functools.partial(_fourier_c_kernel, n2=n2, tk1=tk1, scale=scale),
        grid=(b, n1 // tk1),
        in_specs=[
            pl.BlockSpec((None, tk1, n2, d), lambda bi, i: (bi, i, 0, 0)),
            pl.BlockSpec((None, tk1, n2, d), lambda bi, i: (bi, i, 0, 0)),
            pl.BlockSpec((n2, 2 * n2), lambda bi, i: (0, 0)),
            _layer_spec(w_bf16, layer),
        ],
        out_specs=pl.BlockSpec((None, n2, tk1, d), lambda bi, i: (bi, 0, i, 0)),
        out_shape=jax.ShapeDtypeStruct((b, n2, n1, d), FOURIER_DELTA_DTYPE),
        compiler_params=pltpu.CompilerParams(
            dimension_semantics=("parallel", "parallel"), vmem_limit_bytes=VMEM_LIMIT),
        name="fourier_c",
    )(yr, yi, f2, w_bf16)
    return delta.reshape(b * s, d)


def _qkv_kernel(x_ref, g_ref, w_ref, qg_ref, kg_ref, cos_ref, sin_ref, q_ref, k_ref, vt_ref):
    h = _rms(x_ref[...], g_ref[...]).astype(BF16)
    y = jnp.dot(h, w_ref[...], preferred_element_type=F32)
    cos = cos_ref[...]
    sin = sin_ref[...]
    ones = jnp.ones((2 * HEAD_DIM, HEAD_DIM), BF16)
    even_lane = (lax.broadcasted_iota(jnp.int32, (1, HEAD_DIM), 1) % 2) == 0

    def norm_rope(y_head, a_own, a_partner):
        sq = y_head * y_head
        hi = sq.astype(BF16)
        lo = (sq - hi.astype(F32)).astype(BF16)
        ss = jnp.dot(jnp.concatenate([hi, lo], axis=1), ones, preferred_element_type=F32)
        r = lax.rsqrt(ss * (1.0 / HEAD_DIM) + EPS)
        partner = jnp.where(even_lane, pltpu.roll(y_head, HEAD_DIM - 1, axis=1),
                            pltpu.roll(y_head, 1, axis=1))
        return (y_head * a_own + partner * a_partner) * r

    qg = qg_ref[...] * ATTN_Q_SCALE
    kg = kg_ref[...]
    aq_own, aq_partner = cos * qg[0:1], sin * qg[1:2]
    ak_own, ak_partner = cos * kg[0:1], sin * kg[1:2]
    for hd in range(N_HEADS):
        sl = slice(hd * HEAD_DIM, (hd + 1) * HEAD_DIM)
        q_ref[:, sl] = norm_rope(y[:, sl], aq_own, aq_partner).astype(BF16)
    for hd in range(N_KV_HEADS):
        sl = slice(hd * HEAD_DIM, (hd + 1) * HEAD_DIM)
        so = slice(Q_DIM + hd * HEAD_DIM, Q_DIM + (hd + 1) * HEAD_DIM)
        k_ref[:, sl] = norm_rope(y[:, so], ak_own, ak_partner).astype(BF16)
    vt_ref[...] = y[:, Q_DIM + KV_DIM:].T.astype(BF16)


def _attn_kernel(q_ref, k_ref, vt_ref, o_ref, s0_ref, s1_ref, acc_ref, *, tq, tk):
    n_chunks = vt_ref.shape[0]
    n_items = (q_ref.shape[0] // tq) * n_chunks
    m_cols = GROUP * tq

    def scores(t, dst_ref):
        q_start = pl.multiple_of((t // n_chunks) * tq, tq)
        k_start = pl.multiple_of((t % n_chunks) * tk, tk)
        qb = q_ref[pl.ds(q_start, tq), :]
        q = jnp.concatenate([qb[:, gi * HEAD_DIM:(gi + 1) * HEAD_DIM] for gi in range(GROUP)], axis=0)
        kc = k_ref[pl.ds(k_start, tk), :]
        st = lax.dot_general(kc, q, (((1,), (1,)), ((), ())), preferred_element_type=F32)
        dst_ref[...] = st
        return jnp.max(st, axis=0, keepdims=True)

    def consume(t, src_ref, m, alpha, l):
        p = jnp.exp2(src_ref[...] - m)
        l = alpha * l + jnp.sum(p, axis=0, keepdims=True)
        pv = jnp.dot(vt_ref[t % n_chunks], p.astype(BF16), preferred_element_type=F32)
        acc_ref[...] = alpha * acc_ref[...] + pv
        return l

    def finalize(t, l):
        o_t = acc_ref[...] / l
        q_start = pl.multiple_of((t // n_chunks) * tq, tq)
        for gi in range(GROUP):
            o_ref[pl.ds(q_start, tq), gi * HEAD_DIM:(gi + 1) * HEAD_DIM] = (
                o_t[:, gi * tq:(gi + 1) * tq].T.astype(BF16))

    def step(t, cur_ref, nxt_ref, carry, maybe_last):
        m, alpha, l = carry
        cmax = scores(t + 1, nxt_ref)
        l = consume(t, cur_ref, m, alpha, l)
        if not maybe_last:
            m_next = jnp.maximum(m, cmax)
            return m_next, jnp.exp2(m - m_next), l
        last = (t % n_chunks) == n_chunks - 1
        pl.when(last)(lambda: finalize(t, l))
        m_next = jnp.where(last, cmax, jnp.maximum(m, cmax))
        alpha_next = jnp.where(last, 0.0, jnp.exp2(m - m_next))
        return m_next, alpha_next, l

    def body(tt, carry):
        carry = step(2 * tt, s0_ref, s1_ref, carry, False)
        return step(2 * tt + 1, s1_ref, s0_ref, carry, True)

    acc_ref[...] = jnp.zeros_like(acc_ref)
    m0 = scores(0, s0_ref)
    zero = jnp.zeros((1, m_cols), F32)
    carry = lax.fori_loop(0, n_items // 2 - 1, body, (m0, zero, zero))
    m, alpha, l = step(n_items - 2, s0_ref, s1_ref, carry, False)
    l = consume(n_items - 1, s1_ref, m, alpha, l)
    finalize(n_items - 1, l)


def _attn_bounded_kernel(q_ref, k_ref, vt_ref, o_ref, p0_ref, p1_ref, acc_ref, *, tq, tk):
    n_chunks = vt_ref.shape[0]
    n_tiles = q_ref.shape[0] // tq
    tiles_per_block = max(1, ATTN_ITEMS_PER_BLOCK // n_chunks)
    assert n_tiles % tiles_per_block == 0
    n_blocks = n_tiles // tiles_per_block
    bufs = (p0_ref, p1_ref)

    def probs(qi, j, dst_ref):
        q_start = pl.multiple_of(qi * tq, tq)
        qb = q_ref[pl.ds(q_start, tq), :]
        q = jnp.concatenate([qb[:, gi * HEAD_DIM:(gi + 1) * HEAD_DIM] for gi in range(GROUP)], axis=0)
        kc = k_ref[j * tk:(j + 1) * tk, :]
        st = lax.dot_general(kc, q, (((1,), (1,)), ((), ())), preferred_element_type=F32)
        p = jnp.exp2(st)
        dst_ref[...] = p.astype(BF16)
        return jnp.sum(p, axis=0, keepdims=True)

    def tile(qi, lsum, has_next):
        l = None
        for j in range(n_chunks):
            cur_ref, nxt_ref = bufs[j % 2], bufs[(j + 1) % 2]
            if j + 1 < n_chunks:
                lsum_next = probs(qi, j + 1, nxt_ref)
            elif has_next:
                lsum_next = probs(qi + 1, 0, nxt_ref)
            else:
                lsum_next = None
            pv = jnp.dot(vt_ref[j], cur_ref[...], preferred_element_type=F32)
            if j == 0:
                acc_ref[...] = pv
            else:
                acc_ref[...] += pv
            l = lsum if l is None else l + lsum
            lsum = lsum_next
        o_t = acc_ref[...] / l
        q_start = pl.multiple_of(qi * tq, tq)
        for gi in range(GROUP):
            o_ref[pl.ds(q_start, tq), gi * HEAD_DIM:(gi + 1) * HEAD_DIM] = (
                o_t[:, gi * tq:(gi + 1) * tq].T.astype(BF16))
        return lsum

    def block(bi, lsum, has_next):
        for ti in range(tiles_per_block):
            lsum = tile(bi * tiles_per_block + ti, lsum, has_next or ti + 1 < tiles_per_block)
        return lsum

    lsum = probs(0, 0, p0_ref)
    lsum = lax.fori_loop(0, n_blocks - 1, lambda bi, ls: block(bi, ls, True), lsum)
    block(n_blocks - 1, lsum, False)


def _rope_tables(seq_len):
    f32 = np.float32
    rows = seq_len // GRID_W
    row = np.repeat(np.arange(rows, dtype=f32), GRID_W)
    col = np.tile(np.arange(GRID_W, dtype=f32), rows)
    inv = (f32(ROPE_THETA) ** (-np.arange(0, AXIS_ROT_DIM, 2, dtype=f32) / f32(AXIS_ROT_DIM))).astype(f32)
    ang = np.concatenate([row[:, None] * inv, col[:, None] * inv], axis=-1)
    cos = np.repeat(np.cos(ang), 2, axis=-1)
    sin = np.repeat(np.sin(ang), 2, axis=-1) * np.tile(np.array([-1.0, 1.0], f32), HEAD_DIM // 2)
    return jnp.asarray(cos, F32), jnp.asarray(sin, F32)


def _attention_qkv(x, g, w_qkv_bf16, layer, q_gain, k_gain):
    b, s, d = x.shape
    t = b * s
    tq = ATTN_Q_TILE
    tk = ATTN_K_TILE
    tm = QKV_TOKEN_TILE
    per_chunk = tk // tm
    assert s % (2 * tk) == 0 and s % tq == 0, "attention loops take key chunks in pairs"
    cos, sin = _rope_tables(s)
    nblk = s // tm
    q, k, vt = pl.pallas_call(
        _qkv_kernel,
        grid=(t // tm,),
        in_specs=[
            pl.BlockSpec((tm, d), lambda i: (i, 0)),
            pl.BlockSpec((1, d), lambda i: (0, 0)),
            _layer_spec(w_qkv_bf16, layer),
            pl.BlockSpec((2, HEAD_DIM), lambda i: (0, 0)),
            pl.BlockSpec((2, HEAD_DIM), lambda i: (0, 0)),
            pl.BlockSpec((tm, HEAD_DIM), lambda i: (i % nblk, 0)),
            pl.BlockSpec((tm, HEAD_DIM), lambda i: (i % nblk, 0)),
        ],
        out_specs=[
            pl.BlockSpec((tm, Q_DIM), lambda i: (i, 0)),
            pl.BlockSpec((tm, KV_DIM), lambda i: (i, 0)),
            pl.BlockSpec((None, KV_DIM, tm), lambda i: (i // per_chunk, 0, i % per_chunk)),
        ],
        out_shape=[
            jax.ShapeDtypeStruct((t, Q_DIM), BF16),
            jax.ShapeDtypeStruct((t, KV_DIM), BF16),
            jax.ShapeDtypeStruct((t // tk, KV_DIM, tk), BF16),
        ],
        compiler_params=pltpu.CompilerParams(
            dimension_semantics=("parallel",), vmem_limit_bytes=VMEM_LIMIT),
        name="qkv_proj",
    )(x.reshape(t, d), g.reshape(1, d), w_qkv_bf16, q_gain, k_gain, cos, sin)

    q_bytes = s * GROUP * HEAD_DIM * jnp.dtype(BF16).itemsize
    q_mode = dict(pipeline_mode=pl.Buffered(1)) if 2 * q_bytes > ATTN_Q_PREFETCH_MAX_BYTES else {}

    def attention_call(body, buf_dtype, name):
        return pl.pallas_call(
            functools.partial(body, tq=tq, tk=tk),
            grid=(b, N_KV_HEADS),
            in_specs=[
                pl.BlockSpec((None, s, GROUP * HEAD_DIM), lambda bi, hi: (bi, 0, hi), **q_mode),
                pl.BlockSpec((None, s, HEAD_DIM), lambda bi, hi: (bi, 0, hi)),
                pl.BlockSpec((None, s // tk, HEAD_DIM, tk), lambda bi, hi: (bi, 0, hi, 0)),
            ],
            out_specs=pl.BlockSpec((None, s, GROUP * HEAD_DIM), lambda bi, hi: (bi, 0, hi)),
            out_shape=jax.ShapeDtypeStruct((b, s, Q_DIM), BF16),
            scratch_shapes=[
                pltpu.VMEM((tk, GROUP * tq), buf_dtype),
                pltpu.VMEM((tk, GROUP * tq), buf_dtype),
                pltpu.VMEM((HEAD_DIM, GROUP * tq), F32),
            ],
            compiler_params=pltpu.CompilerParams(
                dimension_semantics=("parallel", "parallel"), vmem_limit_bytes=VMEM_LIMIT),
            name=name,
        )

    score_bound = (HEAD_DIM * ATTN_Q_SCALE) * jnp.max(jnp.abs(q_gain)) * jnp.max(jnp.abs(k_gain))
    o = lax.cond(
        score_bound <= ATTN_BOUNDED_MAX_LOG2_SCORE,
        attention_call(_attn_bounded_kernel, BF16, "attention_bounded"),
        attention_call(_attn_kernel, F32, "attention"),
        q.reshape(b, s, Q_DIM), k.reshape(b, s, KV_DIM), vt.reshape(b, s // tk, KV_DIM, tk))
    return o.reshape(t, Q_DIM)


def _ffn_kernel(*refs, with_attn):
    if with_attn:
        x_ref, a_ref, wo_ref, g_ref, wg_ref, wu_ref, wd_ref, o_ref = refs
        x = x_ref[...] + jnp.dot(a_ref[...], wo_ref[...], preferred_element_type=F32)
    else:
        x_ref, delta_ref, g_ref, wg_ref, wu_ref, wd_ref, o_ref = refs
        x = x_ref[...] + delta_ref[...].astype(F32)
    r = lax.rsqrt(jnp.mean(x * x, axis=-1, keepdims=True) + EPS)
    xg = (x * g_ref[...]).astype(BF16)
    d_ff = wg_ref.shape[1]
    out = x
    for c0 in range(0, d_ff, FFN_HIDDEN_CHUNK):
        c1 = min(c0 + FFN_HIDDEN_CHUNK, d_ff)
        gate = jnp.dot(xg, wg_ref[:, c0:c1], preferred_element_type=F32) * r
        up = jnp.dot(xg, wu_ref[:, c0:c1], preferred_element_type=F32) * r
        act = (gate * jax.nn.sigmoid(gate) * up).astype(BF16)
        out = out + jnp.dot(act, wd_ref[c0:c1, :], preferred_element_type=F32)
    o_ref[...] = out


def _ffn_layer(x2d, g, layer, wg, wu, wd, attn=None, delta=None):
    assert (attn is None) != (delta is None)
    t, d = x2d.shape
    tm = FFN_TOKEN_TILE
    resident = functools.partial(pl.BlockSpec, index_map=lambda i: (0, 0), pipeline_mode=pl.Buffered(1))
    row = lambda width: pl.BlockSpec((tm, width), lambda i: (i, 0))
    in_specs = [row(d)]
    args = [x2d]
    if attn is not None:
        a2d, wo, attn_layer = attn
        in_specs += [row(Q_DIM), _layer_spec(wo, attn_layer, single_buffer=True)]
        args += [a2d, wo]
    else:
        in_specs += [row(d)]
        args += [delta]
    in_specs += [resident((1, d))] + [_layer_spec(w, layer, single_buffer=True) for w in (wg, wu, wd)]
    args += [g.reshape(1, d), wg, wu, wd]
    return pl.pallas_call(
        functools.partial(_ffn_kernel, with_attn=attn is not None),
        grid=(t // tm,),
        in_specs=in_specs,
        out_specs=row(d),
        out_shape=jax.ShapeDtypeStruct((t, d), F32),
        compiler_params=pltpu.CompilerParams(
            dimension_semantics=("parallel",), vmem_limit_bytes=VMEM_LIMIT),
        name="swiglu",
    )(*args)


def _with_pair_partner(gain):
    n_layers = gain.shape[0]
    partner = gain.reshape(n_layers, HEAD_DIM // 2, 2)[:, :, ::-1].reshape(n_layers, HEAD_DIM)
    return jnp.stack([gain, partner], axis=1)


def kernel(x_prompt, x_sample, norm_mix, norm_ffn, fourier_w, attn_w_qkv, attn_q_norm, attn_k_norm,
           attn_w_o, ffn_w_gate, ffn_w_up, ffn_w_down):
    depth = norm_mix.shape[0]
    fw = fourier_w.astype(BF16)
    wqkv = attn_w_qkv.astype(BF16)
    qn = _with_pair_partner(attn_q_norm)
    kn = _with_pair_partner(attn_k_norm)
    wo = attn_w_o.astype(BF16)
    wg = ffn_w_gate.astype(BF16)
    wu = ffn_w_up.astype(BF16)
    wd = ffn_w_down.astype(BF16)

    def run_trunk(x):
        b, s, d = x.shape
        for i in range(depth):
            j = i // 2
            if i % 2 == 0:
                delta = _fourier_layer(x, norm_mix[i], fw, j)
                x2d = _ffn_layer(x.reshape(b * s, d), norm_ffn[i], i, wg, wu, wd, delta=delta)
            else:
                a2d = _attention_qkv(x, norm_mix[i], wqkv, j, qn[j], kn[j])
                x2d = _ffn_layer(x.reshape(b * s, d), norm_ffn[i], i, wg, wu, wd, attn=(a2d, wo, j))
            x = x2d.reshape(b, s, d)
        return x

    return (run_trunk(x_prompt), run_trunk(x_sample))
```

```python
import functools

import numpy as np
import jax
import jax.numpy as jnp
from jax import lax
from jax.experimental import pallas as pl
from jax.experimental.pallas import tpu as pltpu

D_MODEL = 1024
HEAD_DIM = 128
N_HEADS = D_MODEL // HEAD_DIM
N_KV_HEADS = 2
GROUP = N_HEADS // N_KV_HEADS
Q_DIM = N_HEADS * HEAD_DIM
KV_DIM = N_KV_HEADS * HEAD_DIM
QKV_DIM = Q_DIM + 2 * KV_DIM
GRID_W = 64
AXIS_ROT_DIM = HEAD_DIM // 2
ROPE_THETA = 10000.0
N_FGROUPS = 8
FGROUP_DIM = D_MODEL // N_FGROUPS
EPS = 1e-6

ATTN_Q_SCALE = float(HEAD_DIM ** -0.5 * np.log2(np.e))
ATTN_BOUNDED_MAX_LOG2_SCORE = 60.0

F32 = jnp.float32
BF16 = jnp.bfloat16
FOURIER_DELTA_DTYPE = BF16

VMEM_LIMIT = 56 * 1024 * 1024
SUBLANE_BYTES = 32
FOURIER_N1 = 64
FFN_TOKEN_TILE = 1024
FFN_HIDDEN_CHUNK = 768
QKV_TOKEN_TILE = 1024
ATTN_Q_TILE = 256
ATTN_K_TILE = 1024
ATTN_Q_PREFETCH_MAX_BYTES = VMEM_LIMIT // 4
ATTN_ITEMS_PER_BLOCK = 16


def _rms(x, g):
    ms = jnp.mean(x * x, axis=-1, keepdims=True)
    return x * lax.rsqrt(ms + EPS) * g


def _layer_spec(stacked, layer, single_buffer=False):
    zeros = (0,) * (stacked.ndim - 1)
    mode = dict(pipeline_mode=pl.Buffered(1)) if single_buffer else {}
    return pl.BlockSpec((None,) + stacked.shape[1:], lambda *_: (layer,) + zeros, **mode)


def _dft_cos_sin(n):
    k = np.arange(n)
    ang = 2.0 * np.pi * ((k[:, None] * k[None, :]) % n) / n
    return np.cos(ang), np.sin(ang)


def _fourier_a_kernel(x_ref, g_ref, f1_ref, cs_ref, yr_ref, yi_ref, *, n1, ts2):
    d = D_MODEL
    h = _rms(x_ref[...], g_ref[...])
    ht = jnp.swapaxes(h.astype(BF16), 0, 1)
    z = [jnp.dot(f1_ref[j], ht[j], preferred_element_type=F32) for j in range(ts2)]
    zr = jnp.concatenate([zj[:n1] for zj in z], axis=0).astype(BF16)
    zi = jnp.concatenate([zj[n1:] for zj in z], axis=0).astype(BF16)
    cs = cs_ref[...]
    yr, yi = [], []
    for gi in range(N_FGROUPS):
        sl = slice(gi * FGROUP_DIM, (gi + 1) * FGROUP_DIM)
        y = jnp.dot(jnp.concatenate([zr[:, sl], zi[:, sl]], axis=1), cs, preferred_element_type=F32)
        yr.append(y[:, :FGROUP_DIM])
        yi.append(y[:, FGROUP_DIM:])
    yr = jnp.concatenate(yr, axis=1).astype(BF16).reshape(ts2, n1, d)
    yi = jnp.concatenate(yi, axis=1).astype(BF16).reshape(ts2, n1, d)
    yr_ref[...] = jnp.swapaxes(yr, 0, 1)
    yi_ref[...] = jnp.swapaxes(yi, 0, 1)


def _fourier_c_kernel(yr_ref, yi_ref, f2_ref, w_ref, o_ref, *, n2, tk1, scale):
    d = D_MODEL
    f2 = f2_ref[...]
    mixed = [jnp.dot(f2, jnp.concatenate([yr_ref[j], yi_ref[j]], axis=0), preferred_element_type=F32)
             for j in range(tk1)]
    mixed = (jnp.concatenate(mixed, axis=0) * scale).astype(BF16)
    delta = jnp.dot(mixed, w_ref[...], preferred_element_type=F32).astype(o_ref.dtype)
    o_ref[...] = jnp.swapaxes(delta.reshape(tk1, n2, d), 0, 1)


def _fourier_tables(seq_len):
    n1 = FOURIER_N1
    n2 = seq_len // n1
    s2_, k1_, s1_ = np.arange(n2)[:, None, None], np.arange(n1)[None, :, None], np.arange(n1)[None, None, :]
    ang = 2.0 * np.pi * ((k1_ * s1_ * n2 + k1_ * s2_) % seq_len) / seq_len
    f1 = np.concatenate([np.cos(ang), -np.sin(ang)], axis=1)
    c2, s2 = _dft_cos_sin(n2)
    f2 = np.concatenate([c2, s2], axis=1)
    cc, sc = _dft_cos_sin(FGROUP_DIM)
    cs = np.block([[cc, -sc], [sc, cc]])
    return n1, n2, f1, f2, cs


def _fourier_layer(x, g, w_bf16, layer):
    b, s, d = x.shape
    n1, n2, f1, f2, cs = _fourier_tables(s)
    ts2 = SUBLANE_BYTES // jnp.dtype(BF16).itemsize
    tk1 = SUBLANE_BYTES // jnp.dtype(FOURIER_DELTA_DTYPE).itemsize
    assert d == D_MODEL and s == n1 * n2 and n2 % ts2 == 0 and n1 % tk1 == 0
    f1 = jnp.asarray(f1, BF16)
    f2 = jnp.asarray(f2, BF16)
    cs = jnp.asarray(cs, BF16)
    g2 = g.reshape(1, d)

    y_shape = jax.ShapeDtypeStruct((b, n1, n2, d), BF16)
    yr, yi = pl.pallas_call(
        functools.partial(_fourier_a_kernel, n1=n1, ts2=ts2),
        grid=(b, n2 // ts2),
        in_specs=[
            pl.BlockSpec((None, n1, ts2, d), lambda bi, i: (bi, 0, i, 0)),
            pl.BlockSpec((1, d), lambda bi, i: (0, 0)),
            pl.BlockSpec((ts2, 2 * n1, n1), lambda bi, i: (i, 0, 0)),
            pl.BlockSpec((2 * FGROUP_DIM, 2 * FGROUP_DIM), lambda bi, i: (0, 0)),
        ],
        out_specs=[
            pl.BlockSpec((None, n1, ts2, d), lambda bi, i: (bi, 0, i, 0)),
            pl.BlockSpec((None, n1, ts2, d), lambda bi, i: (bi, 0, i, 0)),
        ],
        out_shape=[y_shape, y_shape],
        compiler_params=pltpu.CompilerParams(
            dimension_semantics=("parallel", "parallel"), vmem_limit_bytes=VMEM_LIMIT),
        name="fourier_a",
    )(x.reshape(b, n1, n2, d), g2, f1, cs)

    scale = float(1.0 / np.sqrt(s * FGROUP_DIM))
    delta = pl.pallas_call(
        functools.partial(_fourier_c_kernel, n2=n2, tk1=tk1, scale=scale),
        grid=(b, n1 // tk1),
        in_specs=[
            pl.BlockSpec((None, tk1, n2, d), lambda bi, i: (bi, i, 0, 0)),
            pl.BlockSpec((None, tk1, n2, d), lambda bi, i: (bi, i, 0, 0)),
            pl.BlockSpec((n2, 2 * n2), lambda bi, i: (0, 0)),
            _layer_spec(w_bf16, layer),
        ],
        out_specs=pl.BlockSpec((None, n2, tk1, d), lambda bi, i: (bi, 0, i, 0)),
        out_shape=jax.ShapeDtypeStruct((b, n2, n1, d), FOURIER_DELTA_DTYPE),
        compiler_params=pltpu.CompilerParams(
            dimension_semantics=("parallel", "parallel"), vmem_limit_bytes=VMEM_LIMIT),
        name="fourier_c",
    )(yr, yi, f2, w_bf16)
    return delta.reshape(b * s, d)


def _qkv_kernel(x_ref, g_ref, w_ref, qg_ref, kg_ref, cos_ref, sin_ref, q_ref, k_ref, vt_ref):
    h = _rms(x_ref[...], g_ref[...]).astype(BF16)
    y = jnp.dot(h, w_ref[...], preferred_element_type=F32)
    cos = cos_ref[...]
    sin = sin_ref[...]
    ones = jnp.ones((2 * HEAD_DIM, HEAD_DIM), BF16)
    even_lane = (lax.broadcasted_iota(jnp.int32, (1, HEAD_DIM), 1) % 2) == 0

    def norm_rope(y_head, a_own, a_partner):
        sq = y_head * y_head
        hi = sq.astype(BF16)
        lo = (sq - hi.astype(F32)).astype(BF16)
        ss = jnp.dot(jnp.concatenate([hi, lo], axis=1), ones, preferred_element_type=F32)
        r = lax.rsqrt(ss * (1.0 / HEAD_DIM) + EPS)
        partner = jnp.where(even_lane, pltpu.roll(y_head, HEAD_DIM - 1, axis=1),
                            pltpu.roll(y_head, 1, axis=1))
        return (y_head * a_own + partner * a_partner) * r

    qg = qg_ref[...] * ATTN_Q_SCALE
    kg = kg_ref[...]
    aq_own, aq_partner = cos * qg[0:1], sin * qg[1:2]
    ak_own, ak_partner = cos * kg[0:1], sin * kg[1:2]
    for hd in range(N_HEADS):
        sl = slice(hd * HEAD_DIM, (hd + 1) * HEAD_DIM)
        q_ref[:, sl] = norm_rope(y[:, sl], aq_own, aq_partner).astype(BF16)
    for hd in range(N_KV_HEADS):
        sl = slice(hd * HEAD_DIM, (hd + 1) * HEAD_DIM)
        so = slice(Q_DIM + hd * HEAD_DIM, Q_DIM + (hd + 1) * HEAD_DIM)
        k_ref[:, sl] = norm_rope(y[:, so], ak_own, ak_partner).astype(BF16)
    vt_ref[...] = y[:, Q_DIM + KV_DIM:].T.astype(BF16)


def _attn_kernel(q_ref, k_ref, vt_ref, o_ref, s0_ref, s1_ref, acc_ref, *, tq, tk):
    n_chunks = vt_ref.shape[0]
    n_items = (q_ref.shape[0] // tq) * n_chunks
    m_cols = GROUP * tq

    def scores(t, dst_ref):
        q_start = pl.multiple_of((t // n_chunks) * tq, tq)
        k_start = pl.multiple_of((t % n_chunks) * tk, tk)
        qb = q_ref[pl.ds(q_start, tq), :]
        q = jnp.concatenate([qb[:, gi * HEAD_DIM:(gi + 1) * HEAD_DIM] for gi in range(GROUP)], axis=0)
        kc = k_ref[pl.ds(k_start, tk), :]
        st = lax.dot_general(kc, q, (((1,), (1,)), ((), ())), preferred_element_type=F32)
        dst_ref[...] = st
        return jnp.max(st, axis=0, keepdims=True)

    def consume(t, src_ref, m, alpha, l):
        p = jnp.exp2(src_ref[...] - m)
        l = alpha * l + jnp.sum(p, axis=0, keepdims=True)
        pv = jnp.dot(vt_ref[t % n_chunks], p.astype(BF16), preferred_element_type=F32)
        acc_ref[...] = alpha * acc_ref[...] + pv
        return l

    def finalize(t, l):
        o_t = acc_ref[...] / l
        q_start = pl.multiple_of((t // n_chunks) * tq, tq)
        for gi in range(GROUP):
            o_ref[pl.ds(q_start, tq), gi * HEAD_DIM:(gi + 1) * HEAD_DIM] = (
                o_t[:, gi * tq:(gi + 1) * tq].T.astype(BF16))

    def step(t, cur_ref, nxt_ref, carry, maybe_last):
        m, alpha, l = carry
        cmax = scores(t + 1, nxt_ref)
        l = consume(t, cur_ref, m, alpha, l)
        if not maybe_last:
            m_next = jnp.maximum(m, cmax)
            return m_next, jnp.exp2(m - m_next), l
        last = (t % n_chunks) == n_chunks - 1
        pl.when(last)(lambda: finalize(t, l))
        m_next = jnp.where(last, cmax, jnp.maximum(m, cmax))
        alpha_next = jnp.where(last, 0.0, jnp.exp2(m - m_next))
        return m_next, alpha_next, l

    def body(tt, carry):
        carry = step(2 * tt, s0_ref, s1_ref, carry, False)
        return step(2 * tt + 1, s1_ref, s0_ref, carry, True)

    acc_ref[...] = jnp.zeros_like(acc_ref)
    m0 = scores(0, s0_ref)
    zero = jnp.zeros((1, m_cols), F32)
    carry = lax.fori_loop(0, n_items // 2 - 1, body, (m0, zero, zero))
    m, alpha, l = step(n_items - 2, s0_ref, s1_ref, carry, False)
    l = consume(n_items - 1, s1_ref, m, alpha, l)
    finalize(n_items - 1, l)


def _attn_bounded_kernel(q_ref, k_ref, vt_ref, o_ref, p0_ref, p1_ref, acc_ref, *, tq, tk):
    n_chunks = vt_ref.shape[0]
    n_tiles = q_ref.shape[0] // tq
    tiles_per_block = max(1, ATTN_ITEMS_PER_BLOCK // n_chunks)
    assert n_tiles % tiles_per_block == 0
    n_blocks = n_tiles // tiles_per_block
    bufs = (p0_ref, p1_ref)

    def probs(qi, j, dst_ref):
        q_start = pl.multiple_of(qi * tq, tq)
        qb = q_ref[pl.ds(q_start, tq), :]
        q = jnp.concatenate([qb[:, gi * HEAD_DIM:(gi + 1) * HEAD_DIM] for gi in range(GROUP)], axis=0)
        kc = k_ref[j * tk:(j + 1) * tk, :]
        st = lax.dot_general(kc, q, (((1,), (1,)), ((), ())), preferred_element_type=F32)
        p = jnp.exp2(st)
        dst_ref[...] = p.astype(BF16)
        return jnp.sum(p, axis=0, keepdims=True)

    def tile(qi, lsum, has_next):
        l = None
        for j in range(n_chunks):
            cur_ref, nxt_ref = bufs[j % 2], bufs[(j + 1) % 2]
            if j + 1 < n_chunks:
                lsum_next = probs(qi, j + 1, nxt_ref)
            elif has_next:
                lsum_next = probs(qi + 1, 0, nxt_ref)
            else:
                lsum_next = None
            pv = jnp.dot(vt_ref[j], cur_ref[...], preferred_element_type=F32)
            if j == 0:
                acc_ref[...] = pv
            else:
                acc_ref[...] += pv
            l = lsum if l is None else l + lsum
            lsum = lsum_next
        o_t = acc_ref[...] / l
        q_start = pl.multiple_of(qi * tq, tq)
        for gi in range(GROUP):
            o_ref[pl.ds(q_start, tq), gi * HEAD_DIM:(gi + 1) * HEAD_DIM] = (
                o_t[:, gi * tq:(gi + 1) * tq].T.astype(BF16))
        return lsum

    def block(bi, lsum, has_next):
        for ti in range(tiles_per_block):
            lsum = tile(bi * tiles_per_block + ti, lsum, has_next or ti + 1 < tiles_per_block)
        return lsum

    lsum = probs(0, 0, p0_ref)
    lsum = lax.fori_loop(0, n_blocks - 1, lambda bi, ls: block(bi, ls, True), lsum)
    block(n_blocks - 1, lsum, False)


def _rope_tables(seq_len):
    f32 = np.float32
    rows = seq_len // GRID_W
    row = np.repeat(np.arange(rows, dtype=f32), GRID_W)
    col = np.tile(np.arange(GRID_W, dtype=f32), rows)
    inv = (f32(ROPE_THETA) ** (-np.arange(0, AXIS_ROT_DIM, 2, dtype=f32) / f32(AXIS_ROT_DIM))).astype(f32)
    ang = np.concatenate([row[:, None] * inv, col[:, None] * inv], axis=-1)
    cos = np.repeat(np.cos(ang), 2, axis=-1)
    sin = np.repeat(np.sin(ang), 2, axis=-1) * np.tile(np.array([-1.0, 1.0], f32), HEAD_DIM // 2)
    return jnp.asarray(cos, F32), jnp.asarray(sin, F32)


def _attention_qkv(x, g, w_qkv_bf16, layer, q_gain, k_gain):
    b, s, d = x.shape
    t = b * s
    tq = ATTN_Q_TILE
    tk = ATTN_K_TILE
    tm = QKV_TOKEN_TILE
    per_chunk = tk // tm
    assert s % (2 * tk) == 0 and s % tq == 0, "attention loops take key chunks in pairs"
    cos, sin = _rope_tables(s)
    nblk = s // tm
    q, k, vt = pl.pallas_call(
        _qkv_kernel,
        grid=(t // tm,),
        in_specs=[
            pl.BlockSpec((tm, d), lambda i: (i, 0)),
            pl.BlockSpec((1, d), lambda i: (0, 0)),
            _layer_spec(w_qkv_bf16, layer),
            pl.BlockSpec((2, HEAD_DIM), lambda i: (0, 0)),
            pl.BlockSpec((2, HEAD_DIM), lambda i: (0, 0)),
            pl.BlockSpec((tm, HEAD_DIM), lambda i: (i % nblk, 0)),
            pl.BlockSpec((tm, HEAD_DIM), lambda i: (i % nblk, 0)),
        ],
        out_specs=[
            pl.BlockSpec((tm, Q_DIM), lambda i: (i, 0)),
            pl.BlockSpec((tm, KV_DIM), lambda i: (i, 0)),
            pl.BlockSpec((None, KV_DIM, tm), lambda i: (i // per_chunk, 0, i % per_chunk)),
        ],
        out_shape=[
            jax.ShapeDtypeStruct((t, Q_DIM), BF16),
            jax.ShapeDtypeStruct((t, KV_DIM), BF16),
            jax.ShapeDtypeStruct((t // tk, KV_DIM, tk), BF16),
        ],
        compiler_params=pltpu.CompilerParams(
            dimension_semantics=("parallel",), vmem_limit_bytes=VMEM_LIMIT),
        name="qkv_proj",
    )(x.reshape(t, d), g.reshape(1, d), w_qkv_bf16, q_gain, k_gain, cos, sin)

    q_bytes = s * GROUP * HEAD_DIM * jnp.dtype(BF16).itemsize
    q_mode = dict(pipeline_mode=pl.Buffered(1)) if 2 * q_bytes > ATTN_Q_PREFETCH_MAX_BYTES else {}

    def attention_call(body, buf_dtype, name):
        return pl.pallas_call(
            functools.partial(body, tq=tq, tk=tk),
            grid=(b, N_KV_HEADS),
            in_specs=[
                pl.BlockSpec((None, s, GROUP * HEAD_DIM), lambda bi, hi: (bi, 0, hi), **q_mode),
                pl.BlockSpec((None, s, HEAD_DIM), lambda bi, hi: (bi, 0, hi)),
                pl.BlockSpec((None, s // tk, HEAD_DIM, tk), lambda bi, hi: (bi, 0, hi, 0)),
            ],
            out_specs=pl.BlockSpec((None, s, GROUP * HEAD_DIM), lambda bi, hi: (bi, 0, hi)),
            out_shape=jax.ShapeDtypeStruct((b, s, Q_DIM), BF16),
            scratch_shapes=[
                pltpu.VMEM((tk, GROUP * tq), buf_dtype),
                pltpu.VMEM((tk, GROUP * tq), buf_dtype),
                pltpu.VMEM((HEAD_DIM, GROUP * tq), F32),
            ],
            compiler_params=pltpu.CompilerParams(
                dimension_semantics=("parallel", "parallel"), vmem_limit_bytes=VMEM_LIMIT),
            name=name,
        )

    score_bound = (HEAD_DIM * ATTN_Q_SCALE) * jnp.max(jnp.abs(q_gain)) * jnp.max(jnp.abs(k_gain))
    o = lax.cond(
        score_bound <= ATTN_BOUNDED_MAX_LOG2_SCORE,
        attention_call(_attn_bounded_kernel, BF16, "attention_bounded"),
        attention_call(_attn_kernel, F32, "attention"),
        q.reshape(b, s, Q_DIM), k.reshape(b, s, KV_DIM), vt.reshape(b, s // tk, KV_DIM, tk))
    return o.reshape(t, Q_DIM)


def _ffn_kernel(*refs, with_attn):
    if with_attn:
        x_ref, a_ref, wo_ref, g_ref, wg_ref, wu_ref, wd_ref, o_ref = refs
        x = x_ref[...] + jnp.dot(a_ref[...], wo_ref[...], preferred_element_type=F32)
    else:
        x_ref, delta_ref, g_ref, wg_ref, wu_ref, wd_ref, o_ref = refs
        x = x_ref[...] + delta_ref[...].astype(F32)
    r = lax.rsqrt(jnp.mean(x * x, axis=-1, keepdims=True) + EPS)
    xg = (x * g_ref[...]).astype(BF16)
    d_ff = wg_ref.shape[1]
    out = x
    for c0 in range(0, d_ff, FFN_HIDDEN_CHUNK):
        c1 = min(c0 + FFN_HIDDEN_CHUNK, d_ff)
        gate = jnp.dot(xg, wg_ref[:, c0:c1], preferred_element_type=F32) * r
        up = jnp.dot(xg, wu_ref[:, c0:c1], preferred_element_type=F32) * r
        act = (gate * jax.nn.sigmoid(gate) * up).astype(BF16)
        out = out + jnp.dot(act, wd_ref[c0:c1, :], preferred_element_type=F32)
    o_ref[...] = out


def _ffn_layer(x2d, g, layer, wg, wu, wd, attn=None, delta=None):
    assert (attn is None) != (delta is None)
    t, d = x2d.shape
    tm = FFN_TOKEN_TILE
    resident = functools.partial(pl.BlockSpec, index_map=lambda i: (0, 0), pipeline_mode=pl.Buffered(1))
    row = lambda width: pl.BlockSpec((tm, width), lambda i: (i, 0))
    in_specs = [row(d)]
    args = [x2d]
    if attn is not None:
        a2d, wo, attn_layer = attn
        in_specs += [row(Q_DIM), _layer_spec(wo, attn_layer, single_buffer=True)]
        args += [a2d, wo]
    else:
        in_specs += [row(d)]
        args += [delta]
    in_specs += [resident((1, d))] + [_layer_spec(w, layer, single_buffer=True) for w in (wg, wu, wd)]
    args += [g.reshape(1, d), wg, wu, wd]
    return pl.pallas_call(
        functools.partial(_ffn_kernel, with_attn=attn is not None),
        grid=(t // tm,),
        in_specs=in_specs,
        out_specs=row(d),
        out_shape=jax.ShapeDtypeStruct((t, d), F32),
        compiler_params=pltpu.CompilerParams(
            dimension_semantics=("parallel",), vmem_limit_bytes=VMEM_LIMIT),
        name="swiglu",
    )(*args)


def _with_pair_partner(gain):
    n_layers = gain.shape[0]
    partner = gain.reshape(n_layers, HEAD_DIM // 2, 2)[:, :, ::-1].reshape(n_layers, HEAD_DIM)
    return jnp.stack([gain, partner], axis=1)


def kernel(x_prompt, x_sample, norm_mix, norm_ffn, fourier_w, attn_w_qkv, attn_q_norm, attn_k_norm,
           attn_w_o, ffn_w_gate, ffn_w_up, ffn_w_down):
    depth = norm_mix.shape[0]
    fw = fourier_w.astype(BF16)
    wqkv = attn_w_qkv.astype(BF16)
    qn = _with_pair_partner(attn_q_norm)
    kn = _with_pair_partner(attn_k_norm)
    wo = attn_w_o.astype(BF16)
    wg = ffn_w_gate.astype(BF16)
    wu = ffn_w_up.astype(BF16)
    wd = ffn_w_down.astype(BF16)

    def run_trunk(x):
        b, s, d = x.shape
        for i in range(depth):
            j = i // 2
            if i % 2 == 0:
                delta = _fourier_layer(x, norm_mix[i], fw, j)
                x2d = _ffn_layer(x.reshape(b * s, d), norm_ffn[i], i, wg, wu, wd, delta=delta)
            else:
                a2d = _attention_qkv(x, norm_mix[i], wqkv, j, qn[j], kn[j])
                x2d = _ffn_layer(x.reshape(b * s, d), norm_ffn[i], i, wg, wu, wd, attn=(a2d, wo, j))
            x = x2d.reshape(b, s, d)
        return x

    return (run_trunk(x_prompt), run_trunk(x_sample))
```

```python
import functools

import numpy as np
import jax
import jax.numpy as jnp
from jax import lax
from jax.experimental import pallas as pl
from jax.experimental.pallas import tpu as pltpu

D_MODEL = 1024
HEAD_DIM = 128
N_HEADS = D_MODEL // HEAD_DIM
N_KV_HEADS = 2
GROUP = N_HEADS // N_KV_HEADS
Q_DIM = N_HEADS * HEAD_DIM
KV_DIM = N_KV_HEADS * HEAD_DIM
QKV_DIM = Q_DIM + 2 * KV_DIM
GRID_W = 64
AXIS_ROT_DIM = HEAD_DIM // 2
ROPE_THETA = 10000.0
N_FGROUPS = 8
FGROUP_DIM = D_MODEL // N_FGROUPS
EPS = 1e-6

ATTN_Q_SCALE = float(HEAD_DIM ** -0.5 * np.log2(np.e))
ATTN_BOUNDED_MAX_LOG2_SCORE = 60.0

F32 = jnp.float32
BF16 = jnp.bfloat16
FOURIER_DELTA_DTYPE = BF16

VMEM_LIMIT = 56 * 1024 * 1024
SUBLANE_BYTES = 32
FOURIER_N1 = 64
FFN_TOKEN_TILE = 1024
FFN_HIDDEN_CHUNK = 768
QKV_TOKEN_TILE = 1024
ATTN_Q_TILE = 256
ATTN_K_TILE = 1024
ATTN_ITEMS_PER_BLOCK = 8


def _rms(x, g):
    ms = jnp.mean(x * x, axis=-1, keepdims=True)
    return x * lax.rsqrt(ms + EPS) * g


def _layer_spec(stacked, layer, single_buffer=False):
    zeros = (0,) * (stacked.ndim - 1)
    mode = dict(pipeline_mode=pl.Buffered(1)) if single_buffer else {}
    return pl.BlockSpec((None,) + stacked.shape[1:], lambda *_: (layer,) + zeros, **mode)


def _dft_cos_sin(n):
    k = np.arange(n)
    ang = 2.0 * np.pi * ((k[:, None] * k[None, :]) % n) / n
    return np.cos(ang), np.sin(ang)


def _fourier_a_kernel(x_ref, g_ref, f1_ref, cs_ref, yr_ref, yi_ref, *, n1, ts2):
    d = D_MODEL
    h = _rms(x_ref[...], g_ref[...])
    ht = jnp.swapaxes(h.astype(BF16), 0, 1)
    z = [jnp.dot(f1_ref[j], ht[j], preferred_element_type=F32) for j in range(ts2)]
    zr = jnp.concatenate([zj[:n1] for zj in z], axis=0).astype(BF16)
    zi = jnp.concatenate([zj[n1:] for zj in z], axis=0).astype(BF16)
    cs = cs_ref[...]
    yr, yi = [], []
    for gi in range(N_FGROUPS):
        sl = slice(gi * FGROUP_DIM, (gi + 1) * FGROUP_DIM)
        y = jnp.dot(jnp.concatenate([zr[:, sl], zi[:, sl]], axis=1), cs, preferred_element_type=F32)
        yr.append(y[:, :FGROUP_DIM])
        yi.append(y[:, FGROUP_DIM:])
    yr = jnp.concatenate(yr, axis=1).astype(BF16).reshape(ts2, n1, d)
    yi = jnp.concatenate(yi, axis=1).astype(BF16).reshape(ts2, n1, d)
    yr_ref[...] = jnp.swapaxes(yr, 0, 1)
    yi_ref[...] = jnp.swapaxes(yi, 0, 1)


def _fourier_c_kernel(yr_ref, yi_ref, f2_ref, w_ref, o_ref, *, n2, tk1, scale):
    d = D_MODEL
    f2 = f2_ref[...]
    mixed = [jnp.dot(f2, jnp.concatenate([yr_ref[j], yi_ref[j]], axis=0), preferred_element_type=F32)
             for j in range(tk1)]
    mixed = (jnp.concatenate(mixed, axis=0) * scale).astype(BF16)
    delta = jnp.dot(mixed, w_ref[...], preferred_element_type=F32).astype(o_ref.dtype)
    o_ref[...] = jnp.swapaxes(delta.reshape(tk1, n2, d), 0, 1)


def _fourier_tables(seq_len):
    n1 = FOURIER_N1
    n2 = seq_len // n1
    s2_, k1_, s1_ = np.arange(n2)[:, None, None], np.arange(n1)[None, :, None], np.arange(n1)[None, None, :]
    ang = 2.0 * np.pi * ((k1_ * s1_ * n2 + k1_ * s2_) % seq_len) / seq_len
    f1 = np.concatenate([np.cos(ang), -np.sin(ang)], axis=1)
    c2, s2 = _dft_cos_sin(n2)
    f2 = np.concatenate([c2, s2], axis=1)
    cc, sc = _dft_cos_sin(FGROUP_DIM)
    cs = np.block([[cc, -sc], [sc, cc]])
    return n1, n2, f1, f2, cs


def _fourier_layer(x, g, w_bf16, layer):
    b, s, d = x.shape
    n1, n2, f1, f2, cs = _fourier_tables(s)
    ts2 = SUBLANE_BYTES // jnp.dtype(BF16).itemsize
    tk1 = SUBLANE_BYTES // jnp.dtype(FOURIER_DELTA_DTYPE).itemsize
    assert d == D_MODEL and s == n1 * n2 and n2 % ts2 == 0 and n1 % tk1 == 0
    f1 = jnp.asarray(f1, BF16)
    f2 = jnp.asarray(f2, BF16)
    cs = jnp.asarray(cs, BF16)
    g2 = g.reshape(1, d)

    y_shape = jax.ShapeDtypeStruct((b, n1, n2, d), BF16)
    yr, yi = pl.pallas_call(
        functools.partial(_fourier_a_kernel, n1=n1, ts2=ts2),
        grid=(b, n2 // ts2),
        in_specs=[
            pl.BlockSpec((None, n1, ts2, d), lambda bi, i: (bi, 0, i, 0)),
            pl.BlockSpec((1, d), lambda bi, i: (0, 0)),
            pl.BlockSpec((ts2, 2 * n1, n1), lambda bi, i: (i, 0, 0)),
            pl.BlockSpec((2 * FGROUP_DIM, 2 * FGROUP_DIM), lambda bi, i: (0, 0)),
        ],
        out_specs=[
            pl.BlockSpec((None, n1, ts2, d), lambda bi, i: (bi, 0, i, 0)),
            pl.BlockSpec((None, n1, ts2, d), lambda bi, i: (bi, 0, i, 0)),
        ],
        out_shape=[y_shape, y_shape],
        compiler_params=pltpu.CompilerParams(
            dimension_semantics=("parallel", "parallel"), vmem_limit_bytes=VMEM_LIMIT),
        name="fourier_a",
    )(x.reshape(b, n1, n2, d), g2, f1, cs)

    scale = float(1.0 / np.sqrt(s * FGROUP_DIM))
    delta = pl.pallas_call(
        functools.partial(_fourier_c_kernel, n2=n2, tk1=tk1, scale=scale),
        grid=(b, n1 // tk1),
        in_specs=[
            pl.BlockSpec((None, tk1, n2, d), lambda bi, i: (bi, i, 0, 0)),
            pl.BlockSpec((None, tk1, n2, d), lambda bi, i: (bi, i, 0, 0)),
            pl.BlockSpec((n2, 2 * n2), lambda bi, i: (0, 0)),
            _layer_spec(w_bf16, layer),
        ],
        out_specs=pl.BlockSpec((None, n2, tk1, d), lambda bi, i: (bi, 0, i, 0)),
        out_shape=jax.ShapeDtypeStruct((b, n2, n1, d), FOURIER_DELTA_DTYPE),
        compiler_params=pltpu.CompilerParams(
            dimension_semantics=("parallel", "parallel"), vmem_limit_bytes=VMEM_LIMIT),
        name="fourier_c",
    )(yr, yi, f2, w_bf16)
    return delta.reshape(b * s, d)


def _qkv_kernel(x_ref, g_ref, w_ref, qg_ref, kg_ref, cos_ref, sin_ref, q_ref, k_ref, vt_ref):
    h = _rms(x_ref[...], g_ref[...]).astype(BF16)
    y = jnp.dot(h, w_ref[...], preferred_element_type=F32)
    cos = cos_ref[...]
    sin = sin_ref[...]
    ones = jnp.ones((2 * HEAD_DIM, HEAD_DIM), BF16)
    even_lane = (lax.broadcasted_iota(jnp.int32, (1, HEAD_DIM), 1) % 2) == 0

    def norm_rope(y_head, a_own, a_partner):
        sq = y_head * y_head
        hi = sq.astype(BF16)
        lo = (sq - hi.astype(F32)).astype(BF16)
        ss = jnp.dot(jnp.concatenate([hi, lo], axis=1), ones, preferred_element_type=F32)
        r = lax.rsqrt(ss * (1.0 / HEAD_DIM) + EPS)
        partner = jnp.where(even_lane, pltpu.roll(y_head, HEAD_DIM - 1, axis=1),
                            pltpu.roll(y_head, 1, axis=1))
        return (y_head * a_own + partner * a_partner) * r

    qg = qg_ref[...] * ATTN_Q_SCALE
    kg = kg_ref[...]
    aq_own, aq_partner = cos * qg[0:1], sin * qg[1:2]
    ak_own, ak_partner = cos * kg[0:1], sin * kg[1:2]
    for hd in range(N_HEADS):
        sl = slice(hd * HEAD_DIM, (hd + 1) * HEAD_DIM)
        lo = (hd % GROUP) * HEAD_DIM
        q_ref[hd // GROUP, :, lo:lo + HEAD_DIM] = norm_rope(y[:, sl], aq_own, aq_partner).astype(BF16)
    for hd in range(N_KV_HEADS):
        so = slice(Q_DIM + hd * HEAD_DIM, Q_DIM + (hd + 1) * HEAD_DIM)
        k_ref[hd] = norm_rope(y[:, so], ak_own, ak_partner).astype(BF16)
    vt_ref[...] = y[:, Q_DIM + KV_DIM:].T.astype(BF16)


def _attn_kernel(q_ref, k_ref, vt_ref, o_ref, s0_ref, s1_ref, acc_ref, *, tq, tk):
    n_chunks = vt_ref.shape[0]
    n_items = (q_ref.shape[0] // tq) * n_chunks
    m_cols = GROUP * tq

    def scores(t, dst_ref):
        q_start = pl.multiple_of((t // n_chunks) * tq, tq)
        k_start = pl.multiple_of((t % n_chunks) * tk, tk)
        qb = q_ref[pl.ds(q_start, tq), :]
        q = jnp.concatenate([qb[:, gi * HEAD_DIM:(gi + 1) * HEAD_DIM] for gi in range(GROUP)], axis=0)
        kc = k_ref[pl.ds(k_start, tk), :]
        st = lax.dot_general(kc, q, (((1,), (1,)), ((), ())), preferred_element_type=F32)
        dst_ref[...] = st
        return jnp.max(st, axis=0, keepdims=True)

    def consume(t, src_ref, m, alpha, l):
        p = jnp.exp2(src_ref[...] - m)
        l = alpha * l + jnp.sum(p, axis=0, keepdims=True)
        pv = jnp.dot(vt_ref[t % n_chunks], p.astype(BF16), preferred_element_type=F32)
        acc_ref[...] = alpha * acc_ref[...] + pv
        return l

    def finalize(t, l):
        o_t = acc_ref[...] / l
        q_start = pl.multiple_of((t // n_chunks) * tq, tq)
        for gi in range(GROUP):
            o_ref[pl.ds(q_start, tq), gi * HEAD_DIM:(gi + 1) * HEAD_DIM] = (
                o_t[:, gi * tq:(gi + 1) * tq].T.astype(BF16))

    def step(t, cur_ref, nxt_ref, carry, maybe_last):
        m, alpha, l = carry
        cmax = scores(t + 1, nxt_ref)
        l = consume(t, cur_ref, m, alpha, l)
        if not maybe_last:
            m_next = jnp.maximum(m, cmax)
            return m_next, jnp.exp2(m - m_next), l
        last = (t % n_chunks) == n_chunks - 1
        pl.when(last)(lambda: finalize(t, l))
        m_next = jnp.where(last, cmax, jnp.maximum(m, cmax))
        alpha_next = jnp.where(last, 0.0, jnp.exp2(m - m_next))
        return m_next, alpha_next, l

    def body(tt, carry):
        carry = step(2 * tt, s0_ref, s1_ref, carry, False)
        return step(2 * tt + 1, s1_ref, s0_ref, carry, True)

    acc_ref[...] = jnp.zeros_like(acc_ref)
    m0 = scores(0, s0_ref)
    zero = jnp.zeros((1, m_cols), F32)
    carry = lax.fori_loop(0, n_items // 2 - 1, body, (m0, zero, zero))
    m, alpha, l = step(n_items - 2, s0_ref, s1_ref, carry, False)
    l = consume(n_items - 1, s1_ref, m, alpha, l)
    finalize(n_items - 1, l)


def _attn_bounded_kernel(q_ref, k_ref, vt_ref, o_ref, p0_ref, p1_ref, acc_ref, *, tq, tk):
    n_chunks = vt_ref.shape[0]
    n_tiles = q_ref.shape[0] // tq
    tiles_per_block = max(1, ATTN_ITEMS_PER_BLOCK // n_chunks)
    assert n_tiles % tiles_per_block == 0
    n_blocks = n_tiles // tiles_per_block
    bufs = (p0_ref, p1_ref)

    def probs(qi, j, dst_ref):
        q_start = pl.multiple_of(qi * tq, tq)
        qb = q_ref[pl.ds(q_start, tq), :]
        q = jnp.concatenate([qb[:, gi * HEAD_DIM:(gi + 1) * HEAD_DIM] for gi in range(GROUP)], axis=0)
        kc = k_ref[j * tk:(j + 1) * tk, :]
        st = lax.dot_general(kc, q, (((1,), (1,)), ((), ())), preferred_element_type=F32)
        p = jnp.exp2(st)
        dst_ref[...] = p.astype(BF16)
        return jnp.sum(p, axis=0, keepdims=True)

    def tile(qi, lsum, has_next):
        l = None
        for j in range(n_chunks):
            cur_ref, nxt_ref = bufs[j % 2], bufs[(j + 1) % 2]
            if j + 1 < n_chunks:
                lsum_next = probs(qi, j + 1, nxt_ref)
            elif has_next:
                lsum_next = probs(qi + 1, 0, nxt_ref)
            else:
                lsum_next = None
            pv = jnp.dot(vt_ref[j], cur_ref[...], preferred_element_type=F32)
            if j == 0:
                acc_ref[...] = pv
            else:
                acc_ref[...] += pv
            l = lsum if l is None else l + lsum
            lsum = lsum_next
        o_t = acc_ref[...] / l
        q_start = pl.multiple_of(qi * tq, tq)
        for gi in range(GROUP):
            o_ref[pl.ds(q_start, tq), gi * HEAD_DIM:(gi + 1) * HEAD_DIM] = (
                o_t[:, gi * tq:(gi + 1) * tq].T.astype(BF16))
        return lsum

    def block(bi, lsum, has_next):
        for ti in range(tiles_per_block):
            lsum = tile(bi * tiles_per_block + ti, lsum, has_next or ti + 1 < tiles_per_block)
        return lsum

    lsum = probs(0, 0, p0_ref)
    lsum = lax.fori_loop(0, n_blocks - 1, lambda bi, ls: block(bi, ls, True), lsum)
    block(n_blocks - 1, lsum, False)


def _rope_tables(seq_len):
    f32 = np.float32
    rows = seq_len // GRID_W
    row = np.repeat(np.arange(rows, dtype=f32), GRID_W)
    col = np.tile(np.arange(GRID_W, dtype=f32), rows)
    inv = (f32(ROPE_THETA) ** (-np.arange(0, AXIS_ROT_DIM, 2, dtype=f32) / f32(AXIS_ROT_DIM))).astype(f32)
    ang = np.concatenate([row[:, None] * inv, col[:, None] * inv], axis=-1)
    cos = np.repeat(np.cos(ang), 2, axis=-1)
    sin = np.repeat(np.sin(ang), 2, axis=-1) * np.tile(np.array([-1.0, 1.0], f32), HEAD_DIM // 2)
    return jnp.asarray(cos, F32), jnp.asarray(sin, F32)


def _attention_qkv(x, g, w_qkv_bf16, layer, q_gain, k_gain):
    b, s, d = x.shape
    t = b * s
    tq = ATTN_Q_TILE
    tk = ATTN_K_TILE
    tm = QKV_TOKEN_TILE
    per_chunk = tk // tm
    assert s % (2 * tk) == 0 and s % tq == 0, "attention loops take key chunks in pairs"
    cos, sin = _rope_tables(s)
    nblk = s // tm
    q, k, vt = pl.pallas_call(
        _qkv_kernel,
        grid=(t // tm,),
        in_specs=[
            pl.BlockSpec((tm, d), lambda i: (i, 0)),
            pl.BlockSpec((1, d), lambda i: (0, 0)),
            _layer_spec(w_qkv_bf16, layer),
            pl.BlockSpec((2, HEAD_DIM), lambda i: (0, 0)),
            pl.BlockSpec((2, HEAD_DIM), lambda i: (0, 0)),
            pl.BlockSpec((tm, HEAD_DIM), lambda i: (i % nblk, 0)),
            pl.BlockSpec((tm, HEAD_DIM), lambda i: (i % nblk, 0)),
        ],
        out_specs=[
            pl.BlockSpec((N_KV_HEADS, tm, GROUP * HEAD_DIM), lambda i: (0, i, 0)),
            pl.BlockSpec((N_KV_HEADS, tm, HEAD_DIM), lambda i: (0, i, 0)),
            pl.BlockSpec((None, KV_DIM, tm), lambda i: (i // per_chunk, 0, i % per_chunk)),
        ],
        out_shape=[
            jax.ShapeDtypeStruct((N_KV_HEADS, t, GROUP * HEAD_DIM), BF16),
            jax.ShapeDtypeStruct((N_KV_HEADS, t, HEAD_DIM), BF16),
            jax.ShapeDtypeStruct((t // tk, KV_DIM, tk), BF16),
        ],
        compiler_params=pltpu.CompilerParams(
            dimension_semantics=("parallel",), vmem_limit_bytes=VMEM_LIMIT),
        name="qkv_proj",
    )(x.reshape(t, d), g.reshape(1, d), w_qkv_bf16, q_gain, k_gain, cos, sin)

    def attention_call(body, buf_dtype, name):
        return pl.pallas_call(
            functools.partial(body, tq=tq, tk=tk),
            grid=(b, N_KV_HEADS),
            in_specs=[
                pl.BlockSpec((None, None, s, GROUP * HEAD_DIM), lambda bi, hi: (hi, bi, 0, 0)),
                pl.BlockSpec((None, None, s, HEAD_DIM), lambda bi, hi: (hi, bi, 0, 0)),
                pl.BlockSpec((None, s // tk, HEAD_DIM, tk), lambda bi, hi: (bi, 0, hi, 0)),
            ],
            out_specs=pl.BlockSpec((None, None, s, GROUP * HEAD_DIM), lambda bi, hi: (hi, bi, 0, 0)),
            out_shape=jax.ShapeDtypeStruct((N_KV_HEADS, b, s, GROUP * HEAD_DIM), BF16),
            scratch_shapes=[
                pltpu.VMEM((tk, GROUP * tq), buf_dtype),
                pltpu.VMEM((tk, GROUP * tq), buf_dtype),
                pltpu.VMEM((HEAD_DIM, GROUP * tq), F32),
            ],
            compiler_params=pltpu.CompilerParams(
                dimension_semantics=("parallel", "parallel"), vmem_limit_bytes=VMEM_LIMIT),
            name=name,
        )

    score_bound = (HEAD_DIM * ATTN_Q_SCALE) * jnp.max(jnp.abs(q_gain)) * jnp.max(jnp.abs(k_gain))
    o = lax.cond(
        score_bound <= ATTN_BOUNDED_MAX_LOG2_SCORE,
        attention_call(_attn_bounded_kernel, BF16, "attention_bounded"),
        attention_call(_attn_kernel, F32, "attention"),
        q.reshape(N_KV_HEADS, b, s, GROUP * HEAD_DIM), k.reshape(N_KV_HEADS, b, s, HEAD_DIM),
        vt.reshape(b, s // tk, KV_DIM, tk))
    return o.reshape(N_KV_HEADS, t, GROUP * HEAD_DIM)


def _ffn_kernel(*refs, with_attn):
    if with_attn:
        x_ref, a_ref, wo_ref, g_ref, wg_ref, wu_ref, wd_ref, o_ref = refs
        heads = jnp.concatenate([a_ref[gi] for gi in range(N_KV_HEADS)], axis=1)
        x = x_ref[...] + jnp.dot(heads, wo_ref[...], preferred_element_type=F32)
    else:
        x_ref, delta_ref, g_ref, wg_ref, wu_ref, wd_ref, o_ref = refs
        x = x_ref[...] + delta_ref[...].astype(F32)
    r = lax.rsqrt(jnp.mean(x * x, axis=-1, keepdims=True) + EPS)
    xg = (x * g_ref[...]).astype(BF16)
    d_ff = wg_ref.shape[1]
    out = x
    for c0 in range(0, d_ff, FFN_HIDDEN_CHUNK):
        c1 = min(c0 + FFN_HIDDEN_CHUNK, d_ff)
        gate = jnp.dot(xg, wg_ref[:, c0:c1], preferred_element_type=F32) * r
        up = jnp.dot(xg, wu_ref[:, c0:c1], preferred_element_type=F32) * r
        act = (gate * jax.nn.sigmoid(gate) * up).astype(BF16)
        out = out + jnp.dot(act, wd_ref[c0:c1, :], preferred_element_type=F32)
    o_ref[...] = out


def _ffn_layer(x2d, g, layer, wg, wu, wd, attn=None, delta=None):
    assert (attn is None) != (delta is None)
    t, d = x2d.shape
    tm = FFN_TOKEN_TILE
    resident = functools.partial(pl.BlockSpec, index_map=lambda i: (0, 0), pipeline_mode=pl.Buffered(1))
    row = lambda width: pl.BlockSpec((tm, width), lambda i: (i, 0))
    in_specs = [row(d)]
    args = [x2d]
    if attn is not None:
        a2d, wo, attn_layer = attn
        in_specs += [pl.BlockSpec((N_KV_HEADS, tm, GROUP * HEAD_DIM), lambda i: (0, i, 0)),
                     _layer_spec(wo, attn_layer, single_buffer=True)]
        args += [a2d, wo]
    else:
        in_specs += [row(d)]
        args += [delta]
    in_specs += [resident((1, d))] + [_layer_spec(w, layer, single_buffer=True) for w in (wg, wu, wd)]
    args += [g.reshape(1, d), wg, wu, wd]
    return pl.pallas_call(
        functools.partial(_ffn_kernel, with_attn=attn is not None),
        grid=(t // tm,),
        in_specs=in_specs,
        out_specs=row(d),
        out_shape=jax.ShapeDtypeStruct((t, d), F32),
        compiler_params=pltpu.CompilerParams(
            dimension_semantics=("parallel",), vmem_limit_bytes=VMEM_LIMIT),
        name="swiglu",
    )(*args)


def _with_pair_partner(gain):
    n_layers = gain.shape[0]
    partner = gain.reshape(n_layers, HEAD_DIM // 2, 2)[:, :, ::-1].reshape(n_layers, HEAD_DIM)
    return jnp.stack([gain, partner], axis=1)


def kernel(x_prompt, x_sample, norm_mix, norm_ffn, fourier_w, attn_w_qkv, attn_q_norm, attn_k_norm,
           attn_w_o, ffn_w_gate, ffn_w_up, ffn_w_down):
    depth = norm_mix.shape[0]
    fw = fourier_w.astype(BF16)
    wqkv = attn_w_qkv.astype(BF16)
    qn = _with_pair_partner(attn_q_norm)
    kn = _with_pair_partner(attn_k_norm)
    wo = attn_w_o.astype(BF16)
    wg = ffn_w_gate.astype(BF16)
    wu = ffn_w_up.astype(BF16)
    wd = ffn_w_down.astype(BF16)

    def run_trunk(x):
        b, s, d = x.shape
        for i in range(depth):
            j = i // 2
            if i % 2 == 0:
                delta = _fourier_layer(x, norm_mix[i], fw, j)
                x2d = _ffn_layer(x.reshape(b * s, d), norm_ffn[i], i, wg, wu, wd, delta=delta)
            else:
                a2d = _attention_qkv(x, norm_mix[i], wqkv, j, qn[j], kn[j])
                x2d = _ffn_layer(x.reshape(b * s, d), norm_ffn[i], i, wg, wu, wd, attn=(a2d, wo, j))
            x = x2d.reshape(b, s, d)
        return x

    return (run_trunk(x_prompt), run_trunk(x_sample))
```

```python
import functools

import numpy as np
import jax
import jax.numpy as jnp
from jax import lax
from jax.experimental import pallas as pl
from jax.experimental.pallas import tpu as pltpu

D_MODEL = 1024
HEAD_DIM = 128
N_HEADS = D_MODEL // HEAD_DIM
N_KV_HEADS = 2
GROUP = N_HEADS // N_KV_HEADS
Q_DIM = N_HEADS * HEAD_DIM
KV_DIM = N_KV_HEADS * HEAD_DIM
QKV_DIM = Q_DIM + 2 * KV_DIM
GRID_W = 64
AXIS_ROT_DIM = HEAD_DIM // 2
ROPE_THETA = 10000.0
N_FGROUPS = 8
FGROUP_DIM = D_MODEL // N_FGROUPS
EPS = 1e-6

ATTN_Q_SCALE = float(HEAD_DIM ** -0.5 * np.log2(np.e))
ATTN_BOUNDED_MAX_LOG2_SCORE = 60.0

F32 = jnp.float32
BF16 = jnp.bfloat16
FOURIER_DELTA_DTYPE = BF16

VMEM_LIMIT = 56 * 1024 * 1024
SUBLANE_BYTES = 32
FOURIER_N1 = 64
FFN_TOKEN_TILE = 1024
FFN_HIDDEN_CHUNK = 768
QKV_TOKEN_TILE = 1024
ATTN_Q_TILE = 256
ATTN_K_TILE = 1024
ATTN_ITEMS_PER_BLOCK = 8


def _rms(x, g):
    ms = jnp.mean(x * x, axis=-1, keepdims=True)
    return x * lax.rsqrt(ms + EPS) * g


def _layer_spec(stacked, layer, single_buffer=False):
    zeros = (0,) * (stacked.ndim - 1)
    mode = dict(pipeline_mode=pl.Buffered(1)) if single_buffer else {}
    return pl.BlockSpec((None,) + stacked.shape[1:], lambda *_: (layer,) + zeros, **mode)


def _dft_cos_sin(n):
    k = np.arange(n)
    ang = 2.0 * np.pi * ((k[:, None] * k[None, :]) % n) / n
    return np.cos(ang), np.sin(ang)


def _fourier_a_kernel(x_ref, g_ref, f1_ref, cs_ref, yr_ref, yi_ref, *, n1, ts2):
    d = D_MODEL
    h = _rms(x_ref[...], g_ref[...])
    ht = jnp.swapaxes(h.astype(BF16), 0, 1)
    z = [jnp.dot(f1_ref[j], ht[j], preferred_element_type=F32) for j in range(ts2)]
    zr = jnp.concatenate([zj[:n1] for zj in z], axis=0).astype(BF16)
    zi = jnp.concatenate([zj[n1:] for zj in z], axis=0).astype(BF16)
    cs = cs_ref[...]
    yr, yi = [], []
    for gi in range(N_FGROUPS):
        sl = slice(gi * FGROUP_DIM, (gi + 1) * FGROUP_DIM)
        y = jnp.dot(jnp.concatenate([zr[:, sl], zi[:, sl]], axis=1), cs, preferred_element_type=F32)
        yr.append(y[:, :FGROUP_DIM])
        yi.append(y[:, FGROUP_DIM:])
    yr = jnp.concatenate(yr, axis=1).astype(BF16).reshape(ts2, n1, d)
    yi = jnp.concatenate(yi, axis=1).astype(BF16).reshape(ts2, n1, d)
    yr_ref[...] = jnp.swapaxes(yr, 0, 1)
    yi_ref[...] = jnp.swapaxes(yi, 0, 1)


def _fourier_c_kernel(yr_ref, yi_ref, f2_ref, w_ref, o_ref, *, n2, tk1, scale):
    d = D_MODEL
    f2 = f2_ref[...]
    mixed = [jnp.dot(f2, jnp.concatenate([yr_ref[j], yi_ref[j]], axis=0), preferred_element_type=F32)
             for j in range(tk1)]
    mixed = (jnp.concatenate(mixed, axis=0) * scale).astype(BF16)
    delta = jnp.dot(mixed, w_ref[...], preferred_element_type=F32).astype(o_ref.dtype)
    o_ref[...] = jnp.swapaxes(delta.reshape(tk1, n2, d), 0, 1)


def _fourier_tables(seq_len):
    n1 = FOURIER_N1
    n2 = seq_len // n1
    s2_, k1_, s1_ = np.arange(n2)[:, None, None], np.arange(n1)[None, :, None], np.arange(n1)[None, None, :]
    ang = 2.0 * np.pi * ((k1_ * s1_ * n2 + k1_ * s2_) % seq_len) / seq_len
    f1 = np.concatenate([np.cos(ang), -np.sin(ang)], axis=1)
    c2, s2 = _dft_cos_sin(n2)
    f2 = np.concatenate([c2, s2], axis=1)
    cc, sc = _dft_cos_sin(FGROUP_DIM)
    cs = np.block([[cc, -sc], [sc, cc]])
    return n1, n2, f1, f2, cs


def _fourier_layer(x, g, w_bf16, layer):
    b, s, d = x.shape
    n1, n2, f1, f2, cs = _fourier_tables(s)
    ts2 = SUBLANE_BYTES // jnp.dtype(BF16).itemsize
    tk1 = SUBLANE_BYTES // jnp.dtype(FOURIER_DELTA_DTYPE).itemsize
    assert d == D_MODEL and s == n1 * n2 and n2 % ts2 == 0 and n1 % tk1 == 0
    f1 = jnp.asarray(f1, F32).astype(BF16)
    f2 = jnp.asarray(f2, F32).astype(BF16)
    cs = jnp.asarray(cs, F32).astype(BF16)
    g2 = g.reshape(1, d)

    y_shape = jax.ShapeDtypeStruct((b, n1, n2, d), BF16)
    yr, yi = pl.pallas_call(
        functools.partial(_fourier_a_kernel, n1=n1, ts2=ts2),
        grid=(b, n2 // ts2),
        in_specs=[
            pl.BlockSpec((None, n1, ts2, d), lambda bi, i: (bi, 0, i, 0)),
            pl.BlockSpec((1, d), lambda bi, i: (0, 0)),
            pl.BlockSpec((ts2, 2 * n1, n1), lambda bi, i: (i, 0, 0)),
            pl.BlockSpec((2 * FGROUP_DIM, 2 * FGROUP_DIM), lambda bi, i: (0, 0)),
        ],
        out_specs=[
            pl.BlockSpec((None, n1, ts2, d), lambda bi, i: (bi, 0, i, 0)),
            pl.BlockSpec((None, n1, ts2, d), lambda bi, i: (bi, 0, i, 0)),
        ],
        out_shape=[y_shape, y_shape],
        compiler_params=pltpu.CompilerParams(
            dimension_semantics=("parallel", "parallel"), vmem_limit_bytes=VMEM_LIMIT),
        name="fourier_a",
    )(x.reshape(b, n1, n2, d), g2, f1, cs)

    scale = float(1.0 / np.sqrt(s * FGROUP_DIM))
    delta = pl.pallas_call(
        functools.partial(_fourier_c_kernel, n2=n2, tk1=tk1, scale=scale),
        grid=(b, n1 // tk1),
        in_specs=[
            pl.BlockSpec((None, tk1, n2, d), lambda bi, i: (bi, i, 0, 0)),
            pl.BlockSpec((None, tk1, n2, d), lambda bi, i: (bi, i, 0, 0)),
            pl.BlockSpec((n2, 2 * n2), lambda bi, i: (0, 0)),
            _layer_spec(w_bf16, layer),
        ],
        out_specs=pl.BlockSpec((None, n2, tk1, d), lambda bi, i: (bi, 0, i, 0)),
        out_shape=jax.ShapeDtypeStruct((b, n2, n1, d), FOURIER_DELTA_DTYPE),
        compiler_params=pltpu.CompilerParams(
            dimension_semantics=("parallel", "parallel"), vmem_limit_bytes=VMEM_LIMIT),
        name="fourier_c",
    )(yr, yi, f2, w_bf16)
    return delta.reshape(b * s, d)


def _qkv_kernel(x_ref, g_ref, w_ref, qg_ref, kg_ref, cos_ref, sin_ref, q_ref, k_ref, vt_ref):
    h = _rms(x_ref[...], g_ref[...]).astype(BF16)
    y = jnp.dot(h, w_ref[...], preferred_element_type=F32)
    cos = cos_ref[...]
    sin = sin_ref[...]
    ones = jnp.ones((2 * HEAD_DIM, HEAD_DIM), BF16)
    even_lane = (lax.broadcasted_iota(jnp.int32, (1, HEAD_DIM), 1) % 2) == 0

    def norm_rope(y_head, a_own, a_partner):
        sq = y_head * y_head
        hi = sq.astype(BF16)
        lo = (sq - hi.astype(F32)).astype(BF16)
        ss = jnp.dot(jnp.concatenate([hi, lo], axis=1), ones, preferred_element_type=F32)
        r = lax.rsqrt(ss * (1.0 / HEAD_DIM) + EPS)
        partner = jnp.where(even_lane, pltpu.roll(y_head, HEAD_DIM - 1, axis=1),
                            pltpu.roll(y_head, 1, axis=1))
        return (y_head * a_own + partner * a_partner) * r

    qg = qg_ref[...] * ATTN_Q_SCALE
    kg = kg_ref[...]
    aq_own, aq_partner = cos * qg[0:1], sin * qg[1:2]
    ak_own, ak_partner = cos * kg[0:1], sin * kg[1:2]
    for hd in range(N_HEADS):
        sl = slice(hd * HEAD_DIM, (hd + 1) * HEAD_DIM)
        q_ref[:, sl] = norm_rope(y[:, sl], aq_own, aq_partner).astype(BF16)
    for hd in range(N_KV_HEADS):
        sl = slice(hd * HEAD_DIM, (hd + 1) * HEAD_DIM)
        so = slice(Q_DIM + hd * HEAD_DIM, Q_DIM + (hd + 1) * HEAD_DIM)
        k_ref[:, sl] = norm_rope(y[:, so], ak_own, ak_partner).astype(BF16)
    vt_ref[...] = y[:, Q_DIM + KV_DIM:].T.astype(BF16)


def _attn_kernel(q_ref, k_ref, vt_ref, o_ref, s0_ref, s1_ref, acc_ref, *, tq, tk):
    n_chunks = vt_ref.shape[0]
    n_items = (q_ref.shape[0] // tq) * n_chunks
    m_cols = GROUP * tq

    def scores(t, dst_ref):
        q_start = pl.multiple_of((t // n_chunks) * tq, tq)
        k_start = pl.multiple_of((t % n_chunks) * tk, tk)
        qb = q_ref[pl.ds(q_start, tq), :]
        q = jnp.concatenate([qb[:, gi * HEAD_DIM:(gi + 1) * HEAD_DIM] for gi in range(GROUP)], axis=0)
        kc = k_ref[pl.ds(k_start, tk), :]
        st = lax.dot_general(kc, q, (((1,), (1,)), ((), ())), preferred_element_type=F32)
        dst_ref[...] = st
        return jnp.max(st, axis=0, keepdims=True)

    def consume(t, src_ref, m, alpha, l):
        p = jnp.exp2(src_ref[...] - m)
        l = alpha * l + jnp.sum(p, axis=0, keepdims=True)
        pv = jnp.dot(vt_ref[t % n_chunks], p.astype(BF16), preferred_element_type=F32)
        acc_ref[...] = alpha * acc_ref[...] + pv
        return l

    def finalize(t, l):
        o_t = acc_ref[...] / l
        q_start = pl.multiple_of((t // n_chunks) * tq, tq)
        for gi in range(GROUP):
            o_ref[pl.ds(q_start, tq), gi * HEAD_DIM:(gi + 1) * HEAD_DIM] = (
                o_t[:, gi * tq:(gi + 1) * tq].T.astype(BF16))

    def step(t, cur_ref, nxt_ref, carry, maybe_last):
        m, alpha, l = carry
        cmax = scores(t + 1, nxt_ref)
        l = consume(t, cur_ref, m, alpha, l)
        if not maybe_last:
            m_next = jnp.maximum(m, cmax)
            return m_next, jnp.exp2(m - m_next), l
        last = (t % n_chunks) == n_chunks - 1
        pl.when(last)(lambda: finalize(t, l))
        m_next = jnp.where(last, cmax, jnp.maximum(m, cmax))
        alpha_next = jnp.where(last, 0.0, jnp.exp2(m - m_next))
        return m_next, alpha_next, l

    def body(tt, carry):
        carry = step(2 * tt, s0_ref, s1_ref, carry, False)
        return step(2 * tt + 1, s1_ref, s0_ref, carry, True)

    acc_ref[...] = jnp.zeros_like(acc_ref)
    m0 = scores(0, s0_ref)
    zero = jnp.zeros((1, m_cols), F32)
    carry = lax.fori_loop(0, n_items // 2 - 1, body, (m0, zero, zero))
    m, alpha, l = step(n_items - 2, s0_ref, s1_ref, carry, False)
    l = consume(n_items - 1, s1_ref, m, alpha, l)
    finalize(n_items - 1, l)


def _attn_bounded_kernel(q_ref, k_ref, vt_ref, o_ref, p0_ref, p1_ref, acc_ref, *, tq, tk):
    n_chunks = vt_ref.shape[0]
    n_tiles = q_ref.shape[0] // tq
    tiles_per_block = max(1, ATTN_ITEMS_PER_BLOCK // n_chunks)
    assert n_tiles % tiles_per_block == 0
    n_blocks = n_tiles // tiles_per_block
    bufs = (p0_ref, p1_ref)

    def probs(qi, j, dst_ref):
        q_start = pl.multiple_of(qi * tq, tq)
        qb = q_ref[pl.ds(q_start, tq), :]
        q = jnp.concatenate([qb[:, gi * HEAD_DIM:(gi + 1) * HEAD_DIM] for gi in range(GROUP)], axis=0)
        kc = k_ref[j * tk:(j + 1) * tk, :]
        st = lax.dot_general(kc, q, (((1,), (1,)), ((), ())), preferred_element_type=F32)
        p = jnp.exp2(st)
        dst_ref[...] = p.astype(BF16)
        return jnp.sum(p, axis=0, keepdims=True)

    def tile(qi, lsum, has_next):
        l = None
        for j in range(n_chunks):
            cur_ref, nxt_ref = bufs[j % 2], bufs[(j + 1) % 2]
            if j + 1 < n_chunks:
                lsum_next = probs(qi, j + 1, nxt_ref)
            elif has_next:
                lsum_next = probs(qi + 1, 0, nxt_ref)
            else:
                lsum_next = None
            pv = jnp.dot(vt_ref[j], cur_ref[...], preferred_element_type=F32)
            if j == 0:
                acc_ref[...] = pv
            else:
                acc_ref[...] += pv
            l = lsum if l is None else l + lsum
            lsum = lsum_next
        o_t = acc_ref[...] / l
        q_start = pl.multiple_of(qi * tq, tq)
        for gi in range(GROUP):
            o_ref[pl.ds(q_start, tq), gi * HEAD_DIM:(gi + 1) * HEAD_DIM] = (
                o_t[:, gi * tq:(gi + 1) * tq].T.astype(BF16))
        return lsum

    def block(bi, lsum, has_next):
        for ti in range(tiles_per_block):
            lsum = tile(bi * tiles_per_block + ti, lsum, has_next or ti + 1 < tiles_per_block)
        return lsum

    lsum = probs(0, 0, p0_ref)
    lsum = lax.fori_loop(0, n_blocks - 1, lambda bi, ls: block(bi, ls, True), lsum)
    block(n_blocks - 1, lsum, False)


def _rope_tables(seq_len):
    f32 = np.float32
    rows = seq_len // GRID_W
    row = np.repeat(np.arange(rows, dtype=f32), GRID_W)
    col = np.tile(np.arange(GRID_W, dtype=f32), rows)
    inv = (f32(ROPE_THETA) ** (-np.arange(0, AXIS_ROT_DIM, 2, dtype=f32) / f32(AXIS_ROT_DIM))).astype(f32)
    ang = np.concatenate([row[:, None] * inv, col[:, None] * inv], axis=-1)
    cos = np.repeat(np.cos(ang), 2, axis=-1)
    sin = np.repeat(np.sin(ang), 2, axis=-1) * np.tile(np.array([-1.0, 1.0], f32), HEAD_DIM // 2)
    return jnp.asarray(cos, F32), jnp.asarray(sin, F32)


def _attention_qkv(x, g, w_qkv_bf16, layer, q_gain, k_gain):
    b, s, d = x.shape
    t = b * s
    tq = ATTN_Q_TILE
    tk = ATTN_K_TILE
    tm = QKV_TOKEN_TILE
    per_chunk = tk // tm
    assert s % (2 * tk) == 0 and s % tq == 0, "attention loops take key chunks in pairs"
    cos, sin = _rope_tables(s)
    nblk = s // tm
    q, k, vt = pl.pallas_call(
        _qkv_kernel,
        grid=(t // tm,),
        in_specs=[
            pl.BlockSpec((tm, d), lambda i: (i, 0)),
            pl.BlockSpec((1, d), lambda i: (0, 0)),
            _layer_spec(w_qkv_bf16, layer),
            pl.BlockSpec((2, HEAD_DIM), lambda i: (0, 0)),
            pl.BlockSpec((2, HEAD_DIM), lambda i: (0, 0)),
            pl.BlockSpec((tm, HEAD_DIM), lambda i: (i % nblk, 0)),
            pl.BlockSpec((tm, HEAD_DIM), lambda i: (i % nblk, 0)),
        ],
        out_specs=[
            pl.BlockSpec((tm, Q_DIM), lambda i: (i, 0)),
            pl.BlockSpec((tm, KV_DIM), lambda i: (i, 0)),
            pl.BlockSpec((None, KV_DIM, tm), lambda i: (i // per_chunk, 0, i % per_chunk)),
        ],
        out_shape=[
            jax.ShapeDtypeStruct((t, Q_DIM), BF16),
            jax.ShapeDtypeStruct((t, KV_DIM), BF16),
            jax.ShapeDtypeStruct((t // tk, KV_DIM, tk), BF16),
        ],
        compiler_params=pltpu.CompilerParams(
            dimension_semantics=("parallel",), vmem_limit_bytes=VMEM_LIMIT),
        name="qkv_proj",
    )(x.reshape(t, d), g.reshape(1, d), w_qkv_bf16, q_gain, k_gain, cos, sin)

    def attention_call(body, buf_dtype, name):
        return pl.pallas_call(
            functools.partial(body, tq=tq, tk=tk),
            grid=(b, N_KV_HEADS),
            in_specs=[
                pl.BlockSpec((None, s, GROUP * HEAD_DIM), lambda bi, hi: (bi, 0, hi)),
                pl.BlockSpec((None, s, HEAD_DIM), lambda bi, hi: (bi, 0, hi)),
                pl.BlockSpec((None, s // tk, HEAD_DIM, tk), lambda bi, hi: (bi, 0, hi, 0)),
            ],
            out_specs=pl.BlockSpec((None, s, GROUP * HEAD_DIM), lambda bi, hi: (bi, 0, hi)),
            out_shape=jax.ShapeDtypeStruct((b, s, Q_DIM), BF16),
            scratch_shapes=[
                pltpu.VMEM((tk, GROUP * tq), buf_dtype),
                pltpu.VMEM((tk, GROUP * tq), buf_dtype),
                pltpu.VMEM((HEAD_DIM, GROUP * tq), F32),
            ],
            compiler_params=pltpu.CompilerParams(
                dimension_semantics=("parallel", "parallel"), vmem_limit_bytes=VMEM_LIMIT),
            name=name,
        )

    score_bound = (HEAD_DIM * ATTN_Q_SCALE) * jnp.max(jnp.abs(q_gain)) * jnp.max(jnp.abs(k_gain))
    o = lax.cond(
        score_bound <= ATTN_BOUNDED_MAX_LOG2_SCORE,
        attention_call(_attn_bounded_kernel, BF16, "attention_bounded"),
        attention_call(_attn_kernel, F32, "attention"),
        q.reshape(b, s, Q_DIM), k.reshape(b, s, KV_DIM), vt.reshape(b, s // tk, KV_DIM, tk))
    return o.reshape(t, Q_DIM)


def _ffn_kernel(*refs, with_attn):
    if with_attn:
        x_ref, a_ref, wo_ref, g_ref, wg_ref, wu_ref, wd_ref, o_ref = refs
        x = x_ref[...] + jnp.dot(a_ref[...], wo_ref[...], preferred_element_type=F32)
    else:
        x_ref, delta_ref, g_ref, wg_ref, wu_ref, wd_ref, o_ref = refs
        x = x_ref[...] + delta_ref[...].astype(F32)
    r = lax.rsqrt(jnp.mean(x * x, axis=-1, keepdims=True) + EPS)
    xg = (x * g_ref[...]).astype(BF16)
    d_ff = wg_ref.shape[1]
    out = x
    for c0 in range(0, d_ff, FFN_HIDDEN_CHUNK):
        c1 = min(c0 + FFN_HIDDEN_CHUNK, d_ff)
        gate = jnp.dot(xg, wg_ref[:, c0:c1], preferred_element_type=F32) * r
        up = jnp.dot(xg, wu_ref[:, c0:c1], preferred_element_type=F32) * r
        act = (gate * jax.nn.sigmoid(gate) * up).astype(BF16)
        out = out + jnp.dot(act, wd_ref[c0:c1, :], preferred_element_type=F32)
    o_ref[...] = out


def _ffn_layer(x2d, g, layer, wg, wu, wd, attn=None, delta=None):
    assert (attn is None) != (delta is None)
    t, d = x2d.shape
    tm = FFN_TOKEN_TILE
    resident = functools.partial(pl.BlockSpec, index_map=lambda i: (0, 0), pipeline_mode=pl.Buffered(1))
    row = lambda width: pl.BlockSpec((tm, width), lambda i: (i, 0))
    in_specs = [row(d)]
    args = [x2d]
    if attn is not None:
        a2d, wo, attn_layer = attn
        in_specs += [row(Q_DIM), _layer_spec(wo, attn_layer, single_buffer=True)]
        args += [a2d, wo]
    else:
        in_specs += [row(d)]
        args += [delta]
    in_specs += [resident((1, d))] + [_layer_spec(w, layer, single_buffer=True) for w in (wg, wu, wd)]
    args += [g.reshape(1, d), wg, wu, wd]
    return pl.pallas_call(
        functools.partial(_ffn_kernel, with_attn=attn is not None),
        grid=(t // tm,),
        in_specs=in_specs,
        out_specs=row(d),
        out_shape=jax.ShapeDtypeStruct((t, d), F32),
        compiler_params=pltpu.CompilerParams(
            dimension_semantics=("parallel",), vmem_limit_bytes=VMEM_LIMIT),
        name="swiglu",
    )(*args)


def _with_pair_partner(gain):
    n_layers = gain.shape[0]
    partner = gain.reshape(n_layers, HEAD_DIM // 2, 2)[:, :, ::-1].reshape(n_layers, HEAD_DIM)
    return jnp.stack([gain, partner], axis=1)


def kernel(x_prompt, x_sample, norm_mix, norm_ffn, fourier_w, attn_w_qkv, attn_q_norm, attn_k_norm,
           attn_w_o, ffn_w_gate, ffn_w_up, ffn_w_down):
    depth = norm_mix.shape[0]
    fw = fourier_w.astype(BF16)
    wqkv = attn_w_qkv.astype(BF16)
    qn = _with_pair_partner(attn_q_norm)
    kn = _with_pair_partner(attn_k_norm)
    wo = attn_w_o.astype(BF16)
    wg = ffn_w_gate.astype(BF16)
    wu = ffn_w_up.astype(BF16)
    wd = ffn_w_down.astype(BF16)

    def run_trunk(x):
        b, s, d = x.shape
        for i in range(depth):
            j = i // 2
            if i % 2 == 0:
                delta = _fourier_layer(x, norm_mix[i], fw, j)
                x2d = _ffn_layer(x.reshape(b * s, d), norm_ffn[i], i, wg, wu, wd, delta=delta)
            else:
                a2d = _attention_qkv(x, norm_mix[i], wqkv, j, qn[j], kn[j])
                x2d = _ffn_layer(x.reshape(b * s, d), norm_ffn[i], i, wg, wu, wd, attn=(a2d, wo, j))
            x = x2d.reshape(b, s, d)
        return x

    return (run_trunk(x_prompt), run_trunk(x_sample))
```
